```python
import jax, jax.numpy as jnp
from jax import lax
import numpy as np

D_MODEL = 1024
BATCH = 2
SEQ = 8192
DEPTH = 1
DEC_BATCH = 8
DEC_SEQ = 2048
PAST_LEN = 128

PLE_DIM = 256
MIX_W = D_MODEL
RET_HEADS = 4
MLSTM_HEADS = 4
RET_W = MIX_W // 2
MLSTM_W = MIX_W - RET_W
RET_HD = RET_W // RET_HEADS
MLSTM_HD = MLSTM_W // MLSTM_HEADS
N_GATE_COLS = 4 * MLSTM_HEADS
IN_COLS = 4 * RET_W + 4 * MLSTM_W + N_GATE_COLS
D_FF = ((8 * D_MODEL // 3) + 127) // 128 * 128
CHUNK = 128
CONV_W = 3
ROPE_BASE = 10000.0
RMS_EPS = 1e-6
HN_EPS = 1e-5

kernel_name = 'bidir_retention_mlstm_hybrid'


def rmsnorm(x, w):
    xf = x.astype(jnp.float32)
    y = xf * lax.rsqrt(jnp.mean(xf * xf, axis=-1, keepdims=True) + RMS_EPS)
    return (y * w.astype(jnp.float32)).astype(x.dtype)


def head_layernorm(x, w, n_heads):
    B, S, C = x.shape
    xh = x.reshape(B, S, n_heads, C // n_heads)
    mu = jnp.mean(xh, axis=-1, keepdims=True)
    var = jnp.mean(jnp.square(xh - mu), axis=-1, keepdims=True)
    y = ((xh - mu) * lax.rsqrt(var + HN_EPS)).reshape(B, S, C)
    return y * w.astype(jnp.float32)


def dwconv_centred(x, w, b):
    xp = jnp.pad(x, ((0, 0), (1, 1), (0, 0)))
    return xp[:, :-2] * w[0] + xp[:, 1:-1] * w[1] + xp[:, 2:] * w[2] + b


def rotary(x):
    S, d = x.shape[1], x.shape[3]
    inv = ROPE_BASE ** (-jnp.arange(0, d, 2, dtype=jnp.float32) / d)
    ang = jnp.arange(S, dtype=jnp.float32)[:, None] * inv[None, :]
    cos = jnp.concatenate([jnp.cos(ang), jnp.cos(ang)], axis=-1)[None, :, None, :]
    sin = jnp.concatenate([jnp.sin(ang), jnp.sin(ang)], axis=-1)[None, :, None, :]
    x1, x2 = x[..., : d // 2], x[..., d // 2:]
    rot = jnp.concatenate([-x2, x1], axis=-1)
    return x * cos + rot * sin


def to_heads(t, n_heads):
    B, S, C = t.shape
    return t.reshape(B, S, n_heads, C // n_heads).transpose(0, 2, 1, 3)


def from_heads(t):
    B, H, S, d = t.shape
    return t.transpose(0, 2, 1, 3).reshape(B, S, H * d)


def flip_seq(t):
    return jnp.flip(t, axis=2)


def retention_dir(q, k, v, log_gamma, strict):
    B, H, S, d = q.shape
    nc, L = S // CHUNK, CHUNK
    qc = q.reshape(B, H, nc, L, d)
    kc = k.reshape(B, H, nc, L, d)
    vc = v.reshape(B, H, nc, L, v.shape[-1])
    idx = jnp.arange(L, dtype=jnp.float32)
    diff = idx[:, None] - idx[None, :]
    mask = (diff > 0) if strict else (diff >= 0)
    decay_in = jnp.where(mask[None], jnp.exp(jnp.maximum(diff, 0.0)[None] * log_gamma[:, None, None]), 0.0)
    scores = jnp.einsum('bhcid,bhcjd->bhcij', qc, kc) * decay_in[:, None]
    inner = jnp.einsum('bhcij,bhcje->bhcie', scores, vc)
    k_decay = jnp.exp((L - 1 - idx)[None, :] * log_gamma[:, None])
    upd = jnp.einsum('bhcjd,bhcje->cbhde', kc * k_decay[:, None, :, None], vc)
    chunk_decay = jnp.exp(L * log_gamma)[:, None, None]

    def step(state, u):
        return chunk_decay * state + u, state

    _, s_prev = lax.scan(step, jnp.zeros(upd.shape[1:], jnp.float32), upd)
    q_decay = jnp.exp((idx + 1.0)[None, :] * log_gamma[:, None])
    cross = jnp.einsum('bhcid,cbhde->bhcie', qc * q_decay[:, None, :, None], s_prev)
    return (inner + cross).reshape(B, H, S, vc.shape[-1])


def mlstm_dir(q, k, v, ig, lf):
    B, H, S, d = q.shape
    nc, L = S // CHUNK, CHUNK

    def to_chunks(t):
        return jnp.moveaxis(t.reshape((B, H, nc, L) + t.shape[3:]), 2, 0)

    xs = (to_chunks(q), to_chunks(k), to_chunks(v), to_chunks(ig), to_chunks(lf))
    causal = jnp.tril(jnp.ones((L, L), dtype=bool))

    def step(carry, inp):
        C, n, m = carry
        qc, kc, vc, igc, lfc = inp
        a = jnp.cumsum(lfc, axis=-1)
        log_d = jnp.where(causal, a[..., :, None] - a[..., None, :] + igc[..., None, :], -jnp.inf)
        m_inter = a + m[..., None]
        m_t = jnp.maximum(m_inter, jnp.max(log_d, axis=-1))
        s = jnp.einsum('bhid,bhjd->bhij', qc, kc) * jnp.exp(log_d - m_t[..., None])
        w_inter = jnp.exp(m_inter - m_t)
        num = w_inter[..., None] * jnp.einsum('bhid,bhde->bhie', qc, C) + jnp.einsum('bhij,bhje->bhie', s, vc)
        den = w_inter * jnp.einsum('bhid,bhd->bhi', qc, n) + jnp.sum(s, axis=-1)
        h = num / jnp.maximum(jnp.abs(den), jnp.exp(-m_t))[..., None]
        a_last = a[..., -1]
        log_w = a_last[..., None] - a + igc
        m_new = jnp.maximum(a_last + m, jnp.max(log_w, axis=-1))
        decay = jnp.exp(a_last + m - m_new)
        kw = kc * jnp.exp(log_w - m_new[..., None])[..., None]
        C_new = decay[..., None, None] * C + jnp.einsum('bhjd,bhje->bhde', kw, vc)
        n_new = decay[..., None] * n + jnp.sum(kw, axis=-2)
        return (C_new, n_new, m_new), h

    init = (jnp.zeros((B, H, d, v.shape[-1]), jnp.float32), jnp.zeros((B, H, d), jnp.float32), jnp.zeros((B, H), jnp.float32))
    _, hs = lax.scan(step, init, xs)
    return jnp.moveaxis(hs, 0, 2).reshape(B, H, S, v.shape[-1])


def token_mixers(a, w_in, mlstm_conv_w, mlstm_conv_b, mlstm_gate_b, ret_decay_logit, ret_gn_w, mlstm_gn_w, w_out):
    B, S, _ = a.shape
    f32 = jnp.float32
    proj = (a @ w_in).astype(f32)
    splits = [RET_W, 2 * RET_W, 3 * RET_W, 4 * RET_W, 4 * RET_W + 2 * MLSTM_W, 4 * RET_W + 3 * MLSTM_W, 4 * RET_W + 4 * MLSTM_W]
    rq, rk, rv, rg, mqk, mv, mo, mgates = jnp.split(proj, splits, axis=-1)

    rq = rotary(rq.reshape(B, S, RET_HEADS, RET_HD)).transpose(0, 2, 1, 3) * (RET_HD ** -0.5)
    rk = rotary(rk.reshape(B, S, RET_HEADS, RET_HD)).transpose(0, 2, 1, 3)
    rv = to_heads(rv, RET_HEADS)
    lg = jax.nn.log_sigmoid(ret_decay_logit.astype(f32))
    ret = retention_dir(rq, rk, rv, lg[0], False) + flip_seq(retention_dir(flip_seq(rq), flip_seq(rk), flip_seq(rv), lg[1], True))
    ret = head_layernorm(from_heads(ret), ret_gn_w, RET_HEADS) * jax.nn.silu(rg)

    mqk = jax.nn.silu(dwconv_centred(mqk, mlstm_conv_w.astype(f32), mlstm_conv_b.astype(f32)))
    mq, mk = jnp.split(mqk, 2, axis=-1)
    mq = to_heads(mq, MLSTM_HEADS) * (MLSTM_HD ** -0.5)
    mk = to_heads(mk, MLSTM_HEADS)
    mv = to_heads(mv, MLSTM_HEADS)
    g = (mgates.reshape(B, S, 4, MLSTM_HEADS) + mlstm_gate_b.astype(f32)).transpose(2, 0, 3, 1)
    ig_f, ig_b = g[0], g[1]
    lf_f, lf_b = jax.nn.log_sigmoid(g[2]), jax.nn.log_sigmoid(g[3])
    mh = mlstm_dir(mq, mk, mv, ig_f, lf_f) + flip_seq(mlstm_dir(flip_seq(mq), flip_seq(mk), flip_seq(mv), flip_seq(ig_b), flip_seq(lf_b)))
    mh = head_layernorm(from_heads(mh), mlstm_gn_w, MLSTM_HEADS) * jax.nn.sigmoid(mo)

    mixed = jnp.concatenate([ret, mh], axis=-1).astype(a.dtype)
    return mixed @ w_out


def trunk(h, p, params):
    (norm_mix_w, w_in, mlstm_conv_w, mlstm_conv_b, mlstm_gate_b, ret_decay_logit, ret_gn_w, mlstm_gn_w, w_out,
     norm_ffn_w, ffn_w_gate, ffn_w_up, ffn_conv_w, ffn_conv_b, ffn_w_down,
     ple_w_proj, ple_norm_w, ple_w_gate, ple_gate_b, final_norm_w) = params
    for l in range(DEPTH):
        a = rmsnorm(h, norm_mix_w[l])
        h = h + token_mixers(a, w_in[l], mlstm_conv_w[l], mlstm_conv_b[l], mlstm_gate_b[l], ret_decay_logit[l], ret_gn_w[l], mlstm_gn_w[l], w_out[l])
        f = rmsnorm(h, norm_ffn_w[l])
        u = dwconv_centred(f @ ffn_w_gate[l], ffn_conv_w[l], ffn_conv_b[l])
        h = h + (jax.nn.gelu(u, approximate=False) * (f @ ffn_w_up[l])) @ ffn_w_down[l]
        gate = jax.nn.sigmoid(rmsnorm(h, ple_norm_w[l]) @ ple_w_gate[l] + ple_gate_b[l])
        h = h + (p[l] @ ple_w_proj[l]) * gate
    return rmsnorm(h, final_norm_w)


def setup_inputs(seed: int = 0) -> dict:
    key = jax.random.key(seed)
    ks = jax.random.split(key, 24)
    nrm = jax.random.normal
    f32 = jnp.float32
    gam = 1.0 - 2.0 ** (-5.0 - np.arange(RET_HEADS, dtype=np.float32))
    logit0 = jnp.asarray(np.log(gam / (1.0 - gam)), dtype=f32)
    i_bias = 0.1 * nrm(ks[6], (DEPTH, 2, MLSTM_HEADS), f32)
    f_bias = jnp.linspace(3.0, 6.0, MLSTM_HEADS, dtype=f32) + 0.1 * nrm(ks[7], (DEPTH, 2, MLSTM_HEADS), f32)
    return {
        'x_prompt': nrm(ks[0], (BATCH, SEQ, D_MODEL), f32),
        'x_sample': nrm(ks[1], (DEC_BATCH, DEC_SEQ, D_MODEL), f32),
        'p_prompt': nrm(ks[2], (DEPTH, BATCH, SEQ, PLE_DIM), f32),
        'p_sample': nrm(ks[3], (DEPTH, DEC_BATCH, DEC_SEQ, PLE_DIM), f32),
        'norm_mix_w': 1.0 + 0.02 * nrm(ks[4], (DEPTH, D_MODEL), f32),
        'w_in': nrm(ks[5], (DEPTH, D_MODEL, IN_COLS), f32) * D_MODEL ** -0.5,
        'mlstm_conv_w': nrm(ks[8], (DEPTH, CONV_W, 2 * MLSTM_W), f32) * CONV_W ** -0.5,
        'mlstm_conv_b': 0.02 * nrm(ks[9], (DEPTH, 2 * MLSTM_W), f32),
        'mlstm_gate_b': jnp.concatenate([i_bias, f_bias], axis=1),
        'ret_decay_logit': logit0 + 0.05 * nrm(ks[10], (DEPTH, 2, RET_HEADS), f32),
        'ret_gn_w': 1.0 + 0.02 * nrm(ks[11], (DEPTH, RET_W), f32),
        'mlstm_gn_w': 1.0 + 0.02 * nrm(ks[12], (DEPTH, MLSTM_W), f32),
        'w_out': nrm(ks[13], (DEPTH, MIX_W, D_MODEL), f32) * MIX_W ** -0.5,
        'norm_ffn_w': 1.0 + 0.02 * nrm(ks[14], (DEPTH, D_MODEL), f32),
        'ffn_w_gate': nrm(ks[15], (DEPTH, D_MODEL, D_FF), f32) * D_MODEL ** -0.5,
        'ffn_w_up': nrm(ks[16], (DEPTH, D_MODEL, D_FF), f32) * D_MODEL ** -0.5,
        'ffn_conv_w': nrm(ks[17], (DEPTH, CONV_W, D_FF), f32) * CONV_W ** -0.5,
        'ffn_conv_b': 0.02 * nrm(ks[18], (DEPTH, D_FF), f32),
        'ffn_w_down': nrm(ks[19], (DEPTH, D_FF, D_MODEL), f32) * D_FF ** -0.5,
        'ple_w_proj': nrm(ks[20], (DEPTH, PLE_DIM, D_MODEL), f32) * PLE_DIM ** -0.5,
        'ple_norm_w': 1.0 + 0.02 * nrm(ks[21], (DEPTH, D_MODEL), f32),
        'ple_w_gate': nrm(ks[22], (DEPTH, D_MODEL, D_MODEL), f32) * D_MODEL ** -0.5,
        'ple_gate_b': 0.02 * nrm(ks[23], (DEPTH, D_MODEL), f32),
        'final_norm_w': 1.0 + 0.02 * nrm(jax.random.fold_in(key, 99), (D_MODEL,), f32),
    }


def reference(x_prompt, x_sample, p_prompt, p_sample, norm_mix_w, w_in, mlstm_conv_w, mlstm_conv_b, mlstm_gate_b,
              ret_decay_logit, ret_gn_w, mlstm_gn_w, w_out, norm_ffn_w, ffn_w_gate, ffn_w_up, ffn_conv_w, ffn_conv_b,
              ffn_w_down, ple_w_proj, ple_norm_w, ple_w_gate, ple_gate_b, final_norm_w):
    params = (norm_mix_w, w_in, mlstm_conv_w, mlstm_conv_b, mlstm_gate_b, ret_decay_logit, ret_gn_w, mlstm_gn_w, w_out,
              norm_ffn_w, ffn_w_gate, ffn_w_up, ffn_conv_w, ffn_conv_b, ffn_w_down,
              ple_w_proj, ple_norm_w, ple_w_gate, ple_gate_b, final_norm_w)
    y_prompt = trunk(x_prompt, p_prompt, params)
    y_sample = trunk(x_sample, p_sample, params)
    return (y_prompt, y_sample)
```

```python
import functools

import jax
import jax.numpy as jnp
from jax import lax
from jax.experimental import pallas as pl
from jax.experimental.pallas import tpu as pltpu

F32 = jnp.float32
BF16 = jnp.bfloat16

HEADS = 4
HD = 128
GW = HEADS * HD
CHUNK = 128
ROPE_BASE = 10000.0
RMS_EPS = 1e-6
HN_EPS = 1e-5
SUBLANES = 8
LANES = 128
ROW_TILE = 512
FF_BLOCK = 256
VMEM_LIMIT = 56 * 1024 * 1024


def _rms(x, w):
    return x * lax.rsqrt(jnp.mean(x * x, axis=-1, keepdims=True) + RMS_EPS) * w


def _log_sigmoid(x):
    return jnp.minimum(x, 0.0) - jnp.log1p(jnp.exp(-jnp.abs(x)))


def _sigmoid(x):
    return 1.0 / (1.0 + jnp.exp(-x))


def _gelu(x):
    return 0.5 * x * (1.0 + lax.erf(x * (0.5 ** 0.5)))


def _dot(a, b):
    return jnp.dot(a, b, preferred_element_type=F32)


def _dot_nt(a, b):
    return lax.dot_general(a, b, (((1,), (1,)), ((), ())), preferred_element_type=F32)


def _ext_rows(main, nxt, prv, tin, tiles_per_seq):
    nxt = jnp.where(tin == tiles_per_seq - 1, 0.0, nxt)
    prv = jnp.where(tin == 0, 0.0, prv)
    return jnp.concatenate([main, nxt, prv], axis=0)


def _conv3(ext, rows, w, b):
    n = ext.shape[0]
    up = pltpu.roll(ext, 1, axis=0)[:rows]
    dn = pltpu.roll(ext, n - 1, axis=0)[:rows]
    return up * w[0:1] + ext[:rows] * w[1:2] + dn * w[2:3] + b


def _in_proj_kernel(x_ref, xn_ref, xp_ref, nw_ref, inv_ref, wr_ref, wqk_ref, wm_ref, wg_ref, gb_ref, cw_ref,
                    cb_ref, qkv_ref, og_ref, ig_ref, lf_ref, *, tiles_per_seq):
    tm = x_ref.shape[0]
    tin = pl.program_id(0) % tiles_per_seq
    nw = nw_ref[...]
    a_ext = _ext_rows(_rms(x_ref[...], nw), _rms(xn_ref[...], nw), _rms(xp_ref[...], nw), tin,
                      tiles_per_seq).astype(BF16)
    a = a_ext[:tm]

    pos = (tin * tm + lax.broadcasted_iota(jnp.int32, (tm, HD), 0)).astype(F32)
    ang = pos * inv_ref[...]
    cos = jnp.cos(ang)
    sin = jnp.sin(ang)
    lane = lax.broadcasted_iota(jnp.int32, (tm, HD), 1)
    sin_rot = jnp.where(lane < HD // 2, -sin, sin)

    scale = HD ** -0.5
    for part in range(2):
        t = _dot(a, wr_ref[:, part * GW:(part + 1) * GW])
        for h in range(HEADS):
            th = t[:, h * HD:(h + 1) * HD]
            r = th * cos + pltpu.roll(th, HD // 2, axis=1) * sin_rot
            if part == 0:
                r = r * scale
            qkv_ref[:, part * GW + h * HD:part * GW + (h + 1) * HD] = r.astype(BF16)
    qkv_ref[:, 2 * GW:3 * GW] = _dot(a, wr_ref[:, 2 * GW:3 * GW]).astype(BF16)
    og_ref[:, 0:GW] = _dot(a, wr_ref[:, 3 * GW:4 * GW]).astype(BF16)

    c = _conv3(_dot(a_ext, wqk_ref[...]), tm, cw_ref[...], cb_ref[...])
    c = c * _sigmoid(c)
    qkv_ref[:, 3 * GW:4 * GW] = (c[:, :GW] * scale).astype(BF16)
    qkv_ref[:, 4 * GW:5 * GW] = c[:, GW:].astype(BF16)
    qkv_ref[:, 5 * GW:6 * GW] = _dot(a, wm_ref[:, 0:GW]).astype(BF16)
    og_ref[:, GW:2 * GW] = _dot(a, wm_ref[:, GW:2 * GW]).astype(BF16)

    g = _dot(a, wg_ref[...]) + gb_ref[...]
    ig_ref[...] = g[:, :LANES]
    lf_ref[...] = _log_sigmoid(g[:, LANES:])


def _mixer_kernel(lg_ref, qf_ref, qb_ref, igf_ref, lff_ref, igb_ref, lfb_ref, of_ref, ob_ref,
                  dm_ref, qdf_ref, qdb_ref, kdf_ref, kdb_ref, cd_ref, sf_ref, sb_ref, c_ref, m_ref):
    L = CHUNK
    ii = lax.broadcasted_iota(jnp.int32, (L, L), 0)
    jj = lax.broadcasted_iota(jnp.int32, (L, L), 1)

    @pl.when(pl.program_id(1) == 0)
    def _init():
        lg = _log_sigmoid(lg_ref[...])
        lgf, lgb = lg[0:1], lg[1:2]
        row = lax.broadcasted_iota(jnp.int32, (L, GW), 0).astype(F32)
        qdf_ref[...] = jnp.exp((row + 1.0) * lgf)
        qdb_ref[...] = jnp.exp((L - row) * lgb)
        kdf_ref[...] = jnp.exp((L - 1.0 - row) * lgf)
        kdb_ref[...] = jnp.exp(row * lgb)
        cd_ref[...] = jnp.exp(float(L) * lg)
        diff = (ii - jj).astype(F32)
        for h in range(HEADS):
            sl = slice(h * HD, (h + 1) * HD)
            dm_ref[h] = jnp.where(diff >= 0, jnp.exp(diff * lgf[:, sl]), jnp.exp(-diff * lgb[:, sl]))
        sf_ref[...] = jnp.zeros_like(sf_ref)
        sb_ref[...] = jnp.zeros_like(sb_ref)
        c_ref[...] = jnp.zeros_like(c_ref)
        m_ref[...] = jnp.zeros_like(m_ref)

    for h in range(HEADS):
        sl = slice(h * HD, (h + 1) * HD)
        q = qf_ref[:, sl]
        k = qf_ref[:, GW + h * HD:GW + (h + 1) * HD]
        v = qf_ref[:, 2 * GW + h * HD:2 * GW + (h + 1) * HD]
        p = (_dot_nt(q, k) * dm_ref[h]).astype(BF16)
        s_f = sf_ref[h]
        of_ref[:, sl] = _dot(p, v) + _dot(q, s_f.astype(BF16)) * qdf_ref[:, sl]
        kd = (k.astype(F32) * kdf_ref[:, sl]).T.astype(BF16)
        sf_ref[h] = cd_ref[0:1, sl] * s_f + _dot(kd, v)

        q = qb_ref[:, sl]
        k = qb_ref[:, GW + h * HD:GW + (h + 1) * HD]
        v = qb_ref[:, 2 * GW + h * HD:2 * GW + (h + 1) * HD]
        s_b = sb_ref[h]
        ob_ref[:, sl] = _dot(q, s_b.astype(BF16)) * qdb_ref[:, sl]
        kd = (k.astype(F32) * kdb_ref[:, sl]).T.astype(BF16)
        sb_ref[h] = cd_ref[1:2, sl] * s_b + _dot(kd, v)

    tril = ii >= jj
    triu = jj >= ii
    hi = lax.Precision.HIGHEST
    a_f = jnp.dot(tril.astype(F32), lff_ref[...], precision=hi, preferred_element_type=F32)
    a_b = jnp.dot(triu.astype(F32), lfb_ref[...], precision=hi, preferred_element_type=F32)
    is_f = jj < HEADS
    a_all = jnp.where(is_f, a_f, a_b)
    b_all = jnp.where(is_f, igf_ref[...], igb_ref[...]) - a_all
    b_rows = b_all.T
    one_col = (jj == 0).astype(BF16)

    for c in range(2 * HEADS):
        h, backward = c % HEADS, c >= HEADS
        src, dst = (qb_ref, ob_ref) if backward else (qf_ref, of_ref)
        last = 0 if backward else L - 1
        q = src[:, 3 * GW + h * HD:3 * GW + (h + 1) * HD]
        k = src[:, 4 * GW + h * HD:4 * GW + (h + 1) * HD]
        v = src[:, 5 * GW + h * HD:5 * GW + (h + 1) * HD]
        v1 = jnp.concatenate([v, one_col], axis=1)

        bm = jnp.where(triu if backward else tril, b_rows[c:c + 1, :], -jnp.inf)
        m_old = m_ref[c:c + 1, 0:1]
        mu = jnp.maximum(jnp.max(bm, axis=1, keepdims=True), m_old)
        s = (_dot_nt(q, k) * jnp.exp(bm - mu)).astype(BF16)
        cn = c_ref[c]
        num = jnp.exp(m_old - mu) * _dot(q, cn.astype(BF16)) + _dot(s, v1)
        den = jnp.maximum(jnp.abs(num[:, HD:HD + 1]), jnp.exp(-(a_all[:, c:c + 1] + mu)))
        dst[:, GW + h * HD:GW + (h + 1) * HD] = num[:, :HD] / den

        mu_last = mu[last:last + 1, :]
        kw = (k.astype(F32) * jnp.exp(b_all[:, c:c + 1] - mu_last)).T.astype(BF16)
        c_ref[c] = jnp.exp(m_old - mu_last) * cn + _dot(kw, v1)
        m_ref[c:c + 1, :] = jnp.broadcast_to(a_all[last:last + 1, c:c + 1] + mu_last, (1, LANES))


def _mix_out_kernel(of_ref, ob_ref, og_ref, x_ref, gn_ref, wo_ref, h_ref):
    y = of_ref[...] + ob_ref[...]
    parts = []
    for g in range(2 * HEADS):
        sl = slice(g * HD, (g + 1) * HD)
        yg = y[:, sl]
        d = yg - jnp.mean(yg, axis=-1, keepdims=True)
        n = d * lax.rsqrt(jnp.mean(d * d, axis=-1, keepdims=True) + HN_EPS) * gn_ref[:, sl]
        z = og_ref[:, sl].astype(F32)
        gate = z * _sigmoid(z) if g < HEADS else _sigmoid(z)
        parts.append((n * gate).astype(BF16))
    mixed = jnp.concatenate(parts, axis=1)
    h_ref[...] = x_ref[...] + _dot(mixed, wo_ref[...])


def _ffn_kernel(h_ref, hn_ref, hp_ref, p_ref, nfw_ref, wg_ref, wu_ref, cw_ref, cb_ref, wd_ref, pnw_ref, pwg_ref,
                pgb_ref, pwp_ref, fnw_ref, y_ref, *, tiles_per_seq):
    tm = h_ref.shape[0]
    tin = pl.program_id(0) % tiles_per_seq
    h = h_ref[...]
    nfw = nfw_ref[...]
    f_ext = _ext_rows(_rms(h, nfw), _rms(hn_ref[...], nfw), _rms(hp_ref[...], nfw), tin,
                      tiles_per_seq).astype(BF16)
    f = f_ext[:tm]
    acc = jnp.zeros(h.shape, F32)
    for j in range(wg_ref.shape[0]):
        u = _conv3(_dot(f_ext, wg_ref[j]), tm, cw_ref[j], cb_ref[j])
        act = _gelu(u) * _dot(f, wu_ref[j])
        acc = acc + _dot(act.astype(BF16), wd_ref[j])
    h = h + acc
    gate = _sigmoid(_dot(_rms(h, pnw_ref[...]).astype(BF16), pwg_ref[...]) + pgb_ref[...])
    h = h + _dot(p_ref[...].astype(BF16), pwp_ref[...]) * gate
    y_ref[...] = _rms(h, fnw_ref[...])


def _const_spec(shape):
    nd = len(shape)
    return pl.BlockSpec(shape, lambda *_: (0,) * nd, pipeline_mode=pl.Buffered(1))


def _row_specs(tm, width, n_rows):
    per = tm // SUBLANES
    last = n_rows // SUBLANES - 1
    return [
        pl.BlockSpec((tm, width), lambda i: (i, 0)),
        pl.BlockSpec((SUBLANES, width), lambda i: (jnp.minimum((i + 1) * per, last), 0)),
        pl.BlockSpec((SUBLANES, width), lambda i: (jnp.maximum(i * per - 1, 0), 0)),
    ]


def _params(sem):
    return pltpu.CompilerParams(dimension_semantics=sem, vmem_limit_bytes=VMEM_LIMIT)


def _layer(x, p, w):
    bsz, seq, d = x.shape
    n = bsz * seq
    tm = ROW_TILE
    assert seq % tm == 0 and seq % CHUNK == 0
    tiles_per_seq = seq // tm
    x2 = x.reshape(n, d)
    row = lambda width: pl.BlockSpec((tm, width), lambda i: (i, 0))

    qkv, og, ig, lf = pl.pallas_call(
        functools.partial(_in_proj_kernel, tiles_per_seq=tiles_per_seq),
        grid=(n // tm,),
        in_specs=_row_specs(tm, d, n) + [_const_spec(a.shape) for a in w["in_proj"]],
        out_specs=[row(6 * GW), row(2 * GW), row(LANES), row(LANES)],
        out_shape=[jax.ShapeDtypeStruct((n, 6 * GW), BF16), jax.ShapeDtypeStruct((n, 2 * GW), BF16),
                   jax.ShapeDtypeStruct((n, LANES), F32), jax.ShapeDtypeStruct((n, LANES), F32)],
        compiler_params=_params(("parallel",)),
        name="in_proj",
    )(x2, x2, x2, *w["in_proj"])

    nc = seq // CHUNK
    fwd = lambda width: pl.BlockSpec((CHUNK, width), lambda b, t: (b * nc + t, 0))
    bwd = lambda width: pl.BlockSpec((CHUNK, width), lambda b, t: (b * nc + nc - 1 - t, 0))
    out_f, out_b = pl.pallas_call(
        _mixer_kernel,
        grid=(bsz, nc),
        in_specs=[_const_spec(w["ret_logit"].shape), fwd(6 * GW), bwd(6 * GW), fwd(LANES), fwd(LANES), bwd(LANES),
                  bwd(LANES)],
        out_specs=[fwd(2 * GW), bwd(2 * GW)],
        out_shape=[jax.ShapeDtypeStruct((n, 2 * GW), F32)] * 2,
        scratch_shapes=[
            pltpu.VMEM((HEADS, CHUNK, CHUNK), F32),
            pltpu.VMEM((CHUNK, GW), F32), pltpu.VMEM((CHUNK, GW), F32),
            pltpu.VMEM((CHUNK, GW), F32), pltpu.VMEM((CHUNK, GW), F32),
            pltpu.VMEM((2, GW), F32),
            pltpu.VMEM((HEADS, HD, HD), F32), pltpu.VMEM((HEADS, HD, HD), F32),
            pltpu.VMEM((2 * HEADS, HD, 2 * HD), F32),
            pltpu.VMEM((2 * HEADS, LANES), F32),
        ],
        compiler_params=_params(("arbitrary", "arbitrary")),
        name="mixers",
    )(w["ret_logit"], qkv, qkv, ig, lf, ig, lf)

    h1 = pl.pallas_call(
        _mix_out_kernel,
        grid=(n // tm,),
        in_specs=[row(2 * GW), row(2 * GW), row(2 * GW), row(d)] + [_const_spec(a.shape) for a in w["mix_out"]],
        out_specs=row(d),
        out_shape=jax.ShapeDtypeStruct((n, d), F32),
        compiler_params=_params(("parallel",)),
        name="mix_out",
    )(out_f, out_b, og, x2, *w["mix_out"])

    p2 = p.reshape(n, p.shape[-1])
    y = pl.pallas_call(
        functools.partial(_ffn_kernel, tiles_per_seq=tiles_per_seq),
        grid=(n // tm,),
        in_specs=_row_specs(tm, d, n) + [row(p2.shape[-1])] + [_const_spec(a.shape) for a in w["ffn"]],
        out_specs=row(d),
        out_shape=jax.ShapeDtypeStruct((n, d), F32),
        compiler_params=_params(("parallel",)),
        name="ffn",
    )(h1, h1, h1, p2, *w["ffn"])
    return y.reshape(bsz, seq, d)


def kernel(x_prompt, x_sample, p_prompt, p_sample, norm_mix_w, w_in, mlstm_conv_w, mlstm_conv_b, mlstm_gate_b, ret_decay_logit, ret_gn_w, mlstm_gn_w, w_out, norm_ffn_w, ffn_w_gate, ffn_w_up, ffn_conv_w, ffn_conv_b, ffn_w_down, ple_w_proj, ple_norm_w, ple_w_gate, ple_gate_b, final_norm_w):
    depth = w_in.shape[0]
    d = x_prompt.shape[-1]
    d_ff = ffn_w_gate.shape[-1]
    nblk = d_ff // FF_BLOCK
    assert d_ff % FF_BLOCK == 0
    rowv = lambda a: a.reshape(1, -1).astype(F32)
    inv = ROPE_BASE ** (-jnp.arange(0, HD, 2, dtype=F32) / HD)
    inv = jnp.concatenate([inv, inv]).reshape(1, HD)

    assert depth == 1
    outs = [x_prompt, x_sample]
    for l in range(depth):
        wi = w_in[l].astype(BF16)
        n_gate = 2 * HEADS
        wgate = jnp.zeros((d, 2 * LANES), BF16)
        wgate = wgate.at[:, 0:n_gate].set(wi[:, 8 * GW:8 * GW + n_gate])
        wgate = wgate.at[:, LANES:LANES + n_gate].set(wi[:, 8 * GW + n_gate:8 * GW + 2 * n_gate])
        gb = mlstm_gate_b[l].astype(F32).reshape(-1)
        gbias = jnp.zeros((1, 2 * LANES), F32)
        gbias = gbias.at[0, 0:n_gate].set(gb[:n_gate]).at[0, LANES:LANES + n_gate].set(gb[n_gate:])
        w = {
            "in_proj": [rowv(norm_mix_w[l]), inv, wi[:, 0:4 * GW], wi[:, 4 * GW:6 * GW], wi[:, 6 * GW:8 * GW], wgate,
                        gbias, mlstm_conv_w[l].astype(F32), rowv(mlstm_conv_b[l])],
            "ret_logit": jnp.repeat(ret_decay_logit[l].astype(F32), HD, axis=1),
            "mix_out": [jnp.concatenate([rowv(ret_gn_w[l]), rowv(mlstm_gn_w[l])], axis=1), w_out[l].astype(BF16)],
            "ffn": [rowv(norm_ffn_w[l]),
                    ffn_w_gate[l].astype(BF16).reshape(d, nblk, FF_BLOCK).transpose(1, 0, 2),
                    ffn_w_up[l].astype(BF16).reshape(d, nblk, FF_BLOCK).transpose(1, 0, 2),
                    ffn_conv_w[l].astype(F32).reshape(3, nblk, FF_BLOCK).transpose(1, 0, 2),
                    ffn_conv_b[l].astype(F32).reshape(nblk, 1, FF_BLOCK),
                    ffn_w_down[l].astype(BF16).reshape(nblk, FF_BLOCK, d),
                    rowv(ple_norm_w[l]), ple_w_gate[l].astype(BF16), rowv(ple_gate_b[l]),
                    ple_w_proj[l].astype(BF16), rowv(final_norm_w)],
        }
        outs = [_layer(h, p[l], w) for h, p in zip(outs, (p_prompt, p_sample))]
    return tuple(outs)
```

```python
import functools

import jax
import jax.numpy as jnp
from jax import lax
from jax.experimental import pallas as pl
from jax.experimental.pallas import tpu as pltpu

F32 = jnp.float32
BF16 = jnp.bfloat16

HEADS = 4
HD = 128
GW = HEADS * HD
CHUNK = 128
ROPE_BASE = 10000.0
RMS_EPS = 1e-6
HN_EPS = 1e-5
SUBLANES = 8
LANES = 128
ROW_TILE = 512
FF_BLOCK = 256
VMEM_LIMIT = 56 * 1024 * 1024


def _rms(x, w):
    return x * lax.rsqrt(jnp.mean(x * x, axis=-1, keepdims=True) + RMS_EPS) * w


def _log_sigmoid(x):
    return jnp.minimum(x, 0.0) - jnp.log1p(jnp.exp(-jnp.abs(x)))


def _sigmoid(x):
    return 1.0 / (1.0 + jnp.exp(-x))


def _gelu(x):
    return 0.5 * x * (1.0 + lax.erf(x * (0.5 ** 0.5)))


def _dot(a, b):
    return jnp.dot(a, b, preferred_element_type=F32)


def _dot_nt(a, b):
    return lax.dot_general(a, b, (((1,), (1,)), ((), ())), preferred_element_type=F32)


def _ext_rows(main, nxt, prv, tin, tiles_per_seq):
    nxt = jnp.where(tin == tiles_per_seq - 1, 0.0, nxt)
    prv = jnp.where(tin == 0, 0.0, prv)
    return jnp.concatenate([main, nxt, prv], axis=0)


def _conv3(ext, rows, w, b):
    n = ext.shape[0]
    up = pltpu.roll(ext, 1, axis=0)[:rows]
    dn = pltpu.roll(ext, n - 1, axis=0)[:rows]
    return up * w[0:1] + ext[:rows] * w[1:2] + dn * w[2:3] + b


def _chunk_stats(ig, lf):
    tm = ig.shape[0]
    ii = lax.broadcasted_iota(jnp.int32, (CHUNK, CHUNK), 0)
    jj = lax.broadcasted_iota(jnp.int32, (CHUNK, CHUNK), 1)
    tril = (ii >= jj).astype(F32)
    triu = (jj >= ii).astype(F32)
    hi = lax.Precision.HIGHEST
    a = jnp.concatenate([
        jnp.where(jj < HEADS,
                  jnp.dot(tril, lf[r:r + CHUNK], precision=hi, preferred_element_type=F32),
                  jnp.dot(triu, lf[r:r + CHUNK], precision=hi, preferred_element_type=F32))
        for r in range(0, tm, CHUNK)], axis=0)
    b = ig - a
    r = lax.broadcasted_iota(jnp.int32, (tm, LANES), 0) % CHUNK
    pre = suf = b
    s = 1
    while s < CHUNK:
        pre = jnp.maximum(pre, jnp.where(r >= s, pltpu.roll(pre, s, axis=0), -jnp.inf))
        suf = jnp.maximum(suf, jnp.where(r < CHUNK - s, pltpu.roll(suf, tm - s, axis=0), -jnp.inf))
        s *= 2
    cm = jnp.where(lax.broadcasted_iota(jnp.int32, (tm, LANES), 1) < HEADS, pre, suf)
    return a, b, cm


def _in_proj_kernel(x_ref, xn_ref, xp_ref, nw_ref, inv_ref, wr_ref, wqk_ref, wm_ref, wg_ref, gb_ref, cw_ref,
                    cb_ref, qv_ref, kt_ref, og_ref, ac_ref, cm_ref, br_ref, *, tiles_per_seq):
    tm = x_ref.shape[0]
    tin = pl.program_id(0) % tiles_per_seq
    nw = nw_ref[...]
    a_ext = _ext_rows(_rms(x_ref[...], nw), _rms(xn_ref[...], nw), _rms(xp_ref[...], nw), tin,
                      tiles_per_seq).astype(BF16)
    a = a_ext[:tm]

    pos = (tin * tm + lax.broadcasted_iota(jnp.int32, (tm, HD), 0)).astype(F32)
    ang = pos * inv_ref[...]
    cos = jnp.cos(ang)
    sin = jnp.sin(ang)
    lane = lax.broadcasted_iota(jnp.int32, (tm, HD), 1)
    sin_rot = jnp.where(lane < HD // 2, -sin, sin)

    scale = HD ** -0.5
    for part in range(2):
        t = _dot(a, wr_ref[:, part * GW:(part + 1) * GW])
        for h in range(HEADS):
            th = t[:, h * HD:(h + 1) * HD]
            r = th * cos + pltpu.roll(th, HD // 2, axis=1) * sin_rot
            if part == 0:
                qv_ref[:, h * HD:(h + 1) * HD] = (r * scale).astype(BF16)
            else:
                kt_ref[h * HD:(h + 1) * HD, :] = r.T.astype(BF16)
    qv_ref[:, GW:2 * GW] = _dot(a, wr_ref[:, 2 * GW:3 * GW]).astype(BF16)
    og_ref[:, 0:GW] = _dot(a, wr_ref[:, 3 * GW:4 * GW]).astype(BF16)

    c = _conv3(_dot(a_ext, wqk_ref[...]), tm, cw_ref[...], cb_ref[...])
    c = c * _sigmoid(c)
    qv_ref[:, 2 * GW:3 * GW] = (c[:, :GW] * scale).astype(BF16)
    for h in range(HEADS):
        kt_ref[GW + h * HD:GW + (h + 1) * HD, :] = c[:, GW + h * HD:GW + (h + 1) * HD].T.astype(BF16)
    qv_ref[:, 3 * GW:4 * GW] = _dot(a, wm_ref[:, 0:GW]).astype(BF16)
    og_ref[:, GW:2 * GW] = _dot(a, wm_ref[:, GW:2 * GW]).astype(BF16)

    g = _dot(a, wg_ref[...]) + gb_ref[...]
    acum, b, cmax = _chunk_stats(g[:, :LANES], _log_sigmoid(g[:, LANES:]))
    ac_ref[...] = acum
    cm_ref[...] = cmax
    for ch in range(tm // CHUNK):
        br_ref[ch * SUBLANES:(ch + 1) * SUBLANES, :] = b[ch * CHUNK:(ch + 1) * CHUNK].T[:SUBLANES]


def _mixer_kernel(lg_ref, qf_ref, ktf_ref, acf_ref, cmf_ref, brf_ref, qb_ref, ktb_ref, acb_ref, cmb_ref, brb_ref,
                  of_ref, ob_ref, dm_ref, qdf_ref, qdb_ref, kd_ref, cd_ref, sf_ref, sb_ref, c_ref, m_ref):
    L = CHUNK
    ii = lax.broadcasted_iota(jnp.int32, (L, L), 0)
    jj = lax.broadcasted_iota(jnp.int32, (L, L), 1)

    @pl.when(pl.program_id(1) == 0)
    def _init():
        lg = _log_sigmoid(lg_ref[...])
        lgf, lgb = lg[0:1], lg[1:2]
        row = lax.broadcasted_iota(jnp.int32, (L, GW), 0).astype(F32)
        qdf_ref[...] = jnp.exp((row + 1.0) * lgf)
        qdb_ref[...] = jnp.exp((L - row) * lgb)
        cd_ref[...] = jnp.exp(float(L) * lg)
        diff = (ii - jj).astype(F32)
        col = lax.broadcasted_iota(jnp.int32, (1, L), 1).astype(F32)
        for h in range(HEADS):
            sl = slice(h * HD, (h + 1) * HD)
            dm_ref[h] = jnp.where(diff >= 0, jnp.exp(diff * lgf[:, sl]), jnp.exp(-diff * lgb[:, sl]))
            kd_ref[h:h + 1, :] = jnp.exp((L - 1.0 - col) * lgf[:, sl])
            kd_ref[HEADS + h:HEADS + h + 1, :] = jnp.exp(col * lgb[:, sl])
        sf_ref[...] = jnp.zeros_like(sf_ref)
        sb_ref[...] = jnp.zeros_like(sb_ref)
        c_ref[...] = jnp.zeros_like(c_ref)
        m_ref[...] = jnp.zeros_like(m_ref)

    H = range(HEADS)
    sls = [slice(h * HD, (h + 1) * HD) for h in H]
    part = lambda ref, i, h: ref[:, i * GW + h * HD:i * GW + (h + 1) * HD]
    qf, vf = ([part(qf_ref, i, h) for h in H] for i in range(2))
    qb, vb = ([part(qb_ref, i, h) for h in H] for i in range(2))
    ktf = [ktf_ref[sls[h], :] for h in H]
    ktb = [ktb_ref[sls[h], :] for h in H]
    p = [(_dot(qf[h], ktf[h]) * dm_ref[h]).astype(BF16) for h in H]
    s_f = [sf_ref[h] for h in H]
    s_b = [sb_ref[h] for h in H]
    cross_f = [_dot(qf[h], s_f[h].astype(BF16)) for h in H]
    cross_b = [_dot(qb[h], s_b[h].astype(BF16)) for h in H]
    inner = [_dot(p[h], vf[h]) for h in H]
    for h in H:
        of_ref[:, sls[h]] = inner[h] + cross_f[h] * qdf_ref[:, sls[h]]
        ob_ref[:, sls[h]] = cross_b[h] * qdb_ref[:, sls[h]]
    upd_f = [_dot((ktf[h].astype(F32) * kd_ref[h:h + 1, :]).astype(BF16), vf[h]) for h in H]
    upd_b = [_dot((ktb[h].astype(F32) * kd_ref[HEADS + h:HEADS + h + 1, :]).astype(BF16), vb[h]) for h in H]
    for h in H:
        sf_ref[h] = cd_ref[0:1, sls[h]] * s_f[h] + upd_f[h]
        sb_ref[h] = cd_ref[1:2, sls[h]] * s_b[h] + upd_b[h]

    tril = ii >= jj
    triu = jj >= ii
    is_f = jj < HEADS
    a_all = jnp.where(is_f, acf_ref[...], acb_ref[...])
    cm_all = jnp.where(is_f, cmf_ref[...], cmb_ref[...])
    b_rows = jnp.where(lax.broadcasted_iota(jnp.int32, (SUBLANES, L), 0) < HEADS, brf_ref[...], brb_ref[...])
    one_col = (jj == 0).astype(BF16)

    C = range(2 * HEADS)
    srcs = [qb_ref if c >= HEADS else qf_ref for c in C]
    kts = [ktb_ref if c >= HEADS else ktf_ref for c in C]
    dsts = [ob_ref if c >= HEADS else of_ref for c in C]
    lasts = [0 if c >= HEADS else L - 1 for c in C]
    hs = [c % HEADS for c in C]
    q = [part(srcs[c], 2, hs[c]) for c in C]
    kt = [kts[c][GW + hs[c] * HD:GW + (hs[c] + 1) * HD, :] for c in C]
    v1 = [jnp.concatenate([part(srcs[c], 3, hs[c]), one_col], axis=1) for c in C]
    m_old = [m_ref[c:c + 1, 0:1] for c in C]
    mu = [jnp.maximum(cm_all[:, c:c + 1], m_old[c]) for c in C]
    e = [jnp.exp(jnp.where(triu if c >= HEADS else tril, b_rows[c:c + 1, :] - mu[c], -jnp.inf)) for c in C]
    qk = [_dot(q[c], kt[c]) for c in C]
    s = [(qk[c] * e[c]).astype(BF16) for c in C]
    cn = [c_ref[c] for c in C]
    qc = [_dot(q[c], cn[c].astype(BF16)) for c in C]
    sv = [_dot(s[c], v1[c]) for c in C]
    num = [jnp.exp(m_old[c] - mu[c]) * qc[c] + sv[c] for c in C]
    for c in C:
        den = jnp.maximum(jnp.abs(num[c][:, HD:HD + 1]), jnp.exp(-(a_all[:, c:c + 1] + mu[c])))
        dsts[c][:, GW + hs[c] * HD:GW + (hs[c] + 1) * HD] = num[c][:, :HD] / den
    mu_last = [mu[c][lasts[c]:lasts[c] + 1, :] for c in C]
    kw = [(kt[c].astype(F32) * jnp.exp(b_rows[c:c + 1, :] - mu_last[c])).astype(BF16) for c in C]
    upd = [_dot(kw[c], v1[c]) for c in C]
    for c in C:
        c_ref[c] = jnp.exp(m_old[c] - mu_last[c]) * cn[c] + upd[c]
        m_ref[c:c + 1, :] = jnp.broadcast_to(a_all[lasts[c]:lasts[c] + 1, c:c + 1] + mu_last[c], (1, LANES))


def _mix_out_kernel(of_ref, ob_ref, og_ref, x_ref, gn_ref, wo_ref, h_ref):
    y = of_ref[...] + ob_ref[...]
    parts = []
    for g in range(2 * HEADS):
        sl = slice(g * HD, (g + 1) * HD)
        yg = y[:, sl]
        d = yg - jnp.mean(yg, axis=-1, keepdims=True)
        n = d * lax.rsqrt(jnp.mean(d * d, axis=-1, keepdims=True) + HN_EPS) * gn_ref[:, sl]
        z = og_ref[:, sl].astype(F32)
        gate = z * _sigmoid(z) if g < HEADS else _sigmoid(z)
        parts.append((n * gate).astype(BF16))
    mixed = jnp.concatenate(parts, axis=1)
    h_ref[...] = x_ref[...] + _dot(mixed, wo_ref[...])


def _ffn_kernel(h_ref, hn_ref, hp_ref, p_ref, nfw_ref, wg_ref, wu_ref, cw_ref, cb_ref, wd_ref, pnw_ref, pwg_ref,
                pgb_ref, pwp_ref, fnw_ref, y_ref, *, tiles_per_seq):
    tm = h_ref.shape[0]
    tin = pl.program_id(0) % tiles_per_seq
    h = h_ref[...]
    nfw = nfw_ref[...]
    f_ext = _ext_rows(_rms(h, nfw), _rms(hn_ref[...], nfw), _rms(hp_ref[...], nfw), tin,
                      tiles_per_seq).astype(BF16)
    f = f_ext[:tm]
    acc = jnp.zeros(h.shape, F32)
    for j in range(wg_ref.shape[0]):
        u = _conv3(_dot(f_ext, wg_ref[j]), tm, cw_ref[j], cb_ref[j])
        act = _gelu(u) * _dot(f, wu_ref[j])
        acc = acc + _dot(act.astype(BF16), wd_ref[j])
    h = h + acc
    gate = _sigmoid(_dot(_rms(h, pnw_ref[...]).astype(BF16), pwg_ref[...]) + pgb_ref[...])
    h = h + _dot(p_ref[...].astype(BF16), pwp_ref[...]) * gate
    y_ref[...] = _rms(h, fnw_ref[...])


def _const_spec(shape):
    nd = len(shape)
    return pl.BlockSpec(shape, lambda *_: (0,) * nd, pipeline_mode=pl.Buffered(1))


def _row_specs(tm, width, n_rows):
    per = tm // SUBLANES
    last = n_rows // SUBLANES - 1
    return [
        pl.BlockSpec((tm, width), lambda i: (i, 0)),
        pl.BlockSpec((SUBLANES, width), lambda i: (jnp.minimum((i + 1) * per, last), 0)),
        pl.BlockSpec((SUBLANES, width), lambda i: (jnp.maximum(i * per - 1, 0), 0)),
    ]


def _params(sem):
    return pltpu.CompilerParams(dimension_semantics=sem, vmem_limit_bytes=VMEM_LIMIT)


def _layer(x, p, w):
    bsz, seq, d = x.shape
    n = bsz * seq
    tm = ROW_TILE
    assert seq % tm == 0 and seq % CHUNK == 0
    tiles_per_seq = seq // tm
    x2 = x.reshape(n, d)
    row = lambda width: pl.BlockSpec((tm, width), lambda i: (i, 0))

    rows_per_chunk = SUBLANES
    qv, kt, og, acum, cmax, brow = pl.pallas_call(
        functools.partial(_in_proj_kernel, tiles_per_seq=tiles_per_seq),
        grid=(n // tm,),
        in_specs=_row_specs(tm, d, n) + [_const_spec(a.shape) for a in w["in_proj"]],
        out_specs=[row(4 * GW), pl.BlockSpec((2 * GW, tm), lambda i: (0, i)), row(2 * GW), row(LANES), row(LANES),
                   pl.BlockSpec((tm // CHUNK * rows_per_chunk, LANES), lambda i: (i, 0))],
        out_shape=[jax.ShapeDtypeStruct((n, 4 * GW), BF16), jax.ShapeDtypeStruct((2 * GW, n), BF16),
                   jax.ShapeDtypeStruct((n, 2 * GW), BF16), jax.ShapeDtypeStruct((n, LANES), F32),
                   jax.ShapeDtypeStruct((n, LANES), F32),
                   jax.ShapeDtypeStruct((n // CHUNK * rows_per_chunk, LANES), F32)],
        compiler_params=_params(("parallel",)),
        name="in_proj",
    )(x2, x2, x2, *w["in_proj"])

    nc = seq // CHUNK
    fwd_i = lambda b, t: b * nc + t
    bwd_i = lambda b, t: b * nc + nc - 1 - t

    def chunk_specs(idx):
        return [pl.BlockSpec((CHUNK, 4 * GW), lambda b, t: (idx(b, t), 0)),
                pl.BlockSpec((2 * GW, CHUNK), lambda b, t: (0, idx(b, t))),
                pl.BlockSpec((CHUNK, LANES), lambda b, t: (idx(b, t), 0)),
                pl.BlockSpec((CHUNK, LANES), lambda b, t: (idx(b, t), 0)),
                pl.BlockSpec((rows_per_chunk, LANES), lambda b, t: (idx(b, t), 0))]

    out_f, out_b = pl.pallas_call(
        _mixer_kernel,
        grid=(bsz, nc),
        in_specs=[_const_spec(w["ret_logit"].shape)] + chunk_specs(fwd_i) + chunk_specs(bwd_i),
        out_specs=[pl.BlockSpec((CHUNK, 2 * GW), lambda b, t: (fwd_i(b, t), 0)),
                   pl.BlockSpec((CHUNK, 2 * GW), lambda b, t: (bwd_i(b, t), 0))],
        out_shape=[jax.ShapeDtypeStruct((n, 2 * GW), F32)] * 2,
        scratch_shapes=[
            pltpu.VMEM((HEADS, CHUNK, CHUNK), F32),
            pltpu.VMEM((CHUNK, GW), F32), pltpu.VMEM((CHUNK, GW), F32),
            pltpu.VMEM((2 * HEADS, CHUNK), F32),
            pltpu.VMEM((2, GW), F32),
            pltpu.VMEM((HEADS, HD, HD), F32), pltpu.VMEM((HEADS, HD, HD), F32),
            pltpu.VMEM((2 * HEADS, HD, 2 * HD), F32),
            pltpu.VMEM((2 * HEADS, LANES), F32),
        ],
        compiler_params=_params(("arbitrary", "arbitrary")),
        name="mixers",
    )(w["ret_logit"], *([qv, kt, acum, cmax, brow] * 2))

    h1 = pl.pallas_call(
        _mix_out_kernel,
        grid=(n // tm,),
        in_specs=[row(2 * GW), row(2 * GW), row(2 * GW), row(d)] + [_const_spec(a.shape) for a in w["mix_out"]],
        out_specs=row(d),
        out_shape=jax.ShapeDtypeStruct((n, d), F32),
        compiler_params=_params(("parallel",)),
        name="mix_out",
    )(out_f, out_b, og, x2, *w["mix_out"])

    p2 = p.reshape(n, p.shape[-1])
    y = pl.pallas_call(
        functools.partial(_ffn_kernel, tiles_per_seq=tiles_per_seq),
        grid=(n // tm,),
        in_specs=_row_specs(tm, d, n) + [row(p2.shape[-1])] + [_const_spec(a.shape) for a in w["ffn"]],
        out_specs=row(d),
        out_shape=jax.ShapeDtypeStruct((n, d), F32),
        compiler_params=_params(("parallel",)),
        name="ffn",
    )(h1, h1, h1, p2, *w["ffn"])
    return y.reshape(bsz, seq, d)


def kernel(x_prompt, x_sample, p_prompt, p_sample, norm_mix_w, w_in, mlstm_conv_w, mlstm_conv_b, mlstm_gate_b, ret_decay_logit, ret_gn_w, mlstm_gn_w, w_out, norm_ffn_w, ffn_w_gate, ffn_w_up, ffn_conv_w, ffn_conv_b, ffn_w_down, ple_w_proj, ple_norm_w, ple_w_gate, ple_gate_b, final_norm_w):
    depth = w_in.shape[0]
    d = x_prompt.shape[-1]
    d_ff = ffn_w_gate.shape[-1]
    nblk = d_ff // FF_BLOCK
    assert d_ff % FF_BLOCK == 0
    rowv = lambda a: a.reshape(1, -1).astype(F32)
    inv = ROPE_BASE ** (-jnp.arange(0, HD, 2, dtype=F32) / HD)
    inv = jnp.concatenate([inv, inv]).reshape(1, HD)

    assert depth == 1
    outs = [x_prompt, x_sample]
    for l in range(depth):
        wi = w_in[l].astype(BF16)
        n_gate = 2 * HEADS
        wgate = jnp.zeros((d, 2 * LANES), BF16)
        wgate = wgate.at[:, 0:n_gate].set(wi[:, 8 * GW:8 * GW + n_gate])
        wgate = wgate.at[:, LANES:LANES + n_gate].set(wi[:, 8 * GW + n_gate:8 * GW + 2 * n_gate])
        gb = mlstm_gate_b[l].astype(F32).reshape(-1)
        gbias = jnp.zeros((1, 2 * LANES), F32)
        gbias = gbias.at[0, 0:n_gate].set(gb[:n_gate]).at[0, LANES:LANES + n_gate].set(gb[n_gate:])
        w = {
            "in_proj": [rowv(norm_mix_w[l]), inv, wi[:, 0:4 * GW], wi[:, 4 * GW:6 * GW], wi[:, 6 * GW:8 * GW], wgate,
                        gbias, mlstm_conv_w[l].astype(F32), rowv(mlstm_conv_b[l])],
            "ret_logit": jnp.repeat(ret_decay_logit[l].astype(F32), HD, axis=1),
            "mix_out": [jnp.concatenate([rowv(ret_gn_w[l]), rowv(mlstm_gn_w[l])], axis=1), w_out[l].astype(BF16)],
            "ffn": [rowv(norm_ffn_w[l]),
                    ffn_w_gate[l].astype(BF16).reshape(d, nblk, FF_BLOCK).transpose(1, 0, 2),
                    ffn_w_up[l].astype(BF16).reshape(d, nblk, FF_BLOCK).transpose(1, 0, 2),
                    ffn_conv_w[l].astype(F32).reshape(3, nblk, FF_BLOCK).transpose(1, 0, 2),
                    ffn_conv_b[l].astype(F32).reshape(nblk, 1, FF_BLOCK),
                    ffn_w_down[l].astype(BF16).reshape(nblk, FF_BLOCK, d),
                    rowv(ple_norm_w[l]), ple_w_gate[l].astype(BF16), rowv(ple_gate_b[l]),
                    ple_w_proj[l].astype(BF16), rowv(final_norm_w)],
        }
        outs = [_layer(h, p[l], w) for h, p in zip(outs, (p_prompt, p_sample))]
    return tuple(outs)
```

```python
import functools

import jax
import jax.numpy as jnp
from jax import lax
from jax.experimental import pallas as pl
from jax.experimental.pallas import tpu as pltpu

F32 = jnp.float32
BF16 = jnp.bfloat16

HEADS = 4
HD = 128
GW = HEADS * HD
CHUNK = 128
ROPE_BASE = 10000.0
RMS_EPS = 1e-6
HN_EPS = 1e-5
SUBLANES = 8
LANES = 128
ROW_TILE = 512
FF_BLOCK = 256
VMEM_LIMIT = 56 * 1024 * 1024


def _rms(x, w):
    return x * lax.rsqrt(jnp.mean(x * x, axis=-1, keepdims=True) + RMS_EPS) * w


def _log_sigmoid(x):
    return jnp.minimum(x, 0.0) - jnp.log1p(jnp.exp(-jnp.abs(x)))


def _sigmoid(x):
    return 1.0 / (1.0 + jnp.exp(-x))


def _gelu(x):
    return 0.5 * x * (1.0 + lax.erf(x * (0.5 ** 0.5)))


def _dot(a, b):
    return jnp.dot(a, b, preferred_element_type=F32)


def _dot_nt(a, b):
    return lax.dot_general(a, b, (((1,), (1,)), ((), ())), preferred_element_type=F32)


def _ext_rows(main, nxt, prv, tin, tiles_per_seq):
    nxt = jnp.where(tin == tiles_per_seq - 1, 0.0, nxt)
    prv = jnp.where(tin == 0, 0.0, prv)
    return jnp.concatenate([main, nxt, prv], axis=0)


def _conv3(ext, rows, w, b):
    n = ext.shape[0]
    up = pltpu.roll(ext, 1, axis=0)[:rows]
    dn = pltpu.roll(ext, n - 1, axis=0)[:rows]
    return up * w[0:1] + ext[:rows] * w[1:2] + dn * w[2:3] + b


def _chunk_stats(ig, lf):
    tm = ig.shape[0]
    ii = lax.broadcasted_iota(jnp.int32, (CHUNK, CHUNK), 0)
    jj = lax.broadcasted_iota(jnp.int32, (CHUNK, CHUNK), 1)
    tril = (ii >= jj).astype(F32)
    triu = (jj >= ii).astype(F32)
    hi = lax.Precision.HIGHEST
    a = jnp.concatenate([
        jnp.where(jj < HEADS,
                  jnp.dot(tril, lf[r:r + CHUNK], precision=hi, preferred_element_type=F32),
                  jnp.dot(triu, lf[r:r + CHUNK], precision=hi, preferred_element_type=F32))
        for r in range(0, tm, CHUNK)], axis=0)
    b = ig - a
    r = lax.broadcasted_iota(jnp.int32, (tm, LANES), 0) % CHUNK
    pre = suf = b
    s = 1
    while s < CHUNK:
        pre = jnp.maximum(pre, jnp.where(r >= s, pltpu.roll(pre, s, axis=0), -jnp.inf))
        suf = jnp.maximum(suf, jnp.where(r < CHUNK - s, pltpu.roll(suf, tm - s, axis=0), -jnp.inf))
        s *= 2
    cm = jnp.where(lax.broadcasted_iota(jnp.int32, (tm, LANES), 1) < HEADS, pre, suf)
    return a, b, cm


def _rope_table_kernel(inv_ref, cos_ref, sin_ref):
    tm = cos_ref.shape[0]
    pos = (pl.program_id(0) * tm + lax.broadcasted_iota(jnp.int32, (tm, HD), 0)).astype(F32)
    ang = pos * inv_ref[...]
    cos_ref[...] = jnp.cos(ang)
    sin = jnp.sin(ang)
    lane = lax.broadcasted_iota(jnp.int32, (tm, HD), 1)
    sin_ref[...] = jnp.where(lane < HD // 2, -sin, sin)


def _in_proj_kernel(x_ref, xn_ref, xp_ref, cos_ref, sin_ref, nw_ref, wr_ref, wqk_ref, wm_ref, wg_ref, gb_ref, cw_ref,
                    cb_ref, qv_ref, kt_ref, og_ref, ac_ref, cm_ref, br_ref, *, tiles_per_seq):
    tm = x_ref.shape[0]
    tin = pl.program_id(0) % tiles_per_seq
    nw = nw_ref[...]
    a_ext = _ext_rows(_rms(x_ref[...], nw), _rms(xn_ref[...], nw), _rms(xp_ref[...], nw), tin,
                      tiles_per_seq).astype(BF16)
    a = a_ext[:tm]
    scale = HD ** -0.5

    PW = 2 * HD

    def rope(t, i):
        th = t[:, i * HD:(i + 1) * HD]
        return th * cos_ref[...] + pltpu.roll(th, HD // 2, axis=1) * sin_ref[...]

    def put_rq(t, lo):
        for i in range(PW // HD):
            qv_ref[:, lo + i * HD:lo + (i + 1) * HD] = (rope(t, i) * scale).astype(BF16)

    def put_rk(t, lo):
        for i in range(PW // HD):
            kt_ref[lo + i * HD:lo + (i + 1) * HD, :] = rope(t, i).T.astype(BF16)

    def conv_silu(t, lo):
        c = _conv3(t, tm, cw_ref[:, lo:lo + PW], cb_ref[:, lo:lo + PW])
        return c * _sigmoid(c)

    def put_mq(t, lo):
        qv_ref[:, 2 * GW + lo:2 * GW + lo + PW] = (conv_silu(t, lo) * scale).astype(BF16)

    def put_mk(t, lo):
        c = conv_silu(t, GW + lo)
        for i in range(PW // HD):
            kt_ref[GW + lo + i * HD:GW + lo + (i + 1) * HD, :] = c[:, i * HD:(i + 1) * HD].T.astype(BF16)

    def put_gates(t, lo):
        g = t + gb_ref[...]
        acum, b, cmax = _chunk_stats(g[:, :LANES], _log_sigmoid(g[:, LANES:]))
        ac_ref[...] = acum
        cm_ref[...] = cmax
        for ch in range(tm // CHUNK):
            br_ref[ch * SUBLANES:(ch + 1) * SUBLANES, :] = b[ch * CHUNK:(ch + 1) * CHUNK].T[:SUBLANES]

    def put(ref, base):
        def store(t, lo):
            ref[:, base + lo:base + lo + PW] = t.astype(BF16)
        return store

    def pieces(lhs, w_ref, w_lo, consume):
        return [(lhs, w_ref, w_lo + lo, consume, lo) for lo in range(0, GW, PW)]

    rq, rk = pieces(a, wr_ref, 0, put_rq), pieces(a, wr_ref, GW, put_rk)
    mq, mk = pieces(a_ext, wqk_ref, 0, put_mq), pieces(a_ext, wqk_ref, GW, put_mk)
    rv, rg = pieces(a, wr_ref, 2 * GW, put(qv_ref, GW)), pieces(a, wr_ref, 3 * GW, put(og_ref, 0))
    mv, mo = pieces(a, wm_ref, 0, put(qv_ref, 3 * GW)), pieces(a, wm_ref, GW, put(og_ref, GW))
    gates = [(a, wg_ref, 0, put_gates, 0)]
    order = [mq[0], rv[0], rq[0], rv[1], mq[1], rg[0], rq[1], rg[1], mk[0], mv[0], rk[0], mv[1], mk[1], mo[0], rk[1],
             mo[1], gates[0]]
    project = lambda piece: _dot(piece[0], piece[1][:, piece[2]:piece[2] + PW])
    pending = project(order[0])
    for k, piece in enumerate(order):
        ready = pending
        if k + 1 < len(order):
            pending = project(order[k + 1])
        piece[3](ready, piece[4])


def _mixer_kernel(lg_ref, qf_ref, ktf_ref, acf_ref, cmf_ref, brf_ref, qb_ref, ktb_ref, acb_ref, cmb_ref, brb_ref,
                  of_ref, ob_ref, dm_ref, qdf_ref, qdb_ref, kd_ref, cd_ref, sf_ref, sb_ref, c_ref, m_ref):
    L = CHUNK
    ii = lax.broadcasted_iota(jnp.int32, (L, L), 0)
    jj = lax.broadcasted_iota(jnp.int32, (L, L), 1)

    @pl.when(pl.program_id(1) == 0)
    def _init():
        lg = _log_sigmoid(lg_ref[...])
        lgf, lgb = lg[0:1], lg[1:2]
        row = lax.broadcasted_iota(jnp.int32, (L, GW), 0).astype(F32)
        qdf_ref[...] = jnp.exp((row + 1.0) * lgf)
        qdb_ref[...] = jnp.exp((L - row) * lgb)
        cd_ref[...] = jnp.exp(float(L) * lg)
        diff = (ii - jj).astype(F32)
        col = lax.broadcasted_iota(jnp.int32, (1, L), 1).astype(F32)
        for h in range(HEADS):
            sl = slice(h * HD, (h + 1) * HD)
            dm_ref[h] = jnp.where(diff >= 0, jnp.exp(diff * lgf[:, sl]), jnp.exp(-diff * lgb[:, sl]))
            kd_ref[h:h + 1, :] = jnp.exp((L - 1.0 - col) * lgf[:, sl])
            kd_ref[HEADS + h:HEADS + h + 1, :] = jnp.exp(col * lgb[:, sl])
        sf_ref[...] = jnp.zeros_like(sf_ref)
        sb_ref[...] = jnp.zeros_like(sb_ref)
        c_ref[...] = jnp.zeros_like(c_ref)
        m_ref[...] = jnp.zeros_like(m_ref)

    H = range(HEADS)
    sls = [slice(h * HD, (h + 1) * HD) for h in H]
    part = lambda ref, i, h: ref[:, i * GW + h * HD:i * GW + (h + 1) * HD]
    qf, vf = ([part(qf_ref, i, h) for h in H] for i in range(2))
    qb, vb = ([part(qb_ref, i, h) for h in H] for i in range(2))
    ktf = [ktf_ref[sls[h], :] for h in H]
    ktb = [ktb_ref[sls[h], :] for h in H]
    p = [(_dot(qf[h], ktf[h]) * dm_ref[h]).astype(BF16) for h in H]
    s_f = [sf_ref[h] for h in H]
    s_b = [sb_ref[h] for h in H]
    cross_f = [_dot(qf[h], s_f[h].astype(BF16)) for h in H]
    cross_b = [_dot(qb[h], s_b[h].astype(BF16)) for h in H]
    inner = [_dot(p[h], vf[h]) for h in H]
    for h in H:
        of_ref[:, sls[h]] = inner[h] + cross_f[h] * qdf_ref[:, sls[h]]
        ob_ref[:, sls[h]] = cross_b[h] * qdb_ref[:, sls[h]]
    upd_f = [_dot((ktf[h].astype(F32) * kd_ref[h:h + 1, :]).astype(BF16), vf[h]) for h in H]
    upd_b = [_dot((ktb[h].astype(F32) * kd_ref[HEADS + h:HEADS + h + 1, :]).astype(BF16), vb[h]) for h in H]
    for h in H:
        sf_ref[h] = cd_ref[0:1, sls[h]] * s_f[h] + upd_f[h]
        sb_ref[h] = cd_ref[1:2, sls[h]] * s_b[h] + upd_b[h]

    tril = ii >= jj
    triu = jj >= ii
    is_f = jj < HEADS
    a_all = jnp.where(is_f, acf_ref[...], acb_ref[...])
    cm_all = jnp.where(is_f, cmf_ref[...], cmb_ref[...])
    b_rows = jnp.where(lax.broadcasted_iota(jnp.int32, (SUBLANES, L), 0) < HEADS, brf_ref[...], brb_ref[...])
    one_col = (jj == 0).astype(BF16)

    C = range(2 * HEADS)
    srcs = [qb_ref if c >= HEADS else qf_ref for c in C]
    kts = [ktb_ref if c >= HEADS else ktf_ref for c in C]
    dsts = [ob_ref if c >= HEADS else of_ref for c in C]
    lasts = [0 if c >= HEADS else L - 1 for c in C]
    hs = [c % HEADS for c in C]
    q = [part(srcs[c], 2, hs[c]) for c in C]
    kt = [kts[c][GW + hs[c] * HD:GW + (hs[c] + 1) * HD, :] for c in C]
    v1 = [jnp.concatenate([part(srcs[c], 3, hs[c]), one_col], axis=1) for c in C]
    m_old = [m_ref[c:c + 1, 0:1] for c in C]
    mu = [jnp.maximum(cm_all[:, c:c + 1], m_old[c]) for c in C]
    e = [jnp.exp(jnp.where(triu if c >= HEADS else tril, b_rows[c:c + 1, :] - mu[c], -jnp.inf)) for c in C]
    qk = [_dot(q[c], kt[c]) for c in C]
    s = [(qk[c] * e[c]).astype(BF16) for c in C]
    cn = [c_ref[c] for c in C]
    qc = [_dot(q[c], cn[c].astype(BF16)) for c in C]
    sv = [_dot(s[c], v1[c]) for c in C]
    num = [jnp.exp(m_old[c] - mu[c]) * qc[c] + sv[c] for c in C]
    for c in C:
        den = jnp.maximum(jnp.abs(num[c][:, HD:HD + 1]), jnp.exp(-(a_all[:, c:c + 1] + mu[c])))
        dsts[c][:, GW + hs[c] * HD:GW + (hs[c] + 1) * HD] = num[c][:, :HD] / den
    mu_last = [mu[c][lasts[c]:lasts[c] + 1, :] for c in C]
    kw = [(kt[c].astype(F32) * jnp.exp(b_rows[c:c + 1, :] - mu_last[c])).astype(BF16) for c in C]
    upd = [_dot(kw[c], v1[c]) for c in C]
    for c in C:
        c_ref[c] = jnp.exp(m_old[c] - mu_last[c]) * cn[c] + upd[c]
        m_ref[c:c + 1, :] = jnp.broadcast_to(a_all[lasts[c]:lasts[c] + 1, c:c + 1] + mu_last[c], (1, LANES))


def _mix_out_kernel(of_ref, ob_ref, og_ref, x_ref, gn_ref, wo_ref, h_ref):
    y = of_ref[...] + ob_ref[...]
    parts = []
    for g in range(2 * HEADS):
        sl = slice(g * HD, (g + 1) * HD)
        yg = y[:, sl]
        d = yg - jnp.mean(yg, axis=-1, keepdims=True)
        n = d * lax.rsqrt(jnp.mean(d * d, axis=-1, keepdims=True) + HN_EPS) * gn_ref[:, sl]
        z = og_ref[:, sl].astype(F32)
        gate = z * _sigmoid(z) if g < HEADS else _sigmoid(z)
        parts.append((n * gate).astype(BF16))
    mixed = jnp.concatenate(parts, axis=1)
    h_ref[...] = x_ref[...] + _dot(mixed, wo_ref[...])


def _ffn_kernel(h_ref, hn_ref, hp_ref, p_ref, nfw_ref, wg_ref, wu_ref, cw_ref, cb_ref, wd_ref, pnw_ref, pwg_ref,
                pgb_ref, pwp_ref, fnw_ref, y_ref, act_ref, *, tiles_per_seq):
    tm = h_ref.shape[0]
    tin = pl.program_id(0) % tiles_per_seq
    h = h_ref[...]
    nfw = nfw_ref[...]
    f_ext = _ext_rows(_rms(h, nfw), _rms(hn_ref[...], nfw), _rms(hp_ref[...], nfw), tin,
                      tiles_per_seq).astype(BF16)
    f = f_ext[:tm]
    nblk = wg_ref.shape[0]
    fb = wg_ref.shape[2]
    proj = lambda j: (_dot(f_ext, wg_ref[j]), _dot(f, wu_ref[j]))
    nxt = proj(0)
    for j in range(nblk):
        u_ext, up = nxt
        if j + 1 < nblk:
            nxt = proj(j + 1)
        act_ref[:, j * fb:(j + 1) * fb] = (_gelu(_conv3(u_ext, tm, cw_ref[j], cb_ref[j])) * up).astype(BF16)
    h = h + _dot(act_ref[...], wd_ref[...])
    gate = _sigmoid(_dot(_rms(h, pnw_ref[...]).astype(BF16), pwg_ref[...]) + pgb_ref[...])
    h = h + _dot(p_ref[...].astype(BF16), pwp_ref[...]) * gate
    y_ref[...] = _rms(h, fnw_ref[...])


def _const_spec(shape):
    nd = len(shape)
    return pl.BlockSpec(shape, lambda *_: (0,) * nd, pipeline_mode=pl.Buffered(1))


def _row_specs(tm, width, n_rows):
    per = tm // SUBLANES
    last = n_rows // SUBLANES - 1
    return [
        pl.BlockSpec((tm, width), lambda i: (i, 0)),
        pl.BlockSpec((SUBLANES, width), lambda i: (jnp.minimum((i + 1) * per, last), 0)),
        pl.BlockSpec((SUBLANES, width), lambda i: (jnp.maximum(i * per - 1, 0), 0)),
    ]


def _params(sem):
    return pltpu.CompilerParams(dimension_semantics=sem, vmem_limit_bytes=VMEM_LIMIT)


def _layer(x, p, w):
    bsz, seq, d = x.shape
    n = bsz * seq
    tm = ROW_TILE
    assert seq % tm == 0 and seq % CHUNK == 0
    tiles_per_seq = seq // tm
    x2 = x.reshape(n, d)
    row = lambda width: pl.BlockSpec((tm, width), lambda i: (i, 0))

    rows_per_chunk = SUBLANES
    pos_tile = pl.BlockSpec((tm, HD), lambda i: (i % tiles_per_seq, 0))
    qv, kt, og, acum, cmax, brow = pl.pallas_call(
        functools.partial(_in_proj_kernel, tiles_per_seq=tiles_per_seq),
        grid=(n // tm,),
        in_specs=_row_specs(tm, d, n) + [pos_tile, pos_tile] + [_const_spec(a.shape) for a in w["in_proj"]],
        out_specs=[row(4 * GW), pl.BlockSpec((2 * GW, tm), lambda i: (0, i)), row(2 * GW), row(LANES), row(LANES),
                   pl.BlockSpec((tm // CHUNK * rows_per_chunk, LANES), lambda i: (i, 0))],
        out_shape=[jax.ShapeDtypeStruct((n, 4 * GW), BF16), jax.ShapeDtypeStruct((2 * GW, n), BF16),
                   jax.ShapeDtypeStruct((n, 2 * GW), BF16), jax.ShapeDtypeStruct((n, LANES), F32),
                   jax.ShapeDtypeStruct((n, LANES), F32),
                   jax.ShapeDtypeStruct((n // CHUNK * rows_per_chunk, LANES), F32)],
        compiler_params=_params(("parallel",)),
        name="in_proj",
    )(x2, x2, x2, *w["rope"], *w["in_proj"])

    nc = seq // CHUNK
    fwd_i = lambda b, t: b * nc + t
    bwd_i = lambda b, t: b * nc + nc - 1 - t

    def chunk_specs(idx):
        return [pl.BlockSpec((CHUNK, 4 * GW), lambda b, t: (idx(b, t), 0)),
                pl.BlockSpec((2 * GW, CHUNK), lambda b, t: (0, idx(b, t))),
                pl.BlockSpec((CHUNK, LANES), lambda b, t: (idx(b, t), 0)),
                pl.BlockSpec((CHUNK, LANES), lambda b, t: (idx(b, t), 0)),
                pl.BlockSpec((rows_per_chunk, LANES), lambda b, t: (idx(b, t), 0))]

    out_f, out_b = pl.pallas_call(
        _mixer_kernel,
        grid=(bsz, nc),
        in_specs=[_const_spec(w["ret_logit"].shape)] + chunk_specs(fwd_i) + chunk_specs(bwd_i),
        out_specs=[pl.BlockSpec((CHUNK, 2 * GW), lambda b, t: (fwd_i(b, t), 0)),
                   pl.BlockSpec((CHUNK, 2 * GW), lambda b, t: (bwd_i(b, t), 0))],
        out_shape=[jax.ShapeDtypeStruct((n, 2 * GW), F32)] * 2,
        scratch_shapes=[
            pltpu.VMEM((HEADS, CHUNK, CHUNK), F32),
            pltpu.VMEM((CHUNK, GW), F32), pltpu.VMEM((CHUNK, GW), F32),
            pltpu.VMEM((2 * HEADS, CHUNK), F32),
            pltpu.VMEM((2, GW), F32),
            pltpu.VMEM((HEADS, HD, HD), F32), pltpu.VMEM((HEADS, HD, HD), F32),
            pltpu.VMEM((2 * HEADS, HD, 2 * HD), F32),
            pltpu.VMEM((2 * HEADS, LANES), F32),
        ],
        compiler_params=_params(("arbitrary", "arbitrary")),
        name="mixers",
    )(w["ret_logit"], *([qv, kt, acum, cmax, brow] * 2))

    h1 = pl.pallas_call(
        _mix_out_kernel,
        grid=(n // tm,),
        in_specs=[row(2 * GW), row(2 * GW), row(2 * GW), row(d)] + [_const_spec(a.shape) for a in w["mix_out"]],
        out_specs=row(d),
        out_shape=jax.ShapeDtypeStruct((n, d), F32),
        compiler_params=_params(("parallel",)),
        name="mix_out",
    )(out_f, out_b, og, x2, *w["mix_out"])

    p2 = p.reshape(n, p.shape[-1])
    y = pl.pallas_call(
        functools.partial(_ffn_kernel, tiles_per_seq=tiles_per_seq),
        grid=(n // tm,),
        in_specs=_row_specs(tm, d, n) + [row(p2.shape[-1])] + [_const_spec(a.shape) for a in w["ffn"]],
        out_specs=row(d),
        out_shape=jax.ShapeDtypeStruct((n, d), F32),
        scratch_shapes=[pltpu.VMEM((tm, w["ffn"][5].shape[0]), BF16)],
        compiler_params=_params(("parallel",)),
        name="ffn",
    )(h1, h1, h1, p2, *w["ffn"])
    return y.reshape(bsz, seq, d)


def kernel(x_prompt, x_sample, p_prompt, p_sample, norm_mix_w, w_in, mlstm_conv_w, mlstm_conv_b, mlstm_gate_b, ret_decay_logit, ret_gn_w, mlstm_gn_w, w_out, norm_ffn_w, ffn_w_gate, ffn_w_up, ffn_conv_w, ffn_conv_b, ffn_w_down, ple_w_proj, ple_norm_w, ple_w_gate, ple_gate_b, final_norm_w):
    depth = w_in.shape[0]
    d = x_prompt.shape[-1]
    d_ff = ffn_w_gate.shape[-1]
    nblk = d_ff // FF_BLOCK
    assert d_ff % FF_BLOCK == 0
    rowv = lambda a: a.reshape(1, -1).astype(F32)
    inv = ROPE_BASE ** (-jnp.arange(0, HD, 2, dtype=F32) / HD)
    inv = jnp.concatenate([inv, inv]).reshape(1, HD)
    max_seq = max(x_prompt.shape[1], x_sample.shape[1])
    assert max_seq % ROW_TILE == 0
    rope = pl.pallas_call(
        _rope_table_kernel,
        grid=(max_seq // ROW_TILE,),
        in_specs=[_const_spec(inv.shape)],
        out_specs=[pl.BlockSpec((ROW_TILE, HD), lambda i: (i, 0))] * 2,
        out_shape=[jax.ShapeDtypeStruct((max_seq, HD), F32)] * 2,
        compiler_params=_params(("parallel",)),
        name="rope_tables",
    )(inv)

    assert depth == 1
    outs = [x_prompt, x_sample]
    for l in range(depth):
        wi = w_in[l].astype(BF16)
        n_gate = 2 * HEADS
        wgate = jnp.zeros((d, 2 * LANES), BF16)
        wgate = wgate.at[:, 0:n_gate].set(wi[:, 8 * GW:8 * GW + n_gate])
        wgate = wgate.at[:, LANES:LANES + n_gate].set(wi[:, 8 * GW + n_gate:8 * GW + 2 * n_gate])
        gb = mlstm_gate_b[l].astype(F32).reshape(-1)
        gbias = jnp.zeros((1, 2 * LANES), F32)
        gbias = gbias.at[0, 0:n_gate].set(gb[:n_gate]).at[0, LANES:LANES + n_gate].set(gb[n_gate:])
        w = {
            "rope": rope,
            "in_proj": [rowv(norm_mix_w[l]), wi[:, 0:4 * GW], wi[:, 4 * GW:6 * GW], wi[:, 6 * GW:8 * GW], wgate,
                        gbias, mlstm_conv_w[l].astype(F32), rowv(mlstm_conv_b[l])],
            "ret_logit": jnp.repeat(ret_decay_logit[l].astype(F32), HD, axis=1),
            "mix_out": [jnp.concatenate([rowv(ret_gn_w[l]), rowv(mlstm_gn_w[l])], axis=1), w_out[l].astype(BF16)],
            "ffn": [rowv(norm_ffn_w[l]),
                    ffn_w_gate[l].astype(BF16).reshape(d, nblk, FF_BLOCK).transpose(1, 0, 2),
                    ffn_w_up[l].astype(BF16).reshape(d, nblk, FF_BLOCK).transpose(1, 0, 2),
                    ffn_conv_w[l].astype(F32).reshape(3, nblk, FF_BLOCK).transpose(1, 0, 2),
                    ffn_conv_b[l].astype(F32).reshape(nblk, 1, FF_BLOCK),
                    ffn_w_down[l].astype(BF16),
                    rowv(ple_norm_w[l]), ple_w_gate[l].astype(BF16), rowv(ple_gate_b[l]),
                    ple_w_proj[l].astype(BF16), rowv(final_norm_w)],
        }
        outs = [_layer(h, p[l], w) for h, p in zip(outs, (p_prompt, p_sample))]
    return tuple(outs)
```

```python
import functools

import jax
import jax.numpy as jnp
from jax import lax
from jax.experimental import pallas as pl
from jax.experimental.pallas import tpu as pltpu

F32 = jnp.float32
BF16 = jnp.bfloat16

HEADS = 4
HD = 128
GW = HEADS * HD
CHUNK = 128
ROPE_BASE = 10000.0
RMS_EPS = 1e-6
HN_EPS = 1e-5
SUBLANES = 8
LANES = 128
ROW_TILE = 512
FF_BLOCK = 256
MIX_CHUNKS = 4
VMEM_LIMIT = 56 * 1024 * 1024


def _rms(x, w):
    return x * lax.rsqrt(jnp.mean(x * x, axis=-1, keepdims=True) + RMS_EPS) * w


def _log_sigmoid(x):
    return jnp.minimum(x, 0.0) - jnp.log1p(jnp.exp(-jnp.abs(x)))


def _sigmoid(x):
    return 1.0 / (1.0 + jnp.exp(-x))


def _gelu(x):
    return 0.5 * x * (1.0 + lax.erf(x * (0.5 ** 0.5)))


def _dot(a, b):
    return jnp.dot(a, b, preferred_element_type=F32)


def _dot_nt(a, b):
    return lax.dot_general(a, b, (((1,), (1,)), ((), ())), preferred_element_type=F32)


def _ext_rows(main, nxt, prv, tin, tiles_per_seq):
    nxt = jnp.where(tin == tiles_per_seq - 1, 0.0, nxt)
    prv = jnp.where(tin == 0, 0.0, prv)
    return jnp.concatenate([main, nxt, prv], axis=0)


def _conv3(ext, rows, w, b):
    n = ext.shape[0]
    up = pltpu.roll(ext, 1, axis=0)[:rows]
    dn = pltpu.roll(ext, n - 1, axis=0)[:rows]
    return up * w[0:1] + ext[:rows] * w[1:2] + dn * w[2:3] + b


def _chunk_stats(ig, lf):
    tm = ig.shape[0]
    ii = lax.broadcasted_iota(jnp.int32, (CHUNK, CHUNK), 0)
    jj = lax.broadcasted_iota(jnp.int32, (CHUNK, CHUNK), 1)
    tril = (ii >= jj).astype(F32)
    triu = (jj >= ii).astype(F32)
    hi = lax.Precision.HIGHEST
    a = jnp.concatenate([
        jnp.where(jj < HEADS,
                  jnp.dot(tril, lf[r:r + CHUNK], precision=hi, preferred_element_type=F32),
                  jnp.dot(triu, lf[r:r + CHUNK], precision=hi, preferred_element_type=F32))
        for r in range(0, tm, CHUNK)], axis=0)
    b = ig - a
    r = lax.broadcasted_iota(jnp.int32, (tm, LANES), 0) % CHUNK
    pre = suf = b
    s = 1
    while s < CHUNK:
        pre = jnp.maximum(pre, jnp.where(r >= s, pltpu.roll(pre, s, axis=0), -jnp.inf))
        suf = jnp.maximum(suf, jnp.where(r < CHUNK - s, pltpu.roll(suf, tm - s, axis=0), -jnp.inf))
        s *= 2
    cm = jnp.where(lax.broadcasted_iota(jnp.int32, (tm, LANES), 1) < HEADS, pre, suf)
    return a, b, cm


def _rope_table_kernel(inv_ref, cos_ref, sin_ref):
    tm = cos_ref.shape[0]
    pos = (pl.program_id(0) * tm + lax.broadcasted_iota(jnp.int32, (tm, HD), 0)).astype(F32)
    ang = pos * inv_ref[...]
    cos_ref[...] = jnp.cos(ang)
    sin = jnp.sin(ang)
    lane = lax.broadcasted_iota(jnp.int32, (tm, HD), 1)
    sin_ref[...] = jnp.where(lane < HD // 2, -sin, sin)


def _in_proj_kernel(x_ref, xn_ref, xp_ref, cos_ref, sin_ref, nw_ref, wr_ref, wqk_ref, wm_ref, wg_ref, gb_ref, cw_ref,
                    cb_ref, qv_ref, kt_ref, og_ref, ac_ref, cm_ref, br_ref, *, tiles_per_seq):
    tm = x_ref.shape[0]
    tin = pl.program_id(0) % tiles_per_seq
    nw = nw_ref[...]
    a_ext = _ext_rows(_rms(x_ref[...], nw), _rms(xn_ref[...], nw), _rms(xp_ref[...], nw), tin,
                      tiles_per_seq).astype(BF16)
    a = a_ext[:tm]
    scale = HD ** -0.5

    PW = 2 * HD

    def rope(t, i):
        th = t[:, i * HD:(i + 1) * HD]
        return th * cos_ref[...] + pltpu.roll(th, HD // 2, axis=1) * sin_ref[...]

    def put_rq(t, lo):
        for i in range(PW // HD):
            qv_ref[:, lo + i * HD:lo + (i + 1) * HD] = (rope(t, i) * scale).astype(BF16)

    def put_keys(row_lo, k):
        kt = k.T.astype(BF16)
        for ch in range(tm // CHUNK):
            kt_ref[ch, row_lo:row_lo + HD, :] = kt[:, ch * CHUNK:(ch + 1) * CHUNK]

    def put_rk(t, lo):
        for i in range(PW // HD):
            put_keys(lo + i * HD, rope(t, i))

    def conv_silu(t, lo):
        c = _conv3(t, tm, cw_ref[:, lo:lo + PW], cb_ref[:, lo:lo + PW])
        return c * _sigmoid(c)

    def put_mq(t, lo):
        qv_ref[:, 2 * GW + lo:2 * GW + lo + PW] = (conv_silu(t, lo) * scale).astype(BF16)

    def put_mk(t, lo):
        c = conv_silu(t, GW + lo)
        for i in range(PW // HD):
            put_keys(GW + lo + i * HD, c[:, i * HD:(i + 1) * HD])

    def put_gates(t, lo):
        g = t + gb_ref[...]
        acum, b, cmax = _chunk_stats(g[:, :LANES], _log_sigmoid(g[:, LANES:]))
        ac_ref[...] = acum
        cm_ref[...] = cmax
        for ch in range(tm // CHUNK):
            br_ref[ch * SUBLANES:(ch + 1) * SUBLANES, :] = b[ch * CHUNK:(ch + 1) * CHUNK].T[:SUBLANES]

    def put(ref, base):
        def store(t, lo):
            ref[:, base + lo:base + lo + PW] = t.astype(BF16)
        return store

    def pieces(lhs, w_ref, w_lo, consume):
        return [(lhs, w_ref, w_lo + lo, consume, lo) for lo in range(0, GW, PW)]

    rq, rk = pieces(a, wr_ref, 0, put_rq), pieces(a, wr_ref, GW, put_rk)
    mq, mk = pieces(a_ext, wqk_ref, 0, put_mq), pieces(a_ext, wqk_ref, GW, put_mk)
    rv, rg = pieces(a, wr_ref, 2 * GW, put(qv_ref, GW)), pieces(a, wr_ref, 3 * GW, put(og_ref, 0))
    mv, mo = pieces(a, wm_ref, 0, put(qv_ref, 3 * GW)), pieces(a, wm_ref, GW, put(og_ref, GW))
    gates = [(a, wg_ref, 0, put_gates, 0)]
    order = [mq[0], rv[0], rq[0], rv[1], mq[1], rg[0], rq[1], rg[1], mk[0], mv[0], rk[0], mv[1], mk[1], mo[0], rk[1],
             mo[1], gates[0]]
    project = lambda piece: _dot(piece[0], piece[1][:, piece[2]:piece[2] + PW])
    pending = project(order[0])
    for k, piece in enumerate(order):
        ready = pending
        if k + 1 < len(order):
            pending = project(order[k + 1])
        piece[3](ready, piece[4])


def _mixer_kernel(lg_ref, qf_ref, ktf_ref, acf_ref, cmf_ref, brf_ref, qb_ref, ktb_ref, acb_ref, cmb_ref, brb_ref,
                  of_ref, ob_ref, dm_ref, qdf_ref, qdb_ref, kd_ref, cd_ref, sf_ref, sb_ref, c_ref, m_ref):
    L = CHUNK
    ii = lax.broadcasted_iota(jnp.int32, (L, L), 0)
    jj = lax.broadcasted_iota(jnp.int32, (L, L), 1)

    @pl.when(pl.program_id(1) == 0)
    def _init():
        lg = _log_sigmoid(lg_ref[...])
        lgf, lgb = lg[0:1], lg[1:2]
        row = lax.broadcasted_iota(jnp.int32, (L, GW), 0).astype(F32)
        qdf_ref[...] = jnp.exp((row + 1.0) * lgf)
        qdb_ref[...] = jnp.exp((L - row) * lgb)
        cd_ref[...] = jnp.exp(float(L) * lg)
        diff = (ii - jj).astype(F32)
        col = lax.broadcasted_iota(jnp.int32, (1, L), 1).astype(F32)
        for h in range(HEADS):
            sl = slice(h * HD, (h + 1) * HD)
            dm_ref[h] = jnp.where(diff >= 0, jnp.exp(diff * lgf[:, sl]), jnp.exp(-diff * lgb[:, sl]))
            kd_ref[h:h + 1, :] = jnp.exp((L - 1.0 - col) * lgf[:, sl])
            kd_ref[HEADS + h:HEADS + h + 1, :] = jnp.exp(col * lgb[:, sl])
        sf_ref[...] = jnp.zeros_like(sf_ref)
        sb_ref[...] = jnp.zeros_like(sb_ref)
        c_ref[...] = jnp.zeros_like(c_ref)
        m_ref[...] = jnp.zeros_like(m_ref)

    n_sub = qf_ref.shape[0] // L
    H = range(HEADS)
    C = range(2 * HEADS)
    sls = [slice(h * HD, (h + 1) * HD) for h in H]
    tril = ii >= jj
    triu = jj >= ii
    is_f = jj < HEADS
    one_col = (jj == 0).astype(BF16)

    def chunk(i, carry):
        i_b = n_sub - 1 - i
        rows_f = pl.ds(pl.multiple_of(i * L, L), L)
        rows_b = pl.ds(pl.multiple_of(i_b * L, L), L)
        part = lambda ref, rows, g, h: ref[rows, g * GW + h * HD:g * GW + (h + 1) * HD]

        qf, vf = ([part(qf_ref, rows_f, g, h) for h in H] for g in range(2))
        qb, vb = ([part(qb_ref, rows_b, g, h) for h in H] for g in range(2))
        ktf = [ktf_ref[i, sls[h], :] for h in H]
        ktb = [ktb_ref[i_b, sls[h], :] for h in H]
        p = [(_dot(qf[h], ktf[h]) * dm_ref[h]).astype(BF16) for h in H]
        s_f = [sf_ref[h] for h in H]
        s_b = [sb_ref[h] for h in H]
        cross_f = [_dot(qf[h], s_f[h].astype(BF16)) for h in H]
        cross_b = [_dot(qb[h], s_b[h].astype(BF16)) for h in H]
        inner = [_dot(p[h], vf[h]) for h in H]
        for h in H:
            of_ref[rows_f, sls[h]] = inner[h] + cross_f[h] * qdf_ref[:, sls[h]]
            ob_ref[rows_b, sls[h]] = cross_b[h] * qdb_ref[:, sls[h]]
        upd_f = [_dot((ktf[h].astype(F32) * kd_ref[h:h + 1, :]).astype(BF16), vf[h]) for h in H]
        upd_b = [_dot((ktb[h].astype(F32) * kd_ref[HEADS + h:HEADS + h + 1, :]).astype(BF16), vb[h]) for h in H]
        for h in H:
            sf_ref[h] = cd_ref[0:1, sls[h]] * s_f[h] + upd_f[h]
            sb_ref[h] = cd_ref[1:2, sls[h]] * s_b[h] + upd_b[h]

        a_all = jnp.where(is_f, acf_ref[rows_f, :], acb_ref[rows_b, :])
        cm_all = jnp.where(is_f, cmf_ref[rows_f, :], cmb_ref[rows_b, :])
        b_rows = jnp.where(lax.broadcasted_iota(jnp.int32, (SUBLANES, L), 0) < HEADS,
                           brf_ref[pl.ds(pl.multiple_of(i * SUBLANES, SUBLANES), SUBLANES), :],
                           brb_ref[pl.ds(pl.multiple_of(i_b * SUBLANES, SUBLANES), SUBLANES), :])

        back = [c >= HEADS for c in C]
        hs = [c % HEADS for c in C]
        lasts = [0 if back[c] else L - 1 for c in C]
        q = [part(qb_ref, rows_b, 2, hs[c]) if back[c] else part(qf_ref, rows_f, 2, hs[c]) for c in C]
        v = [part(qb_ref, rows_b, 3, hs[c]) if back[c] else part(qf_ref, rows_f, 3, hs[c]) for c in C]
        kt = [(ktb_ref[i_b, GW + hs[c] * HD:GW + (hs[c] + 1) * HD, :] if back[c] else
               ktf_ref[i, GW + hs[c] * HD:GW + (hs[c] + 1) * HD, :]) for c in C]
        v1 = [jnp.concatenate([v[c], one_col], axis=1) for c in C]
        m_old = [m_ref[c:c + 1, 0:1] for c in C]
        mu = [jnp.maximum(cm_all[:, c:c + 1], m_old[c]) for c in C]
        e = [jnp.exp(jnp.where(triu if back[c] else tril, b_rows[c:c + 1, :] - mu[c], -jnp.inf)) for c in C]
        qk = [_dot(q[c], kt[c]) for c in C]
        s = [(qk[c] * e[c]).astype(BF16) for c in C]
        cn = [c_ref[c] for c in C]
        qc = [_dot(q[c], cn[c].astype(BF16)) for c in C]
        sv = [_dot(s[c], v1[c]) for c in C]
        num = [jnp.exp(m_old[c] - mu[c]) * qc[c] + sv[c] for c in C]
        for c in C:
            den = jnp.maximum(jnp.abs(num[c][:, HD:HD + 1]), jnp.exp(-(a_all[:, c:c + 1] + mu[c])))
            dst, rows = (ob_ref, rows_b) if back[c] else (of_ref, rows_f)
            dst[rows, GW + hs[c] * HD:GW + (hs[c] + 1) * HD] = num[c][:, :HD] / den
        mu_last = [mu[c][lasts[c]:lasts[c] + 1, :] for c in C]
        kw = [(kt[c].astype(F32) * jnp.exp(b_rows[c:c + 1, :] - mu_last[c])).astype(BF16) for c in C]
        upd = [_dot(kw[c], v1[c]) for c in C]
        for c in C:
            c_ref[c] = jnp.exp(m_old[c] - mu_last[c]) * cn[c] + upd[c]
            m_ref[c:c + 1, :] = jnp.broadcast_to(a_all[lasts[c]:lasts[c] + 1, c:c + 1] + mu_last[c], (1, LANES))
        return carry

    lax.fori_loop(0, n_sub, chunk, 0)


def _out_ffn_kernel(of_ref, ofn_ref, ofp_ref, ob_ref, obn_ref, obp_ref, og_ref, ogn_ref, ogp_ref, x_ref, xn_ref,
                    xp_ref, p_ref, gn_ref, wo_ref, nfw_ref, wg_ref, wu_ref, cw_ref, cb_ref, wd_ref, pnw_ref, pwg_ref,
                    pgb_ref, pwp_ref, fnw_ref, y_ref, act_ref, *, tiles_per_seq):
    tm = x_ref.shape[0]
    n_ext = tm + 2 * SUBLANES
    tin = pl.program_id(0) % tiles_per_seq
    ext = lambda m, nx, pv: jnp.concatenate([m[...], nx[...], pv[...]], axis=0)
    y = ext(of_ref, ofn_ref, ofp_ref) + ext(ob_ref, obn_ref, obp_ref)
    z = jnp.concatenate([og_ref[...].astype(F32), ogn_ref[...].astype(F32)[:SUBLANES],
                         ogp_ref[...].astype(F32)[ogp_ref.shape[0] - SUBLANES:]], axis=0)
    parts = []
    for g in range(2 * HEADS):
        sl = slice(g * HD, (g + 1) * HD)
        yg = y[:, sl]
        d = yg - jnp.mean(yg, axis=-1, keepdims=True)
        n = d * lax.rsqrt(jnp.mean(d * d, axis=-1, keepdims=True) + HN_EPS) * gn_ref[:, sl]
        zg = z[:, sl]
        gate = zg * _sigmoid(zg) if g < HEADS else _sigmoid(zg)
        parts.append((n * gate).astype(BF16))
    h_ext = ext(x_ref, xn_ref, xp_ref) + _dot(jnp.concatenate(parts, axis=1), wo_ref[...])

    r = lax.broadcasted_iota(jnp.int32, (n_ext, 1), 0)
    pad = ((r >= tm) & (r < tm + SUBLANES) & (tin == tiles_per_seq - 1)) | ((r >= tm + SUBLANES) & (tin == 0))
    f_ext = jnp.where(pad, 0.0, _rms(h_ext, nfw_ref[...])).astype(BF16)
    f = f_ext[:tm]
    h = h_ext[:tm]
    fb = FF_BLOCK
    nblk = wg_ref.shape[1] // fb
    blk = lambda ref, j: ref[:, j * fb:(j + 1) * fb]
    proj = lambda j: (_dot(f_ext, blk(wg_ref, j)), _dot(f, blk(wu_ref, j)))
    nxt = proj(0)
    for j in range(nblk):
        u_ext, up = nxt
        if j + 1 < nblk:
            nxt = proj(j + 1)
        act_ref[:, j * fb:(j + 1) * fb] = (_gelu(_conv3(u_ext, tm, blk(cw_ref, j), blk(cb_ref, j))) * up).astype(BF16)
    h = h + _dot(act_ref[...], wd_ref[...])
    gate = _sigmoid(_dot(_rms(h, pnw_ref[...]).astype(BF16), pwg_ref[...]) + pgb_ref[...])
    h = h + _dot(p_ref[...].astype(BF16), pwp_ref[...]) * gate
    y_ref[...] = _rms(h, fnw_ref[...])


def _const_spec(shape):
    nd = len(shape)
    return pl.BlockSpec(shape, lambda *_: (0,) * nd, pipeline_mode=pl.Buffered(1))


def _row_specs(tm, width, n_rows, halo=SUBLANES):
    per = tm // halo
    last = n_rows // halo - 1
    return [
        pl.BlockSpec((tm, width), lambda i: (i, 0)),
        pl.BlockSpec((halo, width), lambda i: (jnp.minimum((i + 1) * per, last), 0)),
        pl.BlockSpec((halo, width), lambda i: (jnp.maximum(i * per - 1, 0), 0)),
    ]


def _params(sem):
    return pltpu.CompilerParams(dimension_semantics=sem, vmem_limit_bytes=VMEM_LIMIT)


def _layer(x, p, w):
    bsz, seq, d = x.shape
    n = bsz * seq
    tm = ROW_TILE
    assert seq % tm == 0 and seq % CHUNK == 0
    tiles_per_seq = seq // tm
    x2 = x.reshape(n, d)
    row = lambda width: pl.BlockSpec((tm, width), lambda i: (i, 0))

    rows_per_chunk = SUBLANES
    pos_tile = pl.BlockSpec((tm, HD), lambda i: (i % tiles_per_seq, 0))
    qv, kt, og, acum, cmax, brow = pl.pallas_call(
        functools.partial(_in_proj_kernel, tiles_per_seq=tiles_per_seq),
        grid=(n // tm,),
        in_specs=_row_specs(tm, d, n) + [pos_tile, pos_tile] + [_const_spec(a.shape) for a in w["in_proj"]],
        out_specs=[row(4 * GW), pl.BlockSpec((tm // CHUNK, 2 * GW, CHUNK), lambda i: (i, 0, 0)), row(2 * GW),
                   row(LANES), row(LANES), pl.BlockSpec((tm // CHUNK * rows_per_chunk, LANES), lambda i: (i, 0))],
        out_shape=[jax.ShapeDtypeStruct((n, 4 * GW), BF16), jax.ShapeDtypeStruct((n // CHUNK, 2 * GW, CHUNK), BF16),
                   jax.ShapeDtypeStruct((n, 2 * GW), BF16), jax.ShapeDtypeStruct((n, LANES), F32),
                   jax.ShapeDtypeStruct((n, LANES), F32),
                   jax.ShapeDtypeStruct((n // CHUNK * rows_per_chunk, LANES), F32)],
        compiler_params=_params(("parallel",)),
        name="in_proj",
    )(x2, x2, x2, *w["rope"], *w["in_proj"])

    cb = MIX_CHUNKS
    assert seq % (cb * CHUNK) == 0
    nb = seq // (cb * CHUNK)
    fwd_i = lambda b, t: b * nb + t
    bwd_i = lambda b, t: b * nb + nb - 1 - t

    def block_specs(idx):
        return [pl.BlockSpec((cb * CHUNK, 4 * GW), lambda b, t: (idx(b, t), 0)),
                pl.BlockSpec((cb, 2 * GW, CHUNK), lambda b, t: (idx(b, t), 0, 0)),
                pl.BlockSpec((cb * CHUNK, LANES), lambda b, t: (idx(b, t), 0)),
                pl.BlockSpec((cb * CHUNK, LANES), lambda b, t: (idx(b, t), 0)),
                pl.BlockSpec((cb * rows_per_chunk, LANES), lambda b, t: (idx(b, t), 0))]

    out_f, out_b = pl.pallas_call(
        _mixer_kernel,
        grid=(bsz, nb),
        in_specs=[_const_spec(w["ret_logit"].shape)] + block_specs(fwd_i) + block_specs(bwd_i),
        out_specs=[pl.BlockSpec((cb * CHUNK, 2 * GW), lambda b, t: (fwd_i(b, t), 0)),
                   pl.BlockSpec((cb * CHUNK, 2 * GW), lambda b, t: (bwd_i(b, t), 0))],
        out_shape=[jax.ShapeDtypeStruct((n, 2 * GW), F32)] * 2,
        scratch_shapes=[
            pltpu.VMEM((HEADS, CHUNK, CHUNK), F32),
            pltpu.VMEM((CHUNK, GW), F32), pltpu.VMEM((CHUNK, GW), F32),
            pltpu.VMEM((2 * HEADS, CHUNK), F32),
            pltpu.VMEM((2, GW), F32),
            pltpu.VMEM((HEADS, HD, HD), F32), pltpu.VMEM((HEADS, HD, HD), F32),
            pltpu.VMEM((2 * HEADS, HD, 2 * HD), F32),
            pltpu.VMEM((2 * HEADS, LANES), F32),
        ],
        compiler_params=_params(("arbitrary", "arbitrary")),
        name="mixers",
    )(w["ret_logit"], *([qv, kt, acum, cmax, brow] * 2))

    p2 = p.reshape(n, p.shape[-1])
    bf16_rows = 2 * SUBLANES
    y = pl.pallas_call(
        functools.partial(_out_ffn_kernel, tiles_per_seq=tiles_per_seq),
        grid=(n // tm,),
        in_specs=(_row_specs(tm, d, n) + _row_specs(tm, d, n) + _row_specs(tm, d, n, bf16_rows) + _row_specs(tm, d, n)
                  + [row(p2.shape[-1])] + [_const_spec(a.shape) for a in w["out_ffn"]]),
        out_specs=row(d),
        out_shape=jax.ShapeDtypeStruct((n, d), F32),
        scratch_shapes=[pltpu.VMEM((tm, w["d_ff"]), BF16)],
        compiler_params=_params(("parallel",)),
        name="out_ffn",
    )(out_f, out_f, out_f, out_b, out_b, out_b, og, og, og, x2, x2, x2, p2, *w["out_ffn"])
    return y.reshape(bsz, seq, d)


def kernel(x_prompt, x_sample, p_prompt, p_sample, norm_mix_w, w_in, mlstm_conv_w, mlstm_conv_b, mlstm_gate_b, ret_decay_logit, ret_gn_w, mlstm_gn_w, w_out, norm_ffn_w, ffn_w_gate, ffn_w_up, ffn_conv_w, ffn_conv_b, ffn_w_down, ple_w_proj, ple_norm_w, ple_w_gate, ple_gate_b, final_norm_w):
    depth = w_in.shape[0]
    d = x_prompt.shape[-1]
    d_ff = ffn_w_gate.shape[-1]
    assert d_ff % FF_BLOCK == 0
    rowv = lambda a: a.reshape(1, -1).astype(F32)
    inv = ROPE_BASE ** (-jnp.arange(0, HD, 2, dtype=F32) / HD)
    inv = jnp.concatenate([inv, inv]).reshape(1, HD)
    max_seq = max(x_prompt.shape[1], x_sample.shape[1])
    assert max_seq % ROW_TILE == 0
    rope = pl.pallas_call(
        _rope_table_kernel,
        grid=(max_seq // ROW_TILE,),
        in_specs=[_const_spec(inv.shape)],
        out_specs=[pl.BlockSpec((ROW_TILE, HD), lambda i: (i, 0))] * 2,
        out_shape=[jax.ShapeDtypeStruct((max_seq, HD), F32)] * 2,
        compiler_params=_params(("parallel",)),
        name="rope_tables",
    )(inv)

    assert depth == 1
    outs = [x_prompt, x_sample]
    for l in range(depth):
        wi = w_in[l].astype(BF16)
        n_gate = 2 * HEADS
        wgate = jnp.zeros((d, 2 * LANES), BF16)
        wgate = wgate.at[:, 0:n_gate].set(wi[:, 8 * GW:8 * GW + n_gate])
        wgate = wgate.at[:, LANES:LANES + n_gate].set(wi[:, 8 * GW + n_gate:8 * GW + 2 * n_gate])
        gb = mlstm_gate_b[l].astype(F32).reshape(-1)
        gbias = jnp.zeros((1, 2 * LANES), F32)
        gbias = gbias.at[0, 0:n_gate].set(gb[:n_gate]).at[0, LANES:LANES + n_gate].set(gb[n_gate:])
        w = {
            "rope": rope,
            "in_proj": [rowv(norm_mix_w[l]), wi[:, 0:4 * GW], wi[:, 4 * GW:6 * GW], wi[:, 6 * GW:8 * GW], wgate,
                        gbias, mlstm_conv_w[l].astype(F32), rowv(mlstm_conv_b[l])],
            "ret_logit": jnp.repeat(ret_decay_logit[l].astype(F32), HD, axis=1),
            "d_ff": d_ff,
            "out_ffn": [jnp.concatenate([rowv(ret_gn_w[l]), rowv(mlstm_gn_w[l])], axis=1), w_out[l].astype(BF16),
                        rowv(norm_ffn_w[l]), ffn_w_gate[l].astype(BF16), ffn_w_up[l].astype(BF16),
                        ffn_conv_w[l].astype(F32), rowv(ffn_conv_b[l]), ffn_w_down[l].astype(BF16),
                        rowv(ple_norm_w[l]), ple_w_gate[l].astype(BF16), rowv(ple_gate_b[l]),
                        ple_w_proj[l].astype(BF16), rowv(final_norm_w)],
        }
        outs = [_layer(h, p[l], w) for h, p in zip(outs, (p_prompt, p_sample))]
    return tuple(outs)
```

```python
import functools

import jax
import jax.numpy as jnp
from jax import lax
from jax.experimental import pallas as pl
from jax.experimental.pallas import tpu as pltpu

F32 = jnp.float32
BF16 = jnp.bfloat16

HEADS = 4
HD = 128
GW = HEADS * HD
CHUNK = 128
ROPE_BASE = 10000.0
RMS_EPS = 1e-6
HN_EPS = 1e-5
SUBLANES = 8
LANES = 128
ROW_TILE = 512
FF_BLOCK = 256
MIX_CHUNKS = 4
MLSTM_GROUP = 4
VMEM_LIMIT = 56 * 1024 * 1024


def _rms(x, w):
    return x * lax.rsqrt(jnp.mean(x * x, axis=-1, keepdims=True) + RMS_EPS) * w


def _log_sigmoid(x):
    return jnp.minimum(x, 0.0) - jnp.log1p(jnp.exp(-jnp.abs(x)))


def _sigmoid(x):
    return 1.0 / (1.0 + jnp.exp(-x))


def _gelu(x):
    return 0.5 * x * (1.0 + lax.erf(x * (0.5 ** 0.5)))


def _dot(a, b):
    return jnp.dot(a, b, preferred_element_type=F32)


def _dot_nt(a, b):
    return lax.dot_general(a, b, (((1,), (1,)), ((), ())), preferred_element_type=F32)


def _ext_rows(main, nxt, prv, tin, tiles_per_seq):
    nxt = jnp.where(tin == tiles_per_seq - 1, 0.0, nxt)
    prv = jnp.where(tin == 0, 0.0, prv)
    return jnp.concatenate([main, nxt, prv], axis=0)


def _conv3(ext, rows, w, b):
    n = ext.shape[0]
    up = pltpu.roll(ext, 1, axis=0)[:rows]
    dn = pltpu.roll(ext, n - 1, axis=0)[:rows]
    return up * w[0:1] + ext[:rows] * w[1:2] + dn * w[2:3] + b


def _chunk_stats(ig, lf):
    tm = ig.shape[0]
    ii = lax.broadcasted_iota(jnp.int32, (CHUNK, CHUNK), 0)
    jj = lax.broadcasted_iota(jnp.int32, (CHUNK, CHUNK), 1)
    tril = (ii >= jj).astype(F32)
    triu = (jj >= ii).astype(F32)
    hi = lax.Precision.HIGHEST
    a = jnp.concatenate([
        jnp.where(jj < HEADS,
                  jnp.dot(tril, lf[r:r + CHUNK], precision=hi, preferred_element_type=F32),
                  jnp.dot(triu, lf[r:r + CHUNK], precision=hi, preferred_element_type=F32))
        for r in range(0, tm, CHUNK)], axis=0)
    b = ig - a
    r = lax.broadcasted_iota(jnp.int32, (tm, LANES), 0) % CHUNK
    pre = suf = b
    s = 1
    while s < CHUNK:
        pre = jnp.maximum(pre, jnp.where(r >= s, pltpu.roll(pre, s, axis=0), -jnp.inf))
        suf = jnp.maximum(suf, jnp.where(r < CHUNK - s, pltpu.roll(suf, tm - s, axis=0), -jnp.inf))
        s *= 2
    cm = jnp.where(lax.broadcasted_iota(jnp.int32, (tm, LANES), 1) < HEADS, pre, suf)
    return a, b, cm


def _rope_table_kernel(inv_ref, cos_ref, sin_ref):
    tm = cos_ref.shape[0]
    pos = (pl.program_id(0) * tm + lax.broadcasted_iota(jnp.int32, (tm, HD), 0)).astype(F32)
    ang = pos * inv_ref[...]
    cos_ref[...] = jnp.cos(ang)
    sin = jnp.sin(ang)
    lane = lax.broadcasted_iota(jnp.int32, (tm, HD), 1)
    sin_ref[...] = jnp.where(lane < HD // 2, -sin, sin)


def _in_proj_kernel(x_ref, xn_ref, xp_ref, cos_ref, sin_ref, nw_ref, wr_ref, wqk_ref, wm_ref, wg_ref, gb_ref, cw_ref,
                    cb_ref, qv_ref, kt_ref, og_ref, ac_ref, cm_ref, br_ref, *, tiles_per_seq):
    tm = x_ref.shape[0]
    tin = pl.program_id(0) % tiles_per_seq
    nw = nw_ref[...]
    a_ext = _ext_rows(_rms(x_ref[...], nw), _rms(xn_ref[...], nw), _rms(xp_ref[...], nw), tin,
                      tiles_per_seq).astype(BF16)
    a = a_ext[:tm]
    scale = HD ** -0.5

    PW = 2 * HD

    def rope(t, i):
        th = t[:, i * HD:(i + 1) * HD]
        return th * cos_ref[...] + pltpu.roll(th, HD // 2, axis=1) * sin_ref[...]

    def put_rq(t, lo):
        for i in range(PW // HD):
            qv_ref[:, lo + i * HD:lo + (i + 1) * HD] = (rope(t, i) * scale).astype(BF16)

    def put_keys(row_lo, k):
        kt = k.T.astype(BF16)
        for ch in range(tm // CHUNK):
            kt_ref[ch, row_lo:row_lo + HD, :] = kt[:, ch * CHUNK:(ch + 1) * CHUNK]

    def put_rk(t, lo):
        for i in range(PW // HD):
            put_keys(lo + i * HD, rope(t, i))

    def conv_silu(t, lo):
        c = _conv3(t, tm, cw_ref[:, lo:lo + PW], cb_ref[:, lo:lo + PW])
        return c * _sigmoid(c)

    def put_mq(t, lo):
        qv_ref[:, 2 * GW + lo:2 * GW + lo + PW] = (conv_silu(t, lo) * scale).astype(BF16)

    def put_mk(t, lo):
        c = conv_silu(t, GW + lo)
        for i in range(PW // HD):
            put_keys(GW + lo + i * HD, c[:, i * HD:(i + 1) * HD])

    def put_gates(t, lo):
        g = t + gb_ref[...]
        acum, b, cmax = _chunk_stats(g[:, :LANES], _log_sigmoid(g[:, LANES:]))
        ac_ref[...] = acum
        cm_ref[...] = cmax
        for ch in range(tm // CHUNK):
            br_ref[ch * SUBLANES:(ch + 1) * SUBLANES, :] = b[ch * CHUNK:(ch + 1) * CHUNK].T[:SUBLANES]

    def put(ref, base):
        def store(t, lo):
            ref[:, base + lo:base + lo + PW] = t.astype(BF16)
        return store

    def pieces(lhs, w_ref, w_lo, consume):
        return [(lhs, w_ref, w_lo + lo, consume, lo) for lo in range(0, GW, PW)]

    rq, rk = pieces(a, wr_ref, 0, put_rq), pieces(a, wr_ref, GW, put_rk)
    mq, mk = pieces(a_ext, wqk_ref, 0, put_mq), pieces(a_ext, wqk_ref, GW, put_mk)
    rv, rg = pieces(a, wr_ref, 2 * GW, put(qv_ref, GW)), pieces(a, wr_ref, 3 * GW, put(og_ref, 0))
    mv, mo = pieces(a, wm_ref, 0, put(qv_ref, 3 * GW)), pieces(a, wm_ref, GW, put(og_ref, GW))
    gates = [(a, wg_ref, 0, put_gates, 0)]
    order = [mq[0], rv[0], rq[0], rv[1], mq[1], rg[0], rq[1], rg[1], mk[0], mv[0], rk[0], mv[1], mk[1], mo[0], rk[1],
             mo[1], gates[0]]
    project = lambda piece: _dot(piece[0], piece[1][:, piece[2]:piece[2] + PW])
    pending = project(order[0])
    for k, piece in enumerate(order):
        ready = pending
        if k + 1 < len(order):
            pending = project(order[k + 1])
        piece[3](ready, piece[4])


def _mixer_kernel(lg_ref, qf_ref, ktf_ref, acf_ref, cmf_ref, brf_ref, qb_ref, ktb_ref, acb_ref, cmb_ref, brb_ref,
                  of_ref, ob_ref, dm_ref, qdf_ref, qdb_ref, kd_ref, cd_ref, sf_ref, sb_ref, c_ref, m_ref):
    L = CHUNK
    ii = lax.broadcasted_iota(jnp.int32, (L, L), 0)
    jj = lax.broadcasted_iota(jnp.int32, (L, L), 1)

    @pl.when(pl.program_id(1) == 0)
    def _init():
        lg = _log_sigmoid(lg_ref[...])
        lgf, lgb = lg[0:1], lg[1:2]
        row = lax.broadcasted_iota(jnp.int32, (L, GW), 0).astype(F32)
        qdf_ref[...] = jnp.exp((row + 1.0) * lgf)
        qdb_ref[...] = jnp.exp((L - row) * lgb)
        cd_ref[...] = jnp.exp(float(L) * lg)
        diff = (ii - jj).astype(F32)
        col = lax.broadcasted_iota(jnp.int32, (1, L), 1).astype(F32)
        for h in range(HEADS):
            sl = slice(h * HD, (h + 1) * HD)
            dm_ref[h] = jnp.where(diff >= 0, jnp.exp(diff * lgf[:, sl]), jnp.exp(-diff * lgb[:, sl]))
            kd_ref[h:h + 1, :] = jnp.exp((L - 1.0 - col) * lgf[:, sl])
            kd_ref[HEADS + h:HEADS + h + 1, :] = jnp.exp(col * lgb[:, sl])
        sf_ref[...] = jnp.zeros_like(sf_ref)
        sb_ref[...] = jnp.zeros_like(sb_ref)
        c_ref[...] = jnp.zeros_like(c_ref)
        m_ref[...] = jnp.zeros_like(m_ref)

    n_sub = qf_ref.shape[0] // L
    H = range(HEADS)
    C = range(2 * HEADS)
    sls = [slice(h * HD, (h + 1) * HD) for h in H]
    tril = ii >= jj
    triu = jj >= ii
    is_f = jj < HEADS
    ones_blk = jnp.ones((L, HD), BF16)

    def chunk(i, carry):
        i_b = n_sub - 1 - i
        rows_f = pl.ds(pl.multiple_of(i * L, L), L)
        rows_b = pl.ds(pl.multiple_of(i_b * L, L), L)
        part = lambda ref, rows, g, h: ref[rows, g * GW + h * HD:g * GW + (h + 1) * HD]

        def retention():
            qf, vf = ([part(qf_ref, rows_f, g, h) for h in H] for g in range(2))
            qb, vb = ([part(qb_ref, rows_b, g, h) for h in H] for g in range(2))
            ktf = [ktf_ref[i, sls[h], :] for h in H]
            ktb = [ktb_ref[i_b, sls[h], :] for h in H]
            p = [(_dot(qf[h], ktf[h]) * dm_ref[h]).astype(BF16) for h in H]
            s_f = [sf_ref[h] for h in H]
            s_b = [sb_ref[h] for h in H]
            cross_f = [_dot(qf[h], s_f[h].astype(BF16)) for h in H]
            cross_b = [_dot(qb[h], s_b[h].astype(BF16)) for h in H]
            inner = [_dot(p[h], vf[h]) for h in H]
            for h in H:
                of_ref[rows_f, sls[h]] = inner[h] + cross_f[h] * qdf_ref[:, sls[h]]
                ob_ref[rows_b, sls[h]] = cross_b[h] * qdb_ref[:, sls[h]]
            upd_f = [_dot((ktf[h].astype(F32) * kd_ref[h:h + 1, :]).astype(BF16), vf[h]) for h in H]
            upd_b = [_dot((ktb[h].astype(F32) * kd_ref[HEADS + h:HEADS + h + 1, :]).astype(BF16), vb[h]) for h in H]
            for h in H:
                sf_ref[h] = cd_ref[0:1, sls[h]] * s_f[h] + upd_f[h]
                sb_ref[h] = cd_ref[1:2, sls[h]] * s_b[h] + upd_b[h]

        a_all = jnp.where(is_f, acf_ref[rows_f, :], acb_ref[rows_b, :])
        cm_all = jnp.where(is_f, cmf_ref[rows_f, :], cmb_ref[rows_b, :])
        b_rows = jnp.where(lax.broadcasted_iota(jnp.int32, (SUBLANES, L), 0) < HEADS,
                           brf_ref[pl.ds(pl.multiple_of(i * SUBLANES, SUBLANES), SUBLANES), :],
                           brb_ref[pl.ds(pl.multiple_of(i_b * SUBLANES, SUBLANES), SUBLANES), :])

        def mlstm_group(cs):
            back = {c: c >= HEADS for c in cs}
            hd = {c: c % HEADS for c in cs}
            last = {c: 0 if back[c] else L - 1 for c in cs}
            q = {c: part(qb_ref, rows_b, 2, hd[c]) if back[c] else part(qf_ref, rows_f, 2, hd[c]) for c in cs}
            v = {c: part(qb_ref, rows_b, 3, hd[c]) if back[c] else part(qf_ref, rows_f, 3, hd[c]) for c in cs}
            kt = {c: (ktb_ref[i_b, GW + hd[c] * HD:GW + (hd[c] + 1) * HD, :] if back[c] else
                      ktf_ref[i, GW + hd[c] * HD:GW + (hd[c] + 1) * HD, :]) for c in cs}
            v1 = {c: jnp.concatenate([v[c], ones_blk], axis=1) for c in cs}
            m_old = {c: m_ref[c:c + 1, 0:1] for c in cs}
            mu = {c: jnp.maximum(jnp.broadcast_to(cm_all[:, c:c + 1], (L, L)), m_old[c]) for c in cs}
            floor = {c: jnp.exp(-(jnp.broadcast_to(a_all[:, c:c + 1], (L, L)) + mu[c])) for c in cs}
            e = {c: jnp.exp(jnp.where(triu if back[c] else tril, b_rows[c:c + 1, :] - mu[c], -jnp.inf)) for c in cs}
            qk = {c: _dot(q[c], kt[c]) for c in cs}
            s = {c: (qk[c] * e[c]).astype(BF16) for c in cs}
            cn = {c: c_ref[c] for c in cs}
            qc = {c: _dot(q[c], cn[c].astype(BF16)) for c in cs}
            sv = {c: _dot(s[c], v1[c]) for c in cs}
            w = {c: jnp.exp(m_old[c] - mu[c]) for c in cs}
            for c in cs:
                den = jnp.maximum(jnp.abs(w[c] * qc[c][:, HD:] + sv[c][:, HD:]), floor[c])
                dst, rows = (ob_ref, rows_b) if back[c] else (of_ref, rows_f)
                dst[rows, GW + hd[c] * HD:GW + (hd[c] + 1) * HD] = (w[c] * qc[c][:, :HD] + sv[c][:, :HD]) / den
            mu_last = {c: mu[c][last[c]:last[c] + 1, 0:1] for c in cs}
            kw = {c: (kt[c].astype(F32) * jnp.exp(b_rows[c:c + 1, :] - mu_last[c])).astype(BF16) for c in cs}
            upd = {c: _dot(kw[c], v1[c]) for c in cs}
            for c in cs:
                c_ref[c] = jnp.exp(m_old[c] - mu_last[c]) * cn[c] + upd[c]
                m_ref[c:c + 1, :] = jnp.broadcast_to(a_all[last[c]:last[c] + 1, c:c + 1] + mu_last[c], (1, LANES))

        retention()
        for g in range(0, 2 * HEADS, MLSTM_GROUP):
            mlstm_group(range(g, g + MLSTM_GROUP))
        return carry

    lax.fori_loop(0, n_sub, chunk, 0)


def _out_ffn_kernel(of_ref, ofn_ref, ofp_ref, ob_ref, obn_ref, obp_ref, og_ref, ogn_ref, ogp_ref, x_ref, xn_ref,
                    xp_ref, p_ref, gn_ref, wo_ref, nfw_ref, wg_ref, wu_ref, cw_ref, cb_ref, wd_ref, pnw_ref, pwg_ref,
                    pgb_ref, pwp_ref, fnw_ref, y_ref, act_ref, *, tiles_per_seq):
    tm = x_ref.shape[0]
    n_ext = tm + 2 * SUBLANES
    tin = pl.program_id(0) % tiles_per_seq
    ext = lambda m, nx, pv: jnp.concatenate([m[...], nx[...], pv[...]], axis=0)
    y = ext(of_ref, ofn_ref, ofp_ref) + ext(ob_ref, obn_ref, obp_ref)
    z = jnp.concatenate([og_ref[...].astype(F32), ogn_ref[...].astype(F32)[:SUBLANES],
                         ogp_ref[...].astype(F32)[ogp_ref.shape[0] - SUBLANES:]], axis=0)
    ple = _dot(p_ref[...].astype(BF16), pwp_ref[...])
    h_ext = ext(x_ref, xn_ref, xp_ref)
    parts = []
    for g in range(2 * HEADS):
        sl = slice(g * HD, (g + 1) * HD)
        yg = y[:, sl]
        d = yg - jnp.mean(yg, axis=-1, keepdims=True)
        n = d * lax.rsqrt(jnp.mean(d * d, axis=-1, keepdims=True) + HN_EPS) * gn_ref[:, sl]
        zg = z[:, sl]
        gate = zg * _sigmoid(zg) if g < HEADS else _sigmoid(zg)
        parts.append((n * gate).astype(BF16))
        if g % 2 == 1:
            h_ext = h_ext + _dot(jnp.concatenate(parts[-2:], axis=1), wo_ref[(g - 1) * HD:(g + 1) * HD, :])

    r = lax.broadcasted_iota(jnp.int32, (n_ext, 1), 0)
    pad = ((r >= tm) & (r < tm + SUBLANES) & (tin == tiles_per_seq - 1)) | ((r >= tm + SUBLANES) & (tin == 0))
    f_ext = jnp.where(pad, 0.0, _rms(h_ext, nfw_ref[...])).astype(BF16)
    f = f_ext[:tm]
    h = h_ext[:tm]
    fb = FF_BLOCK
    nblk = wg_ref.shape[1] // fb
    blk = lambda ref, j: ref[:, j * fb:(j + 1) * fb]
    proj = lambda j: (_dot(f_ext, blk(wg_ref, j)), _dot(f, blk(wu_ref, j)))
    nxt = proj(0)
    for j in range(nblk):
        u_ext, up = nxt
        if j + 1 < nblk:
            nxt = proj(j + 1)
        act_ref[:, j * fb:(j + 1) * fb] = (_gelu(_conv3(u_ext, tm, blk(cw_ref, j), blk(cb_ref, j))) * up).astype(BF16)
    h = h + _dot(act_ref[...], wd_ref[...])
    gate = _sigmoid(_dot(_rms(h, pnw_ref[...]).astype(BF16), pwg_ref[...]) + pgb_ref[...])
    h = h + ple * gate
    y_ref[...] = _rms(h, fnw_ref[...])


def _const_spec(shape):
    nd = len(shape)
    return pl.BlockSpec(shape, lambda *_: (0,) * nd, pipeline_mode=pl.Buffered(1))


def _row_specs(tm, width, n_rows, halo=SUBLANES):
    per = tm // halo
    last = n_rows // halo - 1
    return [
        pl.BlockSpec((tm, width), lambda i: (i, 0)),
        pl.BlockSpec((halo, width), lambda i: (jnp.minimum((i + 1) * per, last), 0)),
        pl.BlockSpec((halo, width), lambda i: (jnp.maximum(i * per - 1, 0), 0)),
    ]


def _params(sem):
    return pltpu.CompilerParams(dimension_semantics=sem, vmem_limit_bytes=VMEM_LIMIT)


def _layer(x, p, w):
    bsz, seq, d = x.shape
    n = bsz * seq
    tm = ROW_TILE
    assert seq % tm == 0 and seq % CHUNK == 0
    tiles_per_seq = seq // tm
    x2 = x.reshape(n, d)
    row = lambda width: pl.BlockSpec((tm, width), lambda i: (i, 0))

    rows_per_chunk = SUBLANES
    pos_tile = pl.BlockSpec((tm, HD), lambda i: (i % tiles_per_seq, 0))
    qv, kt, og, acum, cmax, brow = pl.pallas_call(
        functools.partial(_in_proj_kernel, tiles_per_seq=tiles_per_seq),
        grid=(n // tm,),
        in_specs=_row_specs(tm, d, n) + [pos_tile, pos_tile] + [_const_spec(a.shape) for a in w["in_proj"]],
        out_specs=[row(4 * GW), pl.BlockSpec((tm // CHUNK, 2 * GW, CHUNK), lambda i: (i, 0, 0)), row(2 * GW),
                   row(LANES), row(LANES), pl.BlockSpec((tm // CHUNK * rows_per_chunk, LANES), lambda i: (i, 0))],
        out_shape=[jax.ShapeDtypeStruct((n, 4 * GW), BF16), jax.ShapeDtypeStruct((n // CHUNK, 2 * GW, CHUNK), BF16),
                   jax.ShapeDtypeStruct((n, 2 * GW), BF16), jax.ShapeDtypeStruct((n, LANES), F32),
                   jax.ShapeDtypeStruct((n, LANES), F32),
                   jax.ShapeDtypeStruct((n // CHUNK * rows_per_chunk, LANES), F32)],
        compiler_params=_params(("parallel",)),
        name="in_proj",
    )(x2, x2, x2, *w["rope"], *w["in_proj"])

    cb = MIX_CHUNKS
    assert seq % (cb * CHUNK) == 0
    nb = seq // (cb * CHUNK)
    fwd_i = lambda b, t: b * nb + t
    bwd_i = lambda b, t: b * nb + nb - 1 - t

    def block_specs(idx):
        return [pl.BlockSpec((cb * CHUNK, 4 * GW), lambda b, t: (idx(b, t), 0)),
                pl.BlockSpec((cb, 2 * GW, CHUNK), lambda b, t: (idx(b, t), 0, 0)),
                pl.BlockSpec((cb * CHUNK, LANES), lambda b, t: (idx(b, t), 0)),
                pl.BlockSpec((cb * CHUNK, LANES), lambda b, t: (idx(b, t), 0)),
                pl.BlockSpec((cb * rows_per_chunk, LANES), lambda b, t: (idx(b, t), 0))]

    out_f, out_b = pl.pallas_call(
        _mixer_kernel,
        grid=(bsz, nb),
        in_specs=[_const_spec(w["ret_logit"].shape)] + block_specs(fwd_i) + block_specs(bwd_i),
        out_specs=[pl.BlockSpec((cb * CHUNK, 2 * GW), lambda b, t: (fwd_i(b, t), 0)),
                   pl.BlockSpec((cb * CHUNK, 2 * GW), lambda b, t: (bwd_i(b, t), 0))],
        out_shape=[jax.ShapeDtypeStruct((n, 2 * GW), F32)] * 2,
        scratch_shapes=[
            pltpu.VMEM((HEADS, CHUNK, CHUNK), F32),
            pltpu.VMEM((CHUNK, GW), F32), pltpu.VMEM((CHUNK, GW), F32),
            pltpu.VMEM((2 * HEADS, CHUNK), F32),
            pltpu.VMEM((2, GW), F32),
            pltpu.VMEM((HEADS, HD, HD), F32), pltpu.VMEM((HEADS, HD, HD), F32),
            pltpu.VMEM((2 * HEADS, HD, 2 * HD), F32),
            pltpu.VMEM((2 * HEADS, LANES), F32),
        ],
        compiler_params=_params(("arbitrary", "arbitrary")),
        name="mixers",
    )(w["ret_logit"], *([qv, kt, acum, cmax, brow] * 2))

    p2 = p.reshape(n, p.shape[-1])
    bf16_rows = 2 * SUBLANES
    y = pl.pallas_call(
        functools.partial(_out_ffn_kernel, tiles_per_seq=tiles_per_seq),
        grid=(n // tm,),
        in_specs=(_row_specs(tm, d, n) + _row_specs(tm, d, n) + _row_specs(tm, d, n, bf16_rows) + _row_specs(tm, d, n)
                  + [row(p2.shape[-1])] + [_const_spec(a.shape) for a in w["out_ffn"]]),
        out_specs=row(d),
        out_shape=jax.ShapeDtypeStruct((n, d), F32),
        scratch_shapes=[pltpu.VMEM((tm, w["d_ff"]), BF16)],
        compiler_params=_params(("parallel",)),
        name="out_ffn",
    )(out_f, out_f, out_f, out_b, out_b, out_b, og, og, og, x2, x2, x2, p2, *w["out_ffn"])
    return y.reshape(bsz, seq, d)


def kernel(x_prompt, x_sample, p_prompt, p_sample, norm_mix_w, w_in, mlstm_conv_w, mlstm_conv_b, mlstm_gate_b, ret_decay_logit, ret_gn_w, mlstm_gn_w, w_out, norm_ffn_w, ffn_w_gate, ffn_w_up, ffn_conv_w, ffn_conv_b, ffn_w_down, ple_w_proj, ple_norm_w, ple_w_gate, ple_gate_b, final_norm_w):
    depth = w_in.shape[0]
    d = x_prompt.shape[-1]
    d_ff = ffn_w_gate.shape[-1]
    assert d_ff % FF_BLOCK == 0
    rowv = lambda a: a.reshape(1, -1).astype(F32)
    inv = ROPE_BASE ** (-jnp.arange(0, HD, 2, dtype=F32) / HD)
    inv = jnp.concatenate([inv, inv]).reshape(1, HD)
    max_seq = max(x_prompt.shape[1], x_sample.shape[1])
    assert max_seq % ROW_TILE == 0
    rope = pl.pallas_call(
        _rope_table_kernel,
        grid=(max_seq // ROW_TILE,),
        in_specs=[_const_spec(inv.shape)],
        out_specs=[pl.BlockSpec((ROW_TILE, HD), lambda i: (i, 0))] * 2,
        out_shape=[jax.ShapeDtypeStruct((max_seq, HD), F32)] * 2,
        compiler_params=_params(("parallel",)),
        name="rope_tables",
    )(inv)

    assert depth == 1
    outs = [x_prompt, x_sample]
    for l in range(depth):
        wi = w_in[l].astype(BF16)
        n_gate = 2 * HEADS
        wgate = jnp.zeros((d, 2 * LANES), BF16)
        wgate = wgate.at[:, 0:n_gate].set(wi[:, 8 * GW:8 * GW + n_gate])
        wgate = wgate.at[:, LANES:LANES + n_gate].set(wi[:, 8 * GW + n_gate:8 * GW + 2 * n_gate])
        gb = mlstm_gate_b[l].astype(F32).reshape(-1)
        gbias = jnp.zeros((1, 2 * LANES), F32)
        gbias = gbias.at[0, 0:n_gate].set(gb[:n_gate]).at[0, LANES:LANES + n_gate].set(gb[n_gate:])
        w = {
            "rope": rope,
            "in_proj": [rowv(norm_mix_w[l]), wi[:, 0:4 * GW], wi[:, 4 * GW:6 * GW], wi[:, 6 * GW:8 * GW], wgate,
                        gbias, mlstm_conv_w[l].astype(F32), rowv(mlstm_conv_b[l])],
            "ret_logit": jnp.repeat(ret_decay_logit[l].astype(F32), HD, axis=1),
            "d_ff": d_ff,
            "out_ffn": [jnp.concatenate([rowv(ret_gn_w[l]), rowv(mlstm_gn_w[l])], axis=1), w_out[l].astype(BF16),
                        rowv(norm_ffn_w[l]), ffn_w_gate[l].astype(BF16), ffn_w_up[l].astype(BF16),
                        ffn_conv_w[l].astype(F32), rowv(ffn_conv_b[l]), ffn_w_down[l].astype(BF16),
                        rowv(ple_norm_w[l]), ple_w_gate[l].astype(BF16), rowv(ple_gate_b[l]),
                        ple_w_proj[l].astype(BF16), rowv(final_norm_w)],
        }
        outs = [_layer(h, p[l], w) for h, p in zip(outs, (p_prompt, p_sample))]
    return tuple(outs)
```

```python
import functools

import jax
import jax.numpy as jnp
from jax import lax
from jax.experimental import pallas as pl
from jax.experimental.pallas import tpu as pltpu

F32 = jnp.float32
BF16 = jnp.bfloat16

HEADS = 4
HD = 128
GW = HEADS * HD
CHUNK = 128
ROPE_BASE = 10000.0
RMS_EPS = 1e-6
HN_EPS = 1e-5
SUBLANES = 8
LANES = 128
ROW_TILE = 512
IN_TILE = 1024
FF_BLOCK = 256
MIX_CHUNKS = 8
MLSTM_GROUP = 4
VMEM_LIMIT = 56 * 1024 * 1024


def _rms(x, w):
    return x * lax.rsqrt(jnp.mean(x * x, axis=-1, keepdims=True) + RMS_EPS) * w


def _log_sigmoid(x):
    return jnp.minimum(x, 0.0) - jnp.log1p(jnp.exp(-jnp.abs(x)))


def _sigmoid(x):
    return 1.0 / (1.0 + jnp.exp(-x))


def _gelu(x):
    return 0.5 * x * (1.0 + lax.erf(x * (0.5 ** 0.5)))


def _dot(a, b):
    return jnp.dot(a, b, preferred_element_type=F32)


def _dot_nt(a, b):
    return lax.dot_general(a, b, (((1,), (1,)), ((), ())), preferred_element_type=F32)


def _ext_rows(main, nxt, prv, tin, tiles_per_seq):
    nxt = jnp.where(tin == tiles_per_seq - 1, 0.0, nxt)
    prv = jnp.where(tin == 0, 0.0, prv)
    return jnp.concatenate([main, nxt, prv], axis=0)


def _conv3(ext, rows, w, b):
    n = ext.shape[0]
    up = pltpu.roll(ext, 1, axis=0)[:rows]
    dn = pltpu.roll(ext, n - 1, axis=0)[:rows]
    return up * w[0:1] + ext[:rows] * w[1:2] + dn * w[2:3] + b


def _chunk_stats(ig, lf):
    tm = ig.shape[0]
    ii = lax.broadcasted_iota(jnp.int32, (CHUNK, CHUNK), 0)
    jj = lax.broadcasted_iota(jnp.int32, (CHUNK, CHUNK), 1)
    tril = (ii >= jj).astype(F32)
    triu = (jj >= ii).astype(F32)
    hi = lax.Precision.HIGHEST
    a = jnp.concatenate([
        jnp.where(jj < HEADS,
                  jnp.dot(tril, lf[r:r + CHUNK], precision=hi, preferred_element_type=F32),
                  jnp.dot(triu, lf[r:r + CHUNK], precision=hi, preferred_element_type=F32))
        for r in range(0, tm, CHUNK)], axis=0)
    b = ig - a
    r = lax.broadcasted_iota(jnp.int32, (tm, LANES), 0) % CHUNK
    pre = suf = b
    s = 1
    while s < CHUNK:
        pre = jnp.maximum(pre, jnp.where(r >= s, pltpu.roll(pre, s, axis=0), -jnp.inf))
        suf = jnp.maximum(suf, jnp.where(r < CHUNK - s, pltpu.roll(suf, tm - s, axis=0), -jnp.inf))
        s *= 2
    cm = jnp.where(lax.broadcasted_iota(jnp.int32, (tm, LANES), 1) < HEADS, pre, suf)
    return a, b, cm


def _rope_table_kernel(inv_ref, cos_ref, sin_ref):
    tm = cos_ref.shape[0]
    pos = (pl.program_id(0) * tm + lax.broadcasted_iota(jnp.int32, (tm, HD), 0)).astype(F32)
    ang = pos * inv_ref[...]
    cos_ref[...] = jnp.cos(ang)
    sin = jnp.sin(ang)
    lane = lax.broadcasted_iota(jnp.int32, (tm, HD), 1)
    sin_ref[...] = jnp.where(lane < HD // 2, -sin, sin)


def _in_proj_kernel(x_ref, xn_ref, xp_ref, cos_ref, sin_ref, nw_ref, wr_ref, wqk_ref, wm_ref, wg_ref, gb_ref, cw_ref,
                    cb_ref, qv_ref, kt_ref, og_ref, ac_ref, cm_ref, br_ref, *, tiles_per_seq):
    tm = x_ref.shape[0]
    tin = pl.program_id(0) % tiles_per_seq
    nw = nw_ref[...]
    a_ext = _ext_rows(_rms(x_ref[...], nw), _rms(xn_ref[...], nw), _rms(xp_ref[...], nw), tin,
                      tiles_per_seq).astype(BF16)
    a = a_ext[:tm]
    scale = HD ** -0.5

    PW = 2 * HD

    def rope(t, i):
        th = t[:, i * HD:(i + 1) * HD]
        return th * cos_ref[...] + pltpu.roll(th, HD // 2, axis=1) * sin_ref[...]

    def put_rq(t, lo):
        for i in range(PW // HD):
            qv_ref[:, lo + i * HD:lo + (i + 1) * HD] = (rope(t, i) * scale).astype(BF16)

    def put_keys(row_lo, k):
        kt = k.T.astype(BF16)
        for ch in range(tm // CHUNK):
            kt_ref[ch, row_lo:row_lo + HD, :] = kt[:, ch * CHUNK:(ch + 1) * CHUNK]

    def put_rk(t, lo):
        for i in range(PW // HD):
            put_keys(lo + i * HD, rope(t, i))

    def conv_silu(t, lo):
        c = _conv3(t, tm, cw_ref[:, lo:lo + PW], cb_ref[:, lo:lo + PW])
        return c * _sigmoid(c)

    def put_mq(t, lo):
        qv_ref[:, 2 * GW + lo:2 * GW + lo + PW] = (conv_silu(t, lo) * scale).astype(BF16)

    def put_mk(t, lo):
        c = conv_silu(t, GW + lo)
        for i in range(PW // HD):
            put_keys(GW + lo + i * HD, c[:, i * HD:(i + 1) * HD])

    def put_gates(t, lo):
        g = t + gb_ref[...]
        acum, b, cmax = _chunk_stats(g[:, :LANES], _log_sigmoid(g[:, LANES:]))
        ac_ref[...] = acum
        cm_ref[...] = cmax
        for ch in range(tm // CHUNK):
            br_ref[ch * SUBLANES:(ch + 1) * SUBLANES, :] = b[ch * CHUNK:(ch + 1) * CHUNK].T[:SUBLANES]

    def put(ref, base):
        def store(t, lo):
            ref[:, base + lo:base + lo + PW] = t.astype(BF16)
        return store

    def pieces(lhs, w_ref, w_lo, consume):
        return [(lhs, w_ref, w_lo + lo, consume, lo) for lo in range(0, GW, PW)]

    rq, rk = pieces(a, wr_ref, 0, put_rq), pieces(a, wr_ref, GW, put_rk)
    mq, mk = pieces(a_ext, wqk_ref, 0, put_mq), pieces(a_ext, wqk_ref, GW, put_mk)
    rv, rg = pieces(a, wr_ref, 2 * GW, put(qv_ref, GW)), pieces(a, wr_ref, 3 * GW, put(og_ref, 0))
    mv, mo = pieces(a, wm_ref, 0, put(qv_ref, 3 * GW)), pieces(a, wm_ref, GW, put(og_ref, GW))
    gates = [(a, wg_ref, 0, put_gates, 0)]
    order = [(gates[0],), (mq[0], rv[0]), (rq[0], rv[1]), (mq[1], rg[0]), (rq[1], rg[1]), (mk[0], mv[0]),
             (rk[0], mv[1]), (mk[1], mo[0]), (rk[1], mo[1])]
    project = lambda stage: [_dot(piece[0], piece[1][:, piece[2]:piece[2] + PW]) for piece in stage]
    pending = project(order[0])
    for k, stage in enumerate(order):
        ready = pending
        if k + 1 < len(order):
            pending = project(order[k + 1])
        for piece, t in zip(stage, ready):
            piece[3](t, piece[4])


def _mixer_kernel(lg_ref, qf_ref, ktf_ref, acf_ref, cmf_ref, brf_ref, qb_ref, ktb_ref, acb_ref, cmb_ref, brb_ref,
                  of_ref, ob_ref, dm_ref, qdf_ref, qdb_ref, kd_ref, cd_ref, sf_ref, sb_ref, c_ref, m_ref):
    L = CHUNK
    ii = lax.broadcasted_iota(jnp.int32, (L, L), 0)
    jj = lax.broadcasted_iota(jnp.int32, (L, L), 1)

    @pl.when(pl.program_id(1) == 0)
    def _init():
        lg = _log_sigmoid(lg_ref[...])
        lgf, lgb = lg[0:1], lg[1:2]
        row = lax.broadcasted_iota(jnp.int32, (L, GW), 0).astype(F32)
        qdf_ref[...] = jnp.exp((row + 1.0) * lgf)
        qdb_ref[...] = jnp.exp((L - row) * lgb)
        cd_ref[...] = jnp.exp(float(L) * lg)
        diff = (ii - jj).astype(F32)
        col = lax.broadcasted_iota(jnp.int32, (1, L), 1).astype(F32)
        for h in range(HEADS):
            sl = slice(h * HD, (h + 1) * HD)
            dm_ref[h] = jnp.where(diff >= 0, jnp.exp(diff * lgf[:, sl]), jnp.exp(-diff * lgb[:, sl]))
            kd_ref[h:h + 1, :] = jnp.exp((L - 1.0 - col) * lgf[:, sl])
            kd_ref[HEADS + h:HEADS + h + 1, :] = jnp.exp(col * lgb[:, sl])
        sf_ref[...] = jnp.zeros_like(sf_ref)
        sb_ref[...] = jnp.zeros_like(sb_ref)
        c_ref[...] = jnp.zeros_like(c_ref)
        m_ref[...] = jnp.zeros_like(m_ref)

    n_sub = qf_ref.shape[0] // L
    H = range(HEADS)
    C = range(2 * HEADS)
    sls = [slice(h * HD, (h + 1) * HD) for h in H]
    tril = ii >= jj
    triu = jj >= ii
    is_f = jj < HEADS
    ones_blk = jnp.ones((L, HD), BF16)

    def chunk(i, carry):
        i_b = n_sub - 1 - i
        rows_f = pl.ds(pl.multiple_of(i * L, L), L)
        rows_b = pl.ds(pl.multiple_of(i_b * L, L), L)
        part = lambda ref, rows, g, h: ref[rows, g * GW + h * HD:g * GW + (h + 1) * HD]

        def retention():
            qf, vf = ([part(qf_ref, rows_f, g, h) for h in H] for g in range(2))
            qb, vb = ([part(qb_ref, rows_b, g, h) for h in H] for g in range(2))
            ktf = [ktf_ref[i, sls[h], :] for h in H]
            ktb = [ktb_ref[i_b, sls[h], :] for h in H]
            p = [(_dot(qf[h], ktf[h]) * dm_ref[h]).astype(BF16) for h in H]
            s_f = [sf_ref[h] for h in H]
            s_b = [sb_ref[h] for h in H]
            cross_f = [_dot(qf[h], s_f[h].astype(BF16)) for h in H]
            cross_b = [_dot(qb[h], s_b[h].astype(BF16)) for h in H]
            inner = [_dot(p[h], vf[h]) for h in H]
            for h in H:
                of_ref[rows_f, sls[h]] = (inner[h] + cross_f[h] * qdf_ref[:, sls[h]]).astype(of_ref.dtype)
                ob_ref[rows_b, sls[h]] = (cross_b[h] * qdb_ref[:, sls[h]]).astype(ob_ref.dtype)
            upd_f = [_dot((ktf[h].astype(F32) * kd_ref[h:h + 1, :]).astype(BF16), vf[h]) for h in H]
            upd_b = [_dot((ktb[h].astype(F32) * kd_ref[HEADS + h:HEADS + h + 1, :]).astype(BF16), vb[h]) for h in H]
            for h in H:
                sf_ref[h] = cd_ref[0:1, sls[h]] * s_f[h] + upd_f[h]
                sb_ref[h] = cd_ref[1:2, sls[h]] * s_b[h] + upd_b[h]

        a_all = jnp.where(is_f, acf_ref[rows_f, :], acb_ref[rows_b, :])
        cm_all = jnp.where(is_f, cmf_ref[rows_f, :], cmb_ref[rows_b, :])
        b_rows = jnp.where(lax.broadcasted_iota(jnp.int32, (SUBLANES, L), 0) < HEADS,
                           brf_ref[pl.ds(pl.multiple_of(i * SUBLANES, SUBLANES), SUBLANES), :],
                           brb_ref[pl.ds(pl.multiple_of(i_b * SUBLANES, SUBLANES), SUBLANES), :])

        def mlstm_group(cs):
            back = {c: c >= HEADS for c in cs}
            hd = {c: c % HEADS for c in cs}
            last = {c: 0 if back[c] else L - 1 for c in cs}
            q = {c: part(qb_ref, rows_b, 2, hd[c]) if back[c] else part(qf_ref, rows_f, 2, hd[c]) for c in cs}
            v = {c: part(qb_ref, rows_b, 3, hd[c]) if back[c] else part(qf_ref, rows_f, 3, hd[c]) for c in cs}
            kt = {c: (ktb_ref[i_b, GW + hd[c] * HD:GW + (hd[c] + 1) * HD, :] if back[c] else
                      ktf_ref[i, GW + hd[c] * HD:GW + (hd[c] + 1) * HD, :]) for c in cs}
            v1 = {c: jnp.concatenate([v[c], ones_blk], axis=1) for c in cs}
            m_old = {c: m_ref[c:c + 1, 0:1] for c in cs}
            mu = {c: jnp.maximum(jnp.broadcast_to(cm_all[:, c:c + 1], (L, L)), m_old[c]) for c in cs}
            floor = {c: jnp.exp(-(jnp.broadcast_to(a_all[:, c:c + 1], (L, L)) + mu[c])) for c in cs}
            e = {c: jnp.exp(jnp.where(triu if back[c] else tril, b_rows[c:c + 1, :] - mu[c], -jnp.inf)) for c in cs}
            qk = {c: _dot(q[c], kt[c]) for c in cs}
            s = {c: (qk[c] * e[c]).astype(BF16) for c in cs}
            cn = {c: c_ref[c] for c in cs}
            qc = {c: _dot(q[c], cn[c].astype(BF16)) for c in cs}
            sv = {c: _dot(s[c], v1[c]) for c in cs}
            w = {c: jnp.exp(m_old[c] - mu[c]) for c in cs}
            for c in cs:
                den = jnp.maximum(jnp.abs(w[c] * qc[c][:, HD:] + sv[c][:, HD:]), floor[c])
                dst, rows = (ob_ref, rows_b) if back[c] else (of_ref, rows_f)
                h_c = (w[c] * qc[c][:, :HD] + sv[c][:, :HD]) / den
                dst[rows, GW + hd[c] * HD:GW + (hd[c] + 1) * HD] = h_c.astype(dst.dtype)
            mu_last = {c: mu[c][last[c]:last[c] + 1, 0:1] for c in cs}
            kw = {c: (kt[c].astype(F32) * jnp.exp(b_rows[c:c + 1, :] - mu_last[c])).astype(BF16) for c in cs}
            upd = {c: _dot(kw[c], v1[c]) for c in cs}
            for c in cs:
                c_ref[c] = jnp.exp(m_old[c] - mu_last[c]) * cn[c] + upd[c]
                m_ref[c:c + 1, :] = jnp.broadcast_to(a_all[last[c]:last[c] + 1, c:c + 1] + mu_last[c], (1, LANES))

        retention()
        for g in range(0, 2 * HEADS, MLSTM_GROUP):
            mlstm_group(range(g, g + MLSTM_GROUP))
        return carry

    lax.fori_loop(0, n_sub, chunk, 0)


def _mix_out_pieces(rows, tin, tiles_per_seq, gn_ref, wo_ref, nfw_ref, emit):
    of_ref, ofn_ref, ofp_ref, ob_ref, obn_ref, obp_ref, og_ref, ogn_ref, ogp_ref, x_ref, xn_ref, xp_ref = rows
    tm = x_ref.shape[0]
    n_ext = tm + 2 * SUBLANES
    v = {}

    def ext(m, nx, pv):
        return jnp.concatenate([m[...].astype(F32), nx[...].astype(F32)[:SUBLANES],
                                pv[...].astype(F32)[pv.shape[0] - SUBLANES:]], axis=0)

    def load():
        v["y"] = ext(of_ref, ofn_ref, ofp_ref) + ext(ob_ref, obn_ref, obp_ref)
        v["z"] = ext(og_ref, ogn_ref, ogp_ref)
        v["h"] = ext(x_ref, xn_ref, xp_ref)
        v["parts"] = []

    def head(g):
        sl = slice(g * HD, (g + 1) * HD)
        yg = v["y"][:, sl]
        d = yg - jnp.mean(yg, axis=-1, keepdims=True)
        n = d * lax.rsqrt(jnp.mean(d * d, axis=-1, keepdims=True) + HN_EPS) * gn_ref[:, sl]
        zg = v["z"][:, sl]
        gate = zg * _sigmoid(zg) if g < HEADS else _sigmoid(zg)
        v["parts"].append((n * gate).astype(BF16))
        if g % 2 == 1:
            pair = jnp.concatenate(v["parts"][-2:], axis=1)
            v["h"] = v["h"] + _dot(pair, wo_ref[(g - 1) * HD:(g + 1) * HD, :])

    def finish():
        r = lax.broadcasted_iota(jnp.int32, (n_ext, 1), 0)
        pad = ((r >= tm) & (r < tm + SUBLANES) & (tin == tiles_per_seq - 1)) | ((r >= tm + SUBLANES) & (tin == 0))
        emit(jnp.where(pad, 0.0, _rms(v["h"], nfw_ref[...])).astype(BF16), v["h"][:tm])

    return [load] + [functools.partial(head, g) for g in range(2 * HEADS)] + [finish]


def _out_ffn_kernel(*refs, tiles_per_seq):
    rows = refs[:12]
    (p_ref, gn_ref, wo_ref, nfw_ref, wg_ref, wu_ref, cw_ref, cb_ref, wd_ref, pnw_ref, pwg_ref, pgb_ref, pwp_ref, fnw_ref,
     y_ref, act_ref) = refs[12:]
    tm = p_ref.shape[0]
    stage = {}
    tin = pl.program_id(0) % tiles_per_seq
    for piece in _mix_out_pieces(rows, tin, tiles_per_seq, gn_ref, wo_ref, nfw_ref,
                                 lambda f_ext, h: stage.update(f_ext=f_ext, h=h)):
        piece()
    f_ext = stage["f_ext"]
    f = f_ext[:tm]

    fb = FF_BLOCK
    nblk = wg_ref.shape[1] // fb
    blk = lambda ref, j: ref[:, j * fb:(j + 1) * fb]
    proj = lambda j: (_dot(f_ext, blk(wg_ref, j)), _dot(f, blk(wu_ref, j)))
    nxt = proj(0)
    for j in range(nblk):
        u_ext, up = nxt
        if j + 1 < nblk:
            nxt = proj(j + 1)
        act_ref[:, j * fb:(j + 1) * fb] = (_gelu(_conv3(u_ext, tm, blk(cw_ref, j), blk(cb_ref, j))) * up).astype(BF16)
    h = stage["h"] + _dot(act_ref[...], wd_ref[...])
    gate = _sigmoid(_dot(_rms(h, pnw_ref[...]).astype(BF16), pwg_ref[...]) + pgb_ref[...])
    h = h + _dot(p_ref[...].astype(BF16), pwp_ref[...]) * gate
    y_ref[...] = _rms(h, fnw_ref[...])


def _const_spec(shape):
    nd = len(shape)
    return pl.BlockSpec(shape, lambda *_: (0,) * nd, pipeline_mode=pl.Buffered(1))


def _row_specs(tm, width, n_rows, halo=SUBLANES):
    per = tm // halo
    last = n_rows // halo - 1
    return [
        pl.BlockSpec((tm, width), lambda i: (i, 0)),
        pl.BlockSpec((halo, width), lambda i: (jnp.minimum((i + 1) * per, last), 0)),
        pl.BlockSpec((halo, width), lambda i: (jnp.maximum(i * per - 1, 0), 0)),
    ]


def _params(sem):
    return pltpu.CompilerParams(dimension_semantics=sem, vmem_limit_bytes=VMEM_LIMIT)


def _layer(x, p, w):
    bsz, seq, d = x.shape
    n = bsz * seq
    tm = ROW_TILE
    assert seq % tm == 0 and seq % CHUNK == 0
    tiles_per_seq = seq // tm
    x2 = x.reshape(n, d)
    row = lambda width: pl.BlockSpec((tm, width), lambda i: (i, 0))

    rows_per_chunk = SUBLANES
    ti = IN_TILE
    assert seq % ti == 0
    row_i = lambda width: pl.BlockSpec((ti, width), lambda i: (i, 0))
    pos_tile = pl.BlockSpec((ti, HD), lambda i: (i % (seq // ti), 0))
    qv, kt, og, acum, cmax, brow = pl.pallas_call(
        functools.partial(_in_proj_kernel, tiles_per_seq=seq // ti),
        grid=(n // ti,),
        in_specs=_row_specs(ti, d, n) + [pos_tile, pos_tile] + [_const_spec(a.shape) for a in w["in_proj"]],
        out_specs=[row_i(4 * GW), pl.BlockSpec((ti // CHUNK, 2 * GW, CHUNK), lambda i: (i, 0, 0)), row_i(2 * GW),
                   row_i(LANES), row_i(LANES), pl.BlockSpec((ti // CHUNK * rows_per_chunk, LANES), lambda i: (i, 0))],
        out_shape=[jax.ShapeDtypeStruct((n, 4 * GW), BF16), jax.ShapeDtypeStruct((n // CHUNK, 2 * GW, CHUNK), BF16),
                   jax.ShapeDtypeStruct((n, 2 * GW), BF16), jax.ShapeDtypeStruct((n, LANES), F32),
                   jax.ShapeDtypeStruct((n, LANES), F32),
                   jax.ShapeDtypeStruct((n // CHUNK * rows_per_chunk, LANES), F32)],
        compiler_params=_params(("parallel",)),
        name="in_proj",
    )(x2, x2, x2, *w["rope"], *w["in_proj"])

    cb = MIX_CHUNKS
    assert seq % (cb * CHUNK) == 0
    nb = seq // (cb * CHUNK)
    fwd_i = lambda b, t: b * nb + t
    bwd_i = lambda b, t: b * nb + nb - 1 - t

    def block_specs(idx):
        return [pl.BlockSpec((cb * CHUNK, 4 * GW), lambda b, t: (idx(b, t), 0)),
                pl.BlockSpec((cb, 2 * GW, CHUNK), lambda b, t: (idx(b, t), 0, 0)),
                pl.BlockSpec((cb * CHUNK, LANES), lambda b, t: (idx(b, t), 0)),
                pl.BlockSpec((cb * CHUNK, LANES), lambda b, t: (idx(b, t), 0)),
                pl.BlockSpec((cb * rows_per_chunk, LANES), lambda b, t: (idx(b, t), 0))]

    out_f, out_b = pl.pallas_call(
        _mixer_kernel,
        grid=(bsz, nb),
        in_specs=[_const_spec(w["ret_logit"].shape)] + block_specs(fwd_i) + block_specs(bwd_i),
        out_specs=[pl.BlockSpec((cb * CHUNK, 2 * GW), lambda b, t: (fwd_i(b, t), 0)),
                   pl.BlockSpec((cb * CHUNK, 2 * GW), lambda b, t: (bwd_i(b, t), 0))],
        out_shape=[jax.ShapeDtypeStruct((n, 2 * GW), F32)] * 2,
        scratch_shapes=[
            pltpu.VMEM((HEADS, CHUNK, CHUNK), F32),
            pltpu.VMEM((CHUNK, GW), F32), pltpu.VMEM((CHUNK, GW), F32),
            pltpu.VMEM((2 * HEADS, CHUNK), F32),
            pltpu.VMEM((2, GW), F32),
            pltpu.VMEM((HEADS, HD, HD), F32), pltpu.VMEM((HEADS, HD, HD), F32),
            pltpu.VMEM((2 * HEADS, HD, 2 * HD), F32),
            pltpu.VMEM((2 * HEADS, LANES), F32),
        ],
        compiler_params=_params(("arbitrary", "arbitrary")),
        name="mixers",
    )(w["ret_logit"], *([qv, kt, acum, cmax, brow] * 2))

    p2 = p.reshape(n, p.shape[-1])
    bf16_rows = 2 * SUBLANES
    mix_rows = _row_specs(tm, d, n) * 2 + _row_specs(tm, d, n, bf16_rows) + _row_specs(tm, d, n)
    y = pl.pallas_call(
        functools.partial(_out_ffn_kernel, tiles_per_seq=tiles_per_seq),
        grid=(n // tm,),
        in_specs=mix_rows + [row(p2.shape[-1])] + [_const_spec(a.shape) for a in (*w["mix_out"], *w["ffn"])],
        out_specs=row(d),
        out_shape=jax.ShapeDtypeStruct((n, d), F32),
        scratch_shapes=[pltpu.VMEM((tm, w["d_ff"]), BF16)],
        compiler_params=_params(("parallel",)),
        name="out_ffn",
    )(*((out_f,) * 3 + (out_b,) * 3 + (og,) * 3 + (x2,) * 3), p2, *w["mix_out"], *w["ffn"])
    return y.reshape(bsz, seq, d)


def kernel(x_prompt, x_sample, p_prompt, p_sample, norm_mix_w, w_in, mlstm_conv_w, mlstm_conv_b, mlstm_gate_b, ret_decay_logit, ret_gn_w, mlstm_gn_w, w_out, norm_ffn_w, ffn_w_gate, ffn_w_up, ffn_conv_w, ffn_conv_b, ffn_w_down, ple_w_proj, ple_norm_w, ple_w_gate, ple_gate_b, final_norm_w):
    depth = w_in.shape[0]
    d = x_prompt.shape[-1]
    d_ff = ffn_w_gate.shape[-1]
    assert d_ff % FF_BLOCK == 0
    rowv = lambda a: a.reshape(1, -1).astype(F32)
    inv = ROPE_BASE ** (-jnp.arange(0, HD, 2, dtype=F32) / HD)
    inv = jnp.concatenate([inv, inv]).reshape(1, HD)
    max_seq = max(x_prompt.shape[1], x_sample.shape[1])
    assert max_seq % ROW_TILE == 0
    rope = pl.pallas_call(
        _rope_table_kernel,
        grid=(max_seq // ROW_TILE,),
        in_specs=[_const_spec(inv.shape)],
        out_specs=[pl.BlockSpec((ROW_TILE, HD), lambda i: (i, 0))] * 2,
        out_shape=[jax.ShapeDtypeStruct((max_seq, HD), F32)] * 2,
        compiler_params=_params(("parallel",)),
        name="rope_tables",
    )(inv)

    assert depth == 1
    outs = [x_prompt, x_sample]
    for l in range(depth):
        wi = w_in[l].astype(BF16)
        n_gate = 2 * HEADS
        wgate = jnp.zeros((d, 2 * LANES), BF16)
        wgate = wgate.at[:, 0:n_gate].set(wi[:, 8 * GW:8 * GW + n_gate])
        wgate = wgate.at[:, LANES:LANES + n_gate].set(wi[:, 8 * GW + n_gate:8 * GW + 2 * n_gate])
        gb = mlstm_gate_b[l].astype(F32).reshape(-1)
        gbias = jnp.zeros((1, 2 * LANES), F32)
        gbias = gbias.at[0, 0:n_gate].set(gb[:n_gate]).at[0, LANES:LANES + n_gate].set(gb[n_gate:])
        w = {
            "rope": rope,
            "in_proj": [rowv(norm_mix_w[l]), wi[:, 0:4 * GW], wi[:, 4 * GW:6 * GW], wi[:, 6 * GW:8 * GW], wgate,
                        gbias, mlstm_conv_w[l].astype(F32), rowv(mlstm_conv_b[l])],
            "ret_logit": jnp.repeat(ret_decay_logit[l].astype(F32), HD, axis=1),
            "d_ff": d_ff,
            "mix_out": [jnp.concatenate([rowv(ret_gn_w[l]), rowv(mlstm_gn_w[l])], axis=1), w_out[l].astype(BF16),
                        rowv(norm_ffn_w[l])],
            "ffn": [ffn_w_gate[l].astype(BF16), ffn_w_up[l].astype(BF16), ffn_conv_w[l].astype(F32),
                    rowv(ffn_conv_b[l]), ffn_w_down[l].astype(BF16), rowv(ple_norm_w[l]), ple_w_gate[l].astype(BF16),
                    rowv(ple_gate_b[l]), ple_w_proj[l].astype(BF16), rowv(final_norm_w)],
        }
        outs = [_layer(h, p[l], w) for h, p in zip(outs, (p_prompt, p_sample))]
    return tuple(outs)
```

```python
import functools

import jax
import jax.numpy as jnp
from jax import lax
from jax.experimental import pallas as pl
from jax.experimental.pallas import tpu as pltpu

F32 = jnp.float32
BF16 = jnp.bfloat16

HEADS = 4
HD = 128
GW = HEADS * HD
CHUNK = 128
ROPE_BASE = 10000.0
RMS_EPS = 1e-6
HN_EPS = 1e-5
SUBLANES = 8
LANES = 128
ROW_TILE = 512
IN_TILE = 1024
FF_BLOCK = 256
MIX_CHUNKS = 8
MLSTM_GROUP = 2
VMEM_LIMIT = 56 * 1024 * 1024


def _rms(x, w):
    return x * lax.rsqrt(jnp.mean(x * x, axis=-1, keepdims=True) + RMS_EPS) * w


def _log_sigmoid(x):
    return jnp.minimum(x, 0.0) - jnp.log1p(jnp.exp(-jnp.abs(x)))


def _sigmoid(x):
    return 1.0 / (1.0 + jnp.exp(-x))


def _gelu(x):
    return 0.5 * x * (1.0 + lax.erf(x * (0.5 ** 0.5)))


def _dot(a, b):
    return jnp.dot(a, b, preferred_element_type=F32)


def _dot_nt(a, b):
    return lax.dot_general(a, b, (((1,), (1,)), ((), ())), preferred_element_type=F32)


def _ext_rows(main, nxt, prv, tin, tiles_per_seq):
    nxt = jnp.where(tin == tiles_per_seq - 1, 0.0, nxt)
    prv = jnp.where(tin == 0, 0.0, prv)
    return jnp.concatenate([main, nxt, prv], axis=0)


def _conv3(ext, rows, w, b):
    n = ext.shape[0]
    up = pltpu.roll(ext, 1, axis=0)[:rows]
    dn = pltpu.roll(ext, n - 1, axis=0)[:rows]
    return up * w[0:1] + ext[:rows] * w[1:2] + dn * w[2:3] + b


def _chunk_stats(ig, lf):
    tm = ig.shape[0]
    ii = lax.broadcasted_iota(jnp.int32, (CHUNK, CHUNK), 0)
    jj = lax.broadcasted_iota(jnp.int32, (CHUNK, CHUNK), 1)
    tril = (ii >= jj).astype(F32)
    triu = (jj >= ii).astype(F32)
    hi = lax.Precision.HIGHEST
    a = jnp.concatenate([
        jnp.where(jj < HEADS,
                  jnp.dot(tril, lf[r:r + CHUNK], precision=hi, preferred_element_type=F32),
                  jnp.dot(triu, lf[r:r + CHUNK], precision=hi, preferred_element_type=F32))
        for r in range(0, tm, CHUNK)], axis=0)
    b = ig - a
    r = lax.broadcasted_iota(jnp.int32, (tm, LANES), 0) % CHUNK
    pre = suf = b
    s = 1
    while s < CHUNK:
        pre = jnp.maximum(pre, jnp.where(r >= s, pltpu.roll(pre, s, axis=0), -jnp.inf))
        suf = jnp.maximum(suf, jnp.where(r < CHUNK - s, pltpu.roll(suf, tm - s, axis=0), -jnp.inf))
        s *= 2
    cm = jnp.where(lax.broadcasted_iota(jnp.int32, (tm, LANES), 1) < HEADS, pre, suf)
    return a, b, cm


def _rope_table_kernel(inv_ref, cos_ref, sin_ref):
    tm = cos_ref.shape[0]
    pos = (pl.program_id(0) * tm + lax.broadcasted_iota(jnp.int32, (tm, HD), 0)).astype(F32)
    ang = pos * inv_ref[...]
    cos_ref[...] = jnp.cos(ang)
    sin = jnp.sin(ang)
    lane = lax.broadcasted_iota(jnp.int32, (tm, HD), 1)
    sin_ref[...] = jnp.where(lane < HD // 2, -sin, sin)


def _in_proj_kernel(x_ref, xn_ref, xp_ref, cos_ref, sin_ref, nw_ref, wr_ref, wqk_ref, wm_ref, wg_ref, gb_ref, cw_ref,
                    cb_ref, qv_ref, kt_ref, og_ref, ac_ref, cm_ref, br_ref, *, tiles_per_seq):
    tm = x_ref.shape[0]
    tin = pl.program_id(0) % tiles_per_seq
    nw = nw_ref[...]
    a_ext = _ext_rows(_rms(x_ref[...], nw), _rms(xn_ref[...], nw), _rms(xp_ref[...], nw), tin,
                      tiles_per_seq).astype(BF16)
    a = a_ext[:tm]
    scale = HD ** -0.5

    PW = 2 * HD

    def rope(t, i):
        th = t[:, i * HD:(i + 1) * HD]
        return th * cos_ref[...] + pltpu.roll(th, HD // 2, axis=1) * sin_ref[...]

    def put_rq(t, lo):
        for i in range(PW // HD):
            qv_ref[:, lo + i * HD:lo + (i + 1) * HD] = (rope(t, i) * scale).astype(BF16)

    def put_keys(row_lo, k):
        kt = k.T.astype(BF16)
        for ch in range(tm // CHUNK):
            kt_ref[ch, row_lo:row_lo + HD, :] = kt[:, ch * CHUNK:(ch + 1) * CHUNK]

    def put_rk(t, lo):
        for i in range(PW // HD):
            put_keys(lo + i * HD, rope(t, i))

    def conv_silu(t, lo):
        c = _conv3(t, tm, cw_ref[:, lo:lo + PW], cb_ref[:, lo:lo + PW])
        return c * _sigmoid(c)

    def put_mq(t, lo):
        qv_ref[:, 2 * GW + lo:2 * GW + lo + PW] = (conv_silu(t, lo) * scale).astype(BF16)

    def put_mk(t, lo):
        c = conv_silu(t, GW + lo)
        for i in range(PW // HD):
            put_keys(GW + lo + i * HD, c[:, i * HD:(i + 1) * HD])

    def put_gates(t, lo):
        g = t + gb_ref[...]
        acum, b, cmax = _chunk_stats(g[:, :LANES], _log_sigmoid(g[:, LANES:]))
        ac_ref[...] = acum
        cm_ref[...] = cmax
        for ch in range(tm // CHUNK):
            br_ref[ch * SUBLANES:(ch + 1) * SUBLANES, :] = b[ch * CHUNK:(ch + 1) * CHUNK].T[:SUBLANES]

    def put(ref, base):
        def store(t, lo):
            ref[:, base + lo:base + lo + PW] = t.astype(BF16)
        return store

    def pieces(lhs, w_ref, w_lo, consume):
        return [(lhs, w_ref, w_lo + lo, consume, lo) for lo in range(0, GW, PW)]

    rq, rk = pieces(a, wr_ref, 0, put_rq), pieces(a, wr_ref, GW, put_rk)
    mq, mk = pieces(a_ext, wqk_ref, 0, put_mq), pieces(a_ext, wqk_ref, GW, put_mk)
    rv, rg = pieces(a, wr_ref, 2 * GW, put(qv_ref, GW)), pieces(a, wr_ref, 3 * GW, put(og_ref, 0))
    mv, mo = pieces(a, wm_ref, 0, put(qv_ref, 3 * GW)), pieces(a, wm_ref, GW, put(og_ref, GW))
    gates = [(a, wg_ref, 0, put_gates, 0)]
    order = [(gates[0],), (mq[0], rv[0]), (rq[0], rv[1]), (mq[1], rg[0]), (rq[1], rg[1]), (mk[0], mv[0]),
             (rk[0], mv[1]), (mk[1], mo[0]), (rk[1], mo[1])]
    project = lambda stage: [_dot(piece[0], piece[1][:, piece[2]:piece[2] + PW]) for piece in stage]
    pending = project(order[0])
    for k, stage in enumerate(order):
        ready = pending
        if k + 1 < len(order):
            pending = project(order[k + 1])
        for piece, t in zip(stage, ready):
            piece[3](t, piece[4])


def _mixer_kernel(lg_ref, qf_ref, ktf_ref, acf_ref, cmf_ref, brf_ref, qb_ref, ktb_ref, acb_ref, cmb_ref, brb_ref,
                  of_ref, ob_ref, dm_ref, qdf_ref, qdb_ref, kd_ref, cd_ref, sf_ref, sb_ref, c_ref, m_ref):
    L = CHUNK
    ii = lax.broadcasted_iota(jnp.int32, (L, L), 0)
    jj = lax.broadcasted_iota(jnp.int32, (L, L), 1)

    @pl.when(pl.program_id(1) == 0)
    def _init():
        lg = _log_sigmoid(lg_ref[...])
        lgf, lgb = lg[0:1], lg[1:2]
        row = lax.broadcasted_iota(jnp.int32, (L, GW), 0).astype(F32)
        qdf_ref[...] = jnp.exp((row + 1.0) * lgf)
        qdb_ref[...] = jnp.exp((L - row) * lgb)
        cd_ref[...] = jnp.exp(float(L) * lg)
        diff = (ii - jj).astype(F32)
        col = lax.broadcasted_iota(jnp.int32, (1, L), 1).astype(F32)
        for h in range(HEADS):
            sl = slice(h * HD, (h + 1) * HD)
            dm_ref[h] = jnp.where(diff >= 0, jnp.exp(diff * lgf[:, sl]), jnp.exp(-diff * lgb[:, sl]))
            kd_ref[h:h + 1, :] = jnp.exp((L - 1.0 - col) * lgf[:, sl])
            kd_ref[HEADS + h:HEADS + h + 1, :] = jnp.exp(col * lgb[:, sl])
        sf_ref[...] = jnp.zeros_like(sf_ref)
        sb_ref[...] = jnp.zeros_like(sb_ref)
        c_ref[...] = jnp.zeros_like(c_ref)
        m_ref[...] = jnp.zeros_like(m_ref)

    n_sub = qf_ref.shape[0] // L
    H = range(HEADS)
    C = range(2 * HEADS)
    sls = [slice(h * HD, (h + 1) * HD) for h in H]
    tril = ii >= jj
    triu = jj >= ii
    is_f = jj < HEADS
    ones_blk = jnp.ones((L, HD), BF16)

    def chunk(i, carry):
        i_b = n_sub - 1 - i
        rows_f = pl.ds(pl.multiple_of(i * L, L), L)
        rows_b = pl.ds(pl.multiple_of(i_b * L, L), L)
        part = lambda ref, rows, g, h: ref[rows, g * GW + h * HD:g * GW + (h + 1) * HD]

        def retention():
            qf, vf = (lambda h: part(qf_ref, rows_f, 0, h)), (lambda h: part(qf_ref, rows_f, 1, h))
            qb, vb = (lambda h: part(qb_ref, rows_b, 0, h)), (lambda h: part(qb_ref, rows_b, 1, h))
            ktf = lambda h: ktf_ref[i, sls[h], :]
            ktb = lambda h: ktb_ref[i_b, sls[h], :]
            p = [(_dot(qf(h), ktf(h)) * dm_ref[h]).astype(BF16) for h in H]
            cross_f = [_dot(qf(h), sf_ref[h].astype(BF16)) for h in H]
            cross_b = [_dot(qb(h), sb_ref[h].astype(BF16)) for h in H]
            inner = [_dot(p[h], vf(h)) for h in H]
            for h in H:
                of_ref[rows_f, sls[h]] = (inner[h] + cross_f[h] * qdf_ref[:, sls[h]]).astype(of_ref.dtype)
                ob_ref[rows_b, sls[h]] = (cross_b[h] * qdb_ref[:, sls[h]]).astype(ob_ref.dtype)
            upd_f = [_dot((ktf(h).astype(F32) * kd_ref[h:h + 1, :]).astype(BF16), vf(h)) for h in H]
            upd_b = [_dot((ktb(h).astype(F32) * kd_ref[HEADS + h:HEADS + h + 1, :]).astype(BF16), vb(h)) for h in H]
            for h in H:
                sf_ref[h] = cd_ref[0:1, sls[h]] * sf_ref[h] + upd_f[h]
                sb_ref[h] = cd_ref[1:2, sls[h]] * sb_ref[h] + upd_b[h]

        a_all = jnp.where(is_f, acf_ref[rows_f, :], acb_ref[rows_b, :])
        cm_all = jnp.where(is_f, cmf_ref[rows_f, :], cmb_ref[rows_b, :])
        b_rows = jnp.where(lax.broadcasted_iota(jnp.int32, (SUBLANES, L), 0) < HEADS,
                           brf_ref[pl.ds(pl.multiple_of(i * SUBLANES, SUBLANES), SUBLANES), :],
                           brb_ref[pl.ds(pl.multiple_of(i_b * SUBLANES, SUBLANES), SUBLANES), :])

        def mlstm_group(cs):
            back = {c: c >= HEADS for c in cs}
            hd = {c: c % HEADS for c in cs}
            last = {c: 0 if back[c] else L - 1 for c in cs}
            q = lambda c: part(qb_ref, rows_b, 2, hd[c]) if back[c] else part(qf_ref, rows_f, 2, hd[c])
            v = lambda c: part(qb_ref, rows_b, 3, hd[c]) if back[c] else part(qf_ref, rows_f, 3, hd[c])
            kt = lambda c: (ktb_ref[i_b, GW + hd[c] * HD:GW + (hd[c] + 1) * HD, :] if back[c] else
                            ktf_ref[i, GW + hd[c] * HD:GW + (hd[c] + 1) * HD, :])
            v1 = lambda c: jnp.concatenate([v(c), ones_blk], axis=1)
            m_old = {c: m_ref[c:c + 1, 0:1] for c in cs}
            mu = {c: jnp.maximum(jnp.broadcast_to(cm_all[:, c:c + 1], (L, L)), m_old[c]) for c in cs}
            e = {c: jnp.exp(jnp.where(triu if back[c] else tril, b_rows[c:c + 1, :] - mu[c], -jnp.inf)) for c in cs}
            qk = {c: _dot(q(c), kt(c)) for c in cs}
            s = {c: (qk[c] * e[c]).astype(BF16) for c in cs}
            qc = {c: _dot(q(c), c_ref[c].astype(BF16)) for c in cs}
            sv = {c: _dot(s[c], v1(c)) for c in cs}
            for c in cs:
                w = jnp.exp(m_old[c] - mu[c])
                floor = jnp.exp(-(jnp.broadcast_to(a_all[:, c:c + 1], (L, L)) + mu[c]))
                den = jnp.maximum(jnp.abs(w * qc[c][:, HD:] + sv[c][:, HD:]), floor)
                dst, rows = (ob_ref, rows_b) if back[c] else (of_ref, rows_f)
                h_c = (w * qc[c][:, :HD] + sv[c][:, :HD]) / den
                dst[rows, GW + hd[c] * HD:GW + (hd[c] + 1) * HD] = h_c.astype(dst.dtype)
            mu_last = {c: mu[c][last[c]:last[c] + 1, 0:1] for c in cs}
            kw = {c: (kt(c).astype(F32) * jnp.exp(b_rows[c:c + 1, :] - mu_last[c])).astype(BF16) for c in cs}
            upd = {c: _dot(kw[c], v1(c)) for c in cs}
            for c in cs:
                c_ref[c] = jnp.exp(m_old[c] - mu_last[c]) * c_ref[c] + upd[c]
                m_ref[c:c + 1, :] = jnp.broadcast_to(a_all[last[c]:last[c] + 1, c:c + 1] + mu_last[c], (1, LANES))

        retention()
        for g in range(0, 2 * HEADS, MLSTM_GROUP):
            mlstm_group(range(g, g + MLSTM_GROUP))
        return carry

    lax.fori_loop(0, n_sub, chunk, 0)


def _mix_out_pieces(rows, tin, tiles_per_seq, gn_ref, wo_ref, nfw_ref, emit):
    of_ref, ofn_ref, ofp_ref, ob_ref, obn_ref, obp_ref, og_ref, ogn_ref, ogp_ref, x_ref, xn_ref, xp_ref = rows
    tm = x_ref.shape[0]
    n_ext = tm + 2 * SUBLANES
    v = {}

    def ext(m, nx, pv):
        return jnp.concatenate([m[...].astype(F32), nx[...].astype(F32)[:SUBLANES],
                                pv[...].astype(F32)[pv.shape[0] - SUBLANES:]], axis=0)

    def load():
        v["y"] = ext(of_ref, ofn_ref, ofp_ref) + ext(ob_ref, obn_ref, obp_ref)
        v["z"] = ext(og_ref, ogn_ref, ogp_ref)
        v["h"] = ext(x_ref, xn_ref, xp_ref)
        v["parts"] = []

    def head(g):
        sl = slice(g * HD, (g + 1) * HD)
        yg = v["y"][:, sl]
        d = yg - jnp.mean(yg, axis=-1, keepdims=True)
        n = d * lax.rsqrt(jnp.mean(d * d, axis=-1, keepdims=True) + HN_EPS) * gn_ref[:, sl]
        zg = v["z"][:, sl]
        gate = zg * _sigmoid(zg) if g < HEADS else _sigmoid(zg)
        v["parts"].append((n * gate).astype(BF16))
        if g % 2 == 1:
            pair = jnp.concatenate(v["parts"][-2:], axis=1)
            v["h"] = v["h"] + _dot(pair, wo_ref[(g - 1) * HD:(g + 1) * HD, :])

    def finish():
        r = lax.broadcasted_iota(jnp.int32, (n_ext, 1), 0)
        pad = ((r >= tm) & (r < tm + SUBLANES) & (tin == tiles_per_seq - 1)) | ((r >= tm + SUBLANES) & (tin == 0))
        emit(jnp.where(pad, 0.0, _rms(v["h"], nfw_ref[...])).astype(BF16), v["h"][:tm])

    return [load] + [functools.partial(head, g) for g in range(2 * HEADS)] + [finish]


def _out_ffn_kernel(*refs, tiles_per_seq):
    rows = refs[:12]
    (p_ref, gn_ref, wo_ref, nfw_ref, wg_ref, wu_ref, cw_ref, cb_ref, wd_ref, pnw_ref, pwg_ref, pgb_ref, pwp_ref, fnw_ref,
     y_ref, act_ref) = refs[12:]
    tm = p_ref.shape[0]
    stage = {}
    tin = pl.program_id(0) % tiles_per_seq
    for piece in _mix_out_pieces(rows, tin, tiles_per_seq, gn_ref, wo_ref, nfw_ref,
                                 lambda f_ext, h: stage.update(f_ext=f_ext, h=h)):
        piece()
    f_ext = stage["f_ext"]
    f = f_ext[:tm]

    fb = FF_BLOCK
    nblk = wg_ref.shape[1] // fb
    blk = lambda ref, j: ref[:, j * fb:(j + 1) * fb]
    proj = lambda j: (_dot(f_ext, blk(wg_ref, j)), _dot(f, blk(wu_ref, j)))
    nxt = proj(0)
    for j in range(nblk):
        u_ext, up = nxt
        if j + 1 < nblk:
            nxt = proj(j + 1)
        act_ref[:, j * fb:(j + 1) * fb] = (_gelu(_conv3(u_ext, tm, blk(cw_ref, j), blk(cb_ref, j))) * up).astype(BF16)
    h = stage["h"] + _dot(act_ref[...], wd_ref[...])
    gate = _sigmoid(_dot(_rms(h, pnw_ref[...]).astype(BF16), pwg_ref[...]) + pgb_ref[...])
    h = h + _dot(p_ref[...].astype(BF16), pwp_ref[...]) * gate
    y_ref[...] = _rms(h, fnw_ref[...])


def _const_spec(shape):
    nd = len(shape)
    return pl.BlockSpec(shape, lambda *_: (0,) * nd, pipeline_mode=pl.Buffered(1))


def _row_specs(tm, width, n_rows, halo=SUBLANES):
    per = tm // halo
    last = n_rows // halo - 1
    return [
        pl.BlockSpec((tm, width), lambda i: (i, 0)),
        pl.BlockSpec((halo, width), lambda i: (jnp.minimum((i + 1) * per, last), 0)),
        pl.BlockSpec((halo, width), lambda i: (jnp.maximum(i * per - 1, 0), 0)),
    ]


def _params(sem):
    return pltpu.CompilerParams(dimension_semantics=sem, vmem_limit_bytes=VMEM_LIMIT)


def _layer(x, p, w):
    bsz, seq, d = x.shape
    n = bsz * seq
    tm = ROW_TILE
    assert seq % tm == 0 and seq % CHUNK == 0
    tiles_per_seq = seq // tm
    x2 = x.reshape(n, d)
    row = lambda width: pl.BlockSpec((tm, width), lambda i: (i, 0))

    rows_per_chunk = SUBLANES
    ti = IN_TILE
    assert seq % ti == 0
    row_i = lambda width: pl.BlockSpec((ti, width), lambda i: (i, 0))
    pos_tile = pl.BlockSpec((ti, HD), lambda i: (i % (seq // ti), 0))
    qv, kt, og, acum, cmax, brow = pl.pallas_call(
        functools.partial(_in_proj_kernel, tiles_per_seq=seq // ti),
        grid=(n // ti,),
        in_specs=_row_specs(ti, d, n) + [pos_tile, pos_tile] + [_const_spec(a.shape) for a in w["in_proj"]],
        out_specs=[row_i(4 * GW), pl.BlockSpec((ti // CHUNK, 2 * GW, CHUNK), lambda i: (i, 0, 0)), row_i(2 * GW),
                   row_i(LANES), row_i(LANES), pl.BlockSpec((ti // CHUNK * rows_per_chunk, LANES), lambda i: (i, 0))],
        out_shape=[jax.ShapeDtypeStruct((n, 4 * GW), BF16), jax.ShapeDtypeStruct((n // CHUNK, 2 * GW, CHUNK), BF16),
                   jax.ShapeDtypeStruct((n, 2 * GW), BF16), jax.ShapeDtypeStruct((n, LANES), F32),
                   jax.ShapeDtypeStruct((n, LANES), F32),
                   jax.ShapeDtypeStruct((n // CHUNK * rows_per_chunk, LANES), F32)],
        compiler_params=_params(("parallel",)),
        name="in_proj",
    )(x2, x2, x2, *w["rope"], *w["in_proj"])

    cb = MIX_CHUNKS
    assert seq % (cb * CHUNK) == 0
    nb = seq // (cb * CHUNK)
    fwd_i = lambda b, t: b * nb + t
    bwd_i = lambda b, t: b * nb + nb - 1 - t

    def block_specs(idx):
        return [pl.BlockSpec((cb * CHUNK, 4 * GW), lambda b, t: (idx(b, t), 0)),
                pl.BlockSpec((cb, 2 * GW, CHUNK), lambda b, t: (idx(b, t), 0, 0)),
                pl.BlockSpec((cb * CHUNK, LANES), lambda b, t: (idx(b, t), 0)),
                pl.BlockSpec((cb * CHUNK, LANES), lambda b, t: (idx(b, t), 0)),
                pl.BlockSpec((cb * rows_per_chunk, LANES), lambda b, t: (idx(b, t), 0))]

    out_f, out_b = pl.pallas_call(
        _mixer_kernel,
        grid=(bsz, nb),
        in_specs=[_const_spec(w["ret_logit"].shape)] + block_specs(fwd_i) + block_specs(bwd_i),
        out_specs=[pl.BlockSpec((cb * CHUNK, 2 * GW), lambda b, t: (fwd_i(b, t), 0)),
                   pl.BlockSpec((cb * CHUNK, 2 * GW), lambda b, t: (bwd_i(b, t), 0))],
        out_shape=[jax.ShapeDtypeStruct((n, 2 * GW), F32)] * 2,
        scratch_shapes=[
            pltpu.VMEM((HEADS, CHUNK, CHUNK), F32),
            pltpu.VMEM((CHUNK, GW), F32), pltpu.VMEM((CHUNK, GW), F32),
            pltpu.VMEM((2 * HEADS, CHUNK), F32),
            pltpu.VMEM((2, GW), F32),
            pltpu.VMEM((HEADS, HD, HD), F32), pltpu.VMEM((HEADS, HD, HD), F32),
            pltpu.VMEM((2 * HEADS, HD, 2 * HD), F32),
            pltpu.VMEM((2 * HEADS, LANES), F32),
        ],
        compiler_params=_params(("arbitrary", "arbitrary")),
        name="mixers",
    )(w["ret_logit"], *([qv, kt, acum, cmax, brow] * 2))

    p2 = p.reshape(n, p.shape[-1])
    bf16_rows = 2 * SUBLANES
    mix_rows = _row_specs(tm, d, n) * 2 + _row_specs(tm, d, n, bf16_rows) + _row_specs(tm, d, n)
    y = pl.pallas_call(
        functools.partial(_out_ffn_kernel, tiles_per_seq=tiles_per_seq),
        grid=(n // tm,),
        in_specs=mix_rows + [row(p2.shape[-1])] + [_const_spec(a.shape) for a in (*w["mix_out"], *w["ffn"])],
        out_specs=row(d),
        out_shape=jax.ShapeDtypeStruct((n, d), F32),
        scratch_shapes=[pltpu.VMEM((tm, w["d_ff"]), BF16)],
        compiler_params=_params(("parallel",)),
        name="out_ffn",
    )(*((out_f,) * 3 + (out_b,) * 3 + (og,) * 3 + (x2,) * 3), p2, *w["mix_out"], *w["ffn"])
    return y.reshape(bsz, seq, d)


def kernel(x_prompt, x_sample, p_prompt, p_sample, norm_mix_w, w_in, mlstm_conv_w, mlstm_conv_b, mlstm_gate_b, ret_decay_logit, ret_gn_w, mlstm_gn_w, w_out, norm_ffn_w, ffn_w_gate, ffn_w_up, ffn_conv_w, ffn_conv_b, ffn_w_down, ple_w_proj, ple_norm_w, ple_w_gate, ple_gate_b, final_norm_w):
    depth = w_in.shape[0]
    d = x_prompt.shape[-1]
    d_ff = ffn_w_gate.shape[-1]
    assert d_ff % FF_BLOCK == 0
    rowv = lambda a: a.reshape(1, -1).astype(F32)
    inv = ROPE_BASE ** (-jnp.arange(0, HD, 2, dtype=F32) / HD)
    inv = jnp.concatenate([inv, inv]).reshape(1, HD)
    max_seq = max(x_prompt.shape[1], x_sample.shape[1])
    assert max_seq % ROW_TILE == 0
    rope = pl.pallas_call(
        _rope_table_kernel,
        grid=(max_seq // ROW_TILE,),
        in_specs=[_const_spec(inv.shape)],
        out_specs=[pl.BlockSpec((ROW_TILE, HD), lambda i: (i, 0))] * 2,
        out_shape=[jax.ShapeDtypeStruct((max_seq, HD), F32)] * 2,
        compiler_params=_params(("parallel",)),
        name="rope_tables",
    )(inv)

    assert depth == 1
    outs = [x_prompt, x_sample]
    for l in range(depth):
        wi = w_in[l].astype(BF16)
        n_gate = 2 * HEADS
        wgate = jnp.zeros((d, 2 * LANES), BF16)
        wgate = wgate.at[:, 0:n_gate].set(wi[:, 8 * GW:8 * GW + n_gate])
        wgate = wgate.at[:, LANES:LANES + n_gate].set(wi[:, 8 * GW + n_gate:8 * GW + 2 * n_gate])
        gb = mlstm_gate_b[l].astype(F32).reshape(-1)
        gbias = jnp.zeros((1, 2 * LANES), F32)
        gbias = gbias.at[0, 0:n_gate].set(gb[:n_gate]).at[0, LANES:LANES + n_gate].set(gb[n_gate:])
        w = {
            "rope": rope,
            "in_proj": [rowv(norm_mix_w[l]), wi[:, 0:4 * GW], wi[:, 4 * GW:6 * GW], wi[:, 6 * GW:8 * GW], wgate,
                        gbias, mlstm_conv_w[l].astype(F32), rowv(mlstm_conv_b[l])],
            "ret_logit": jnp.repeat(ret_decay_logit[l].astype(F32), HD, axis=1),
            "d_ff": d_ff,
            "mix_out": [jnp.concatenate([rowv(ret_gn_w[l]), rowv(mlstm_gn_w[l])], axis=1), w_out[l].astype(BF16),
                        rowv(norm_ffn_w[l])],
            "ffn": [ffn_w_gate[l].astype(BF16), ffn_w_up[l].astype(BF16), ffn_conv_w[l].astype(F32),
                    rowv(ffn_conv_b[l]), ffn_w_down[l].astype(BF16), rowv(ple_norm_w[l]), ple_w_gate[l].astype(BF16),
                    rowv(ple_gate_b[l]), ple_w_proj[l].astype(BF16), rowv(final_norm_w)],
        }
        outs = [_layer(h, p[l], w) for h, p in zip(outs, (p_prompt, p_sample))]
    return tuple(outs)
```

```python
import functools

import jax
import jax.numpy as jnp
from jax import lax
from jax.experimental import pallas as pl
from jax.experimental.pallas import tpu as pltpu

F32 = jnp.float32
BF16 = jnp.bfloat16

HEADS = 4
HD = 128
GW = HEADS * HD
CHUNK = 128
ROPE_BASE = 10000.0
RMS_EPS = 1e-6
HN_EPS = 1e-5
SUBLANES = 8
LANES = 128
ROW_TILE = 512
IN_TILE = 1024
FF_BLOCK = 256
MIX_CHUNKS = 8
MLSTM_GROUP = 4
VMEM_LIMIT = 56 * 1024 * 1024


def _rms(x, w):
    return x * lax.rsqrt(jnp.mean(x * x, axis=-1, keepdims=True) + RMS_EPS) * w


def _log_sigmoid(x):
    return jnp.minimum(x, 0.0) - jnp.log1p(jnp.exp(-jnp.abs(x)))


def _sigmoid(x):
    return 1.0 / (1.0 + jnp.exp(-x))


def _gelu(x):
    return 0.5 * x * (1.0 + lax.erf(x * (0.5 ** 0.5)))


def _dot(a, b):
    return jnp.dot(a, b, preferred_element_type=F32)


def _dot_nt(a, b):
    return lax.dot_general(a, b, (((1,), (1,)), ((), ())), preferred_element_type=F32)


def _ext_rows(main, nxt, prv, tin, tiles_per_seq):
    nxt = jnp.where(tin == tiles_per_seq - 1, 0.0, nxt)
    prv = jnp.where(tin == 0, 0.0, prv)
    return jnp.concatenate([main, nxt, prv], axis=0)


def _conv3(ext, rows, w, b):
    n = ext.shape[0]
    up = pltpu.roll(ext, 1, axis=0)[:rows]
    dn = pltpu.roll(ext, n - 1, axis=0)[:rows]
    return up * w[0:1] + ext[:rows] * w[1:2] + dn * w[2:3] + b


def _chunk_stats(ig, lf):
    tm = ig.shape[0]
    ii = lax.broadcasted_iota(jnp.int32, (CHUNK, CHUNK), 0)
    jj = lax.broadcasted_iota(jnp.int32, (CHUNK, CHUNK), 1)
    tril = (ii >= jj).astype(F32)
    triu = (jj >= ii).astype(F32)
    hi = lax.Precision.HIGHEST
    a = jnp.concatenate([
        jnp.where(jj < HEADS,
                  jnp.dot(tril, lf[r:r + CHUNK], precision=hi, preferred_element_type=F32),
                  jnp.dot(triu, lf[r:r + CHUNK], precision=hi, preferred_element_type=F32))
        for r in range(0, tm, CHUNK)], axis=0)
    b = ig - a
    r = lax.broadcasted_iota(jnp.int32, (tm, LANES), 0) % CHUNK
    pre = suf = b
    s = 1
    while s < CHUNK:
        pre = jnp.maximum(pre, jnp.where(r >= s, pltpu.roll(pre, s, axis=0), -jnp.inf))
        suf = jnp.maximum(suf, jnp.where(r < CHUNK - s, pltpu.roll(suf, tm - s, axis=0), -jnp.inf))
        s *= 2
    cm = jnp.where(lax.broadcasted_iota(jnp.int32, (tm, LANES), 1) < HEADS, pre, suf)
    return a, b, cm


def _rope_table_kernel(inv_ref, cos_ref, sin_ref):
    tm = cos_ref.shape[0]
    pos = (pl.program_id(0) * tm + lax.broadcasted_iota(jnp.int32, (tm, HD), 0)).astype(F32)
    ang = pos * inv_ref[...]
    cos_ref[...] = jnp.cos(ang)
    sin = jnp.sin(ang)
    lane = lax.broadcasted_iota(jnp.int32, (tm, HD), 1)
    sin_ref[...] = jnp.where(lane < HD // 2, -sin, sin)


def _in_proj_kernel(x_ref, xn_ref, xp_ref, cos_ref, sin_ref, nw_ref, wr_ref, wqk_ref, wm_ref, wg_ref, gb_ref, cw_ref,
                    cb_ref, qv_ref, kt_ref, og_ref, ac_ref, cm_ref, br_ref, *, tiles_per_seq):
    tm = x_ref.shape[0]
    tin = pl.program_id(0) % tiles_per_seq
    nw = nw_ref[...]
    a_ext = _ext_rows(_rms(x_ref[...], nw), _rms(xn_ref[...], nw), _rms(xp_ref[...], nw), tin,
                      tiles_per_seq).astype(BF16)
    a = a_ext[:tm]
    scale = HD ** -0.5

    PW = 2 * HD

    def rope(t, i):
        th = t[:, i * HD:(i + 1) * HD]
        return th * cos_ref[...] + pltpu.roll(th, HD // 2, axis=1) * sin_ref[...]

    def put_rq(t, lo):
        for i in range(PW // HD):
            qv_ref[:, lo + i * HD:lo + (i + 1) * HD] = (rope(t, i) * scale).astype(BF16)

    def put_keys(row_lo, k):
        kt = k.T.astype(BF16)
        for ch in range(tm // CHUNK):
            kt_ref[ch, row_lo:row_lo + HD, :] = kt[:, ch * CHUNK:(ch + 1) * CHUNK]

    def put_rk(t, lo):
        for i in range(PW // HD):
            put_keys(lo + i * HD, rope(t, i))

    def conv_silu(t, lo):
        c = _conv3(t, tm, cw_ref[:, lo:lo + PW], cb_ref[:, lo:lo + PW])
        return c * _sigmoid(c)

    def put_mq(t, lo):
        qv_ref[:, 2 * GW + lo:2 * GW + lo + PW] = (conv_silu(t, lo) * scale).astype(BF16)

    def put_mk(t, lo):
        c = conv_silu(t, GW + lo)
        for i in range(PW // HD):
            put_keys(GW + lo + i * HD, c[:, i * HD:(i + 1) * HD])

    def put_gates(t, lo):
        g = t + gb_ref[...]
        acum, b, cmax = _chunk_stats(g[:, :LANES], _log_sigmoid(g[:, LANES:]))
        ac_ref[...] = acum
        cm_ref[...] = cmax
        for ch in range(tm // CHUNK):
            br_ref[ch * SUBLANES:(ch + 1) * SUBLANES, :] = b[ch * CHUNK:(ch + 1) * CHUNK].T[:SUBLANES]

    def put(ref, base):
        def store(t, lo):
            ref[:, base + lo:base + lo + PW] = t.astype(BF16)
        return store

    def pieces(lhs, w_ref, w_lo, consume):
        return [(lhs, w_ref, w_lo + lo, consume, lo) for lo in range(0, GW, PW)]

    rq, rk = pieces(a, wr_ref, 0, put_rq), pieces(a, wr_ref, GW, put_rk)
    mq, mk = pieces(a_ext, wqk_ref, 0, put_mq), pieces(a_ext, wqk_ref, GW, put_mk)
    rv, rg = pieces(a, wr_ref, 2 * GW, put(qv_ref, GW)), pieces(a, wr_ref, 3 * GW, put(og_ref, 0))
    mv, mo = pieces(a, wm_ref, 0, put(qv_ref, 3 * GW)), pieces(a, wm_ref, GW, put(og_ref, GW))
    gates = [(a, wg_ref, 0, put_gates, 0)]
    order = [(gates[0],), (mq[0], rv[0]), (rq[0], rv[1]), (mq[1], rg[0]), (rq[1], rg[1]), (mk[0], mv[0]),
             (rk[0], mv[1]), (mk[1], mo[0]), (rk[1], mo[1])]
    project = lambda stage: [_dot(piece[0], piece[1][:, piece[2]:piece[2] + PW]) for piece in stage]
    pending = project(order[0])
    for k, stage in enumerate(order):
        ready = pending
        if k + 1 < len(order):
            pending = project(order[k + 1])
        for piece, t in zip(stage, ready):
            piece[3](t, piece[4])


def _mixer_kernel(lg_ref, qf_ref, ktf_ref, acf_ref, cmf_ref, brf_ref, qb_ref, ktb_ref, acb_ref, cmb_ref, brb_ref,
                  of_ref, ob_ref, dm_ref, qdf_ref, qdb_ref, kd_ref, cd_ref, sf_ref, sb_ref, c_ref, m_ref):
    L = CHUNK
    ii = lax.broadcasted_iota(jnp.int32, (L, L), 0)
    jj = lax.broadcasted_iota(jnp.int32, (L, L), 1)

    @pl.when(pl.program_id(1) == 0)
    def _init():
        lg = _log_sigmoid(lg_ref[...])
        lgf, lgb = lg[0:1], lg[1:2]
        row = lax.broadcasted_iota(jnp.int32, (L, GW), 0).astype(F32)
        qdf_ref[...] = jnp.exp((row + 1.0) * lgf)
        qdb_ref[...] = jnp.exp((L - row) * lgb)
        cd_ref[...] = jnp.exp(float(L) * lg)
        diff = (ii - jj).astype(F32)
        col = lax.broadcasted_iota(jnp.int32, (1, L), 1).astype(F32)
        for h in range(HEADS):
            sl = slice(h * HD, (h + 1) * HD)
            dm_ref[h] = jnp.where(diff >= 0, jnp.exp(diff * lgf[:, sl]), jnp.exp(-diff * lgb[:, sl]))
            kd_ref[h:h + 1, :] = jnp.exp((L - 1.0 - col) * lgf[:, sl])
            kd_ref[HEADS + h:HEADS + h + 1, :] = jnp.exp(col * lgb[:, sl])
        sf_ref[...] = jnp.zeros_like(sf_ref)
        sb_ref[...] = jnp.zeros_like(sb_ref)
        c_ref[...] = jnp.zeros_like(c_ref)
        m_ref[...] = jnp.zeros_like(m_ref)

    n_sub = qf_ref.shape[0] // L
    H = range(HEADS)
    C = range(2 * HEADS)
    sls = [slice(h * HD, (h + 1) * HD) for h in H]
    tril = ii >= jj
    triu = jj >= ii
    is_f = jj < HEADS
    ones_blk = jnp.ones((L, HD), BF16)

    def chunk(i, carry):
        i_b = n_sub - 1 - i
        rows_f = pl.ds(pl.multiple_of(i * L, L), L)
        rows_b = pl.ds(pl.multiple_of(i_b * L, L), L)
        part = lambda ref, rows, g, h: ref[rows, g * GW + h * HD:g * GW + (h + 1) * HD]

        def retention():
            qf, vf = (lambda h: part(qf_ref, rows_f, 0, h)), (lambda h: part(qf_ref, rows_f, 1, h))
            qb, vb = (lambda h: part(qb_ref, rows_b, 0, h)), (lambda h: part(qb_ref, rows_b, 1, h))
            ktf = lambda h: ktf_ref[i, sls[h], :]
            ktb = lambda h: ktb_ref[i_b, sls[h], :]
            qkc = [_dot(qf(h), jnp.concatenate([ktf(h), sf_ref[h].astype(BF16)], axis=1)) for h in H]
            p = [(qkc[h][:, :L] * dm_ref[h]).astype(BF16) for h in H]
            cross_b = [_dot(qb(h), sb_ref[h].astype(BF16)) for h in H]
            kdf = [(ktf(h).astype(F32) * kd_ref[h:h + 1, :]).astype(BF16) for h in H]
            pv = [_dot(jnp.concatenate([p[h], kdf[h]], axis=0), vf(h)) for h in H]
            for h in H:
                of_ref[rows_f, sls[h]] = (pv[h][:L] + qkc[h][:, L:] * qdf_ref[:, sls[h]]).astype(of_ref.dtype)
                ob_ref[rows_b, sls[h]] = (cross_b[h] * qdb_ref[:, sls[h]]).astype(ob_ref.dtype)
            upd_b = [_dot((ktb(h).astype(F32) * kd_ref[HEADS + h:HEADS + h + 1, :]).astype(BF16), vb(h)) for h in H]
            for h in H:
                sf_ref[h] = cd_ref[0:1, sls[h]] * sf_ref[h] + pv[h][L:]
                sb_ref[h] = cd_ref[1:2, sls[h]] * sb_ref[h] + upd_b[h]

        a_all = jnp.where(is_f, acf_ref[rows_f, :], acb_ref[rows_b, :])
        cm_all = jnp.where(is_f, cmf_ref[rows_f, :], cmb_ref[rows_b, :])
        b_rows = jnp.where(lax.broadcasted_iota(jnp.int32, (SUBLANES, L), 0) < HEADS,
                           brf_ref[pl.ds(pl.multiple_of(i * SUBLANES, SUBLANES), SUBLANES), :],
                           brb_ref[pl.ds(pl.multiple_of(i_b * SUBLANES, SUBLANES), SUBLANES), :])

        def mlstm_group(cs):
            back = {c: c >= HEADS for c in cs}
            hd = {c: c % HEADS for c in cs}
            last = {c: 0 if back[c] else L - 1 for c in cs}
            q = lambda c: part(qb_ref, rows_b, 2, hd[c]) if back[c] else part(qf_ref, rows_f, 2, hd[c])
            v = lambda c: part(qb_ref, rows_b, 3, hd[c]) if back[c] else part(qf_ref, rows_f, 3, hd[c])
            kt = lambda c: (ktb_ref[i_b, GW + hd[c] * HD:GW + (hd[c] + 1) * HD, :] if back[c] else
                            ktf_ref[i, GW + hd[c] * HD:GW + (hd[c] + 1) * HD, :])
            v1 = lambda c: jnp.concatenate([v(c), ones_blk], axis=1)
            m_old = {c: m_ref[c:c + 1, 0:1] for c in cs}
            mu = {c: jnp.maximum(jnp.broadcast_to(cm_all[:, c:c + 1], (L, L)), m_old[c]) for c in cs}
            e = {c: jnp.exp(jnp.where(triu if back[c] else tril, b_rows[c:c + 1, :] - mu[c], -jnp.inf)) for c in cs}
            qk = {c: _dot(q(c), kt(c)) for c in cs}
            s = {c: (qk[c] * e[c]).astype(BF16) for c in cs}
            qc = {c: _dot(q(c), c_ref[c].astype(BF16)) for c in cs}
            mu_last = {c: mu[c][last[c]:last[c] + 1, 0:1] for c in cs}
            kw = {c: (kt(c).astype(F32) * jnp.exp(b_rows[c:c + 1, :] - mu_last[c])).astype(BF16) for c in cs}
            sv = {c: _dot(jnp.concatenate([s[c], kw[c]], axis=0), v1(c)) for c in cs}
            for c in cs:
                w = jnp.exp(m_old[c] - mu[c])
                floor = jnp.exp(-(jnp.broadcast_to(a_all[:, c:c + 1], (L, L)) + mu[c]))
                den = jnp.maximum(jnp.abs(w * qc[c][:, HD:] + sv[c][:L, HD:]), floor)
                dst, rows = (ob_ref, rows_b) if back[c] else (of_ref, rows_f)
                h_c = (w * qc[c][:, :HD] + sv[c][:L, :HD]) / den
                dst[rows, GW + hd[c] * HD:GW + (hd[c] + 1) * HD] = h_c.astype(dst.dtype)
            for c in cs:
                c_ref[c] = jnp.exp(m_old[c] - mu_last[c]) * c_ref[c] + sv[c][L:]
                m_ref[c:c + 1, :] = jnp.broadcast_to(a_all[last[c]:last[c] + 1, c:c + 1] + mu_last[c], (1, LANES))

        retention()
        for g in range(0, 2 * HEADS, MLSTM_GROUP):
            mlstm_group(range(g, g + MLSTM_GROUP))
        return carry

    lax.fori_loop(0, n_sub, chunk, 0)


def _mix_out_pieces(rows, tin, tiles_per_seq, gn_ref, wo_ref, nfw_ref, emit):
    of_ref, ofn_ref, ofp_ref, ob_ref, obn_ref, obp_ref, og_ref, ogn_ref, ogp_ref, x_ref, xn_ref, xp_ref = rows
    tm = x_ref.shape[0]
    n_ext = tm + 2 * SUBLANES
    v = {}

    def ext(m, nx, pv):
        return jnp.concatenate([m[...].astype(F32), nx[...].astype(F32)[:SUBLANES],
                                pv[...].astype(F32)[pv.shape[0] - SUBLANES:]], axis=0)

    def load():
        v["y"] = ext(of_ref, ofn_ref, ofp_ref) + ext(ob_ref, obn_ref, obp_ref)
        v["z"] = ext(og_ref, ogn_ref, ogp_ref)
        v["h"] = ext(x_ref, xn_ref, xp_ref)
        v["parts"] = []

    def head(g):
        sl = slice(g * HD, (g + 1) * HD)
        yg = v["y"][:, sl]
        d = yg - jnp.mean(yg, axis=-1, keepdims=True)
        n = d * lax.rsqrt(jnp.mean(d * d, axis=-1, keepdims=True) + HN_EPS) * gn_ref[:, sl]
        zg = v["z"][:, sl]
        gate = zg * _sigmoid(zg) if g < HEADS else _sigmoid(zg)
        v["parts"].append((n * gate).astype(BF16))
        if g % 2 == 1:
            pair = jnp.concatenate(v["parts"][-2:], axis=1)
            v["h"] = v["h"] + _dot(pair, wo_ref[(g - 1) * HD:(g + 1) * HD, :])

    def finish():
        r = lax.broadcasted_iota(jnp.int32, (n_ext, 1), 0)
        pad = ((r >= tm) & (r < tm + SUBLANES) & (tin == tiles_per_seq - 1)) | ((r >= tm + SUBLANES) & (tin == 0))
        emit(jnp.where(pad, 0.0, _rms(v["h"], nfw_ref[...])).astype(BF16), v["h"][:tm])

    return [load] + [functools.partial(head, g) for g in range(2 * HEADS)] + [finish]


def _out_ffn_kernel(*refs, tiles_per_seq):
    rows = refs[:12]
    (p_ref, gn_ref, wo_ref, nfw_ref, wg_ref, wu_ref, cw_ref, cb_ref, wd_ref, pnw_ref, pwg_ref, pgb_ref, pwp_ref, fnw_ref,
     y_ref, act_ref) = refs[12:]
    tm = p_ref.shape[0]
    stage = {}
    tin = pl.program_id(0) % tiles_per_seq
    for piece in _mix_out_pieces(rows, tin, tiles_per_seq, gn_ref, wo_ref, nfw_ref,
                                 lambda f_ext, h: stage.update(f_ext=f_ext, h=h)):
        piece()
    f_ext = stage["f_ext"]
    f = f_ext[:tm]

    fb = FF_BLOCK
    nblk = wg_ref.shape[1] // fb
    blk = lambda ref, j: ref[:, j * fb:(j + 1) * fb]
    proj = lambda j: (_dot(f_ext, blk(wg_ref, j)), _dot(f, blk(wu_ref, j)))
    nxt = proj(0)
    for j in range(nblk):
        u_ext, up = nxt
        if j + 1 < nblk:
            nxt = proj(j + 1)
        act_ref[:, j * fb:(j + 1) * fb] = (_gelu(_conv3(u_ext, tm, blk(cw_ref, j), blk(cb_ref, j))) * up).astype(BF16)
    h = stage["h"] + _dot(act_ref[...], wd_ref[...])
    gate = _sigmoid(_dot(_rms(h, pnw_ref[...]).astype(BF16), pwg_ref[...]) + pgb_ref[...])
    h = h + _dot(p_ref[...].astype(BF16), pwp_ref[...]) * gate
    y_ref[...] = _rms(h, fnw_ref[...])


def _const_spec(shape):
    nd = len(shape)
    return pl.BlockSpec(shape, lambda *_: (0,) * nd, pipeline_mode=pl.Buffered(1))


def _row_specs(tm, width, n_rows, halo=SUBLANES):
    per = tm // halo
    last = n_rows // halo - 1
    return [
        pl.BlockSpec((tm, width), lambda i: (i, 0)),
        pl.BlockSpec((halo, width), lambda i: (jnp.minimum((i + 1) * per, last), 0)),
        pl.BlockSpec((halo, width), lambda i: (jnp.maximum(i * per - 1, 0), 0)),
    ]


def _params(sem):
    return pltpu.CompilerParams(dimension_semantics=sem, vmem_limit_bytes=VMEM_LIMIT)


def _layer(x, p, w):
    bsz, seq, d = x.shape
    n = bsz * seq
    tm = ROW_TILE
    assert seq % tm == 0 and seq % CHUNK == 0
    tiles_per_seq = seq // tm
    x2 = x.reshape(n, d)
    row = lambda width: pl.BlockSpec((tm, width), lambda i: (i, 0))

    rows_per_chunk = SUBLANES
    ti = IN_TILE
    assert seq % ti == 0
    row_i = lambda width: pl.BlockSpec((ti, width), lambda i: (i, 0))
    pos_tile = pl.BlockSpec((ti, HD), lambda i: (i % (seq // ti), 0))
    qv, kt, og, acum, cmax, brow = pl.pallas_call(
        functools.partial(_in_proj_kernel, tiles_per_seq=seq // ti),
        grid=(n // ti,),
        in_specs=_row_specs(ti, d, n) + [pos_tile, pos_tile] + [_const_spec(a.shape) for a in w["in_proj"]],
        out_specs=[row_i(4 * GW), pl.BlockSpec((ti // CHUNK, 2 * GW, CHUNK), lambda i: (i, 0, 0)), row_i(2 * GW),
                   row_i(LANES), row_i(LANES), pl.BlockSpec((ti // CHUNK * rows_per_chunk, LANES), lambda i: (i, 0))],
        out_shape=[jax.ShapeDtypeStruct((n, 4 * GW), BF16), jax.ShapeDtypeStruct((n // CHUNK, 2 * GW, CHUNK), BF16),
                   jax.ShapeDtypeStruct((n, 2 * GW), BF16), jax.ShapeDtypeStruct((n, LANES), F32),
                   jax.ShapeDtypeStruct((n, LANES), F32),
                   jax.ShapeDtypeStruct((n // CHUNK * rows_per_chunk, LANES), F32)],
        compiler_params=_params(("parallel",)),
        name="in_proj",
    )(x2, x2, x2, *w["rope"], *w["in_proj"])

    cb = MIX_CHUNKS
    assert seq % (cb * CHUNK) == 0
    nb = seq // (cb * CHUNK)
    fwd_i = lambda b, t: b * nb + t
    bwd_i = lambda b, t: b * nb + nb - 1 - t

    def block_specs(idx):
        return [pl.BlockSpec((cb * CHUNK, 4 * GW), lambda b, t: (idx(b, t), 0)),
                pl.BlockSpec((cb, 2 * GW, CHUNK), lambda b, t: (idx(b, t), 0, 0)),
                pl.BlockSpec((cb * CHUNK, LANES), lambda b, t: (idx(b, t), 0)),
                pl.BlockSpec((cb * CHUNK, LANES), lambda b, t: (idx(b, t), 0)),
                pl.BlockSpec((cb * rows_per_chunk, LANES), lambda b, t: (idx(b, t), 0))]

    out_f, out_b = pl.pallas_call(
        _mixer_kernel,
        grid=(bsz, nb),
        in_specs=[_const_spec(w["ret_logit"].shape)] + block_specs(fwd_i) + block_specs(bwd_i),
        out_specs=[pl.BlockSpec((cb * CHUNK, 2 * GW), lambda b, t: (fwd_i(b, t), 0)),
                   pl.BlockSpec((cb * CHUNK, 2 * GW), lambda b, t: (bwd_i(b, t), 0))],
        out_shape=[jax.ShapeDtypeStruct((n, 2 * GW), F32)] * 2,
        scratch_shapes=[
            pltpu.VMEM((HEADS, CHUNK, CHUNK), F32),
            pltpu.VMEM((CHUNK, GW), F32), pltpu.VMEM((CHUNK, GW), F32),
            pltpu.VMEM((2 * HEADS, CHUNK), F32),
            pltpu.VMEM((2, GW), F32),
            pltpu.VMEM((HEADS, HD, HD), F32), pltpu.VMEM((HEADS, HD, HD), F32),
            pltpu.VMEM((2 * HEADS, HD, 2 * HD), F32),
            pltpu.VMEM((2 * HEADS, LANES), F32),
        ],
        compiler_params=_params(("arbitrary", "arbitrary")),
        name="mixers",
    )(w["ret_logit"], *([qv, kt, acum, cmax, brow] * 2))

    p2 = p.reshape(n, p.shape[-1])
    bf16_rows = 2 * SUBLANES
    mix_rows = _row_specs(tm, d, n) * 2 + _row_specs(tm, d, n, bf16_rows) + _row_specs(tm, d, n)
    y = pl.pallas_call(
        functools.partial(_out_ffn_kernel, tiles_per_seq=tiles_per_seq),
        grid=(n // tm,),
        in_specs=mix_rows + [row(p2.shape[-1])] + [_const_spec(a.shape) for a in (*w["mix_out"], *w["ffn"])],
        out_specs=row(d),
        out_shape=jax.ShapeDtypeStruct((n, d), F32),
        scratch_shapes=[pltpu.VMEM((tm, w["d_ff"]), BF16)],
        compiler_params=_params(("parallel",)),
        name="out_ffn",
    )(*((out_f,) * 3 + (out_b,) * 3 + (og,) * 3 + (x2,) * 3), p2, *w["mix_out"], *w["ffn"])
    return y.reshape(bsz, seq, d)


def kernel(x_prompt, x_sample, p_prompt, p_sample, norm_mix_w, w_in, mlstm_conv_w, mlstm_conv_b, mlstm_gate_b, ret_decay_logit, ret_gn_w, mlstm_gn_w, w_out, norm_ffn_w, ffn_w_gate, ffn_w_up, ffn_conv_w, ffn_conv_b, ffn_w_down, ple_w_proj, ple_norm_w, ple_w_gate, ple_gate_b, final_norm_w):
    depth = w_in.shape[0]
    d = x_prompt.shape[-1]
    d_ff = ffn_w_gate.shape[-1]
    assert d_ff % FF_BLOCK == 0
    rowv = lambda a: a.reshape(1, -1).astype(F32)
    inv = ROPE_BASE ** (-jnp.arange(0, HD, 2, dtype=F32) / HD)
    inv = jnp.concatenate([inv, inv]).reshape(1, HD)
    max_seq = max(x_prompt.shape[1], x_sample.shape[1])
    assert max_seq % ROW_TILE == 0
    rope = pl.pallas_call(
        _rope_table_kernel,
        grid=(max_seq // ROW_TILE,),
        in_specs=[_const_spec(inv.shape)],
        out_specs=[pl.BlockSpec((ROW_TILE, HD), lambda i: (i, 0))] * 2,
        out_shape=[jax.ShapeDtypeStruct((max_seq, HD), F32)] * 2,
        compiler_params=_params(("parallel",)),
        name="rope_tables",
    )(inv)

    assert depth == 1
    outs = [x_prompt, x_sample]
    for l in range(depth):
        wi = w_in[l].astype(BF16)
        n_gate = 2 * HEADS
        wgate = jnp.zeros((d, 2 * LANES), BF16)
        wgate = wgate.at[:, 0:n_gate].set(wi[:, 8 * GW:8 * GW + n_gate])
        wgate = wgate.at[:, LANES:LANES + n_gate].set(wi[:, 8 * GW + n_gate:8 * GW + 2 * n_gate])
        gb = mlstm_gate_b[l].astype(F32).reshape(-1)
        gbias = jnp.zeros((1, 2 * LANES), F32)
        gbias = gbias.at[0, 0:n_gate].set(gb[:n_gate]).at[0, LANES:LANES + n_gate].set(gb[n_gate:])
        w = {
            "rope": rope,
            "in_proj": [rowv(norm_mix_w[l]), wi[:, 0:4 * GW], wi[:, 4 * GW:6 * GW], wi[:, 6 * GW:8 * GW], wgate,
                        gbias, mlstm_conv_w[l].astype(F32), rowv(mlstm_conv_b[l])],
            "ret_logit": jnp.repeat(ret_decay_logit[l].astype(F32), HD, axis=1),
            "d_ff": d_ff,
            "mix_out": [jnp.concatenate([rowv(ret_gn_w[l]), rowv(mlstm_gn_w[l])], axis=1), w_out[l].astype(BF16),
                        rowv(norm_ffn_w[l])],
            "ffn": [ffn_w_gate[l].astype(BF16), ffn_w_up[l].astype(BF16), ffn_conv_w[l].astype(F32),
                    rowv(ffn_conv_b[l]), ffn_w_down[l].astype(BF16), rowv(ple_norm_w[l]), ple_w_gate[l].astype(BF16),
                    rowv(ple_gate_b[l]), ple_w_proj[l].astype(BF16), rowv(final_norm_w)],
        }
        outs = [_layer(h, p[l], w) for h, p in zip(outs, (p_prompt, p_sample))]
    return tuple(outs)
```

```python
import functools

import jax
import jax.numpy as jnp
from jax import lax
from jax.experimental import pallas as pl
from jax.experimental.pallas import tpu as pltpu

F32 = jnp.float32
BF16 = jnp.bfloat16

HEADS = 4
HD = 128
GW = HEADS * HD
CHUNK = 128
ROPE_BASE = 10000.0
RMS_EPS = 1e-6
HN_EPS = 1e-5
SUBLANES = 8
LANES = 128
ROW_TILE = 512
IN_TILE = 1024
FF_BLOCK = 256
MIX_CHUNKS = 8
MLSTM_GROUP = 4
VMEM_LIMIT = 56 * 1024 * 1024


def _rms(x, w):
    return x * lax.rsqrt(jnp.mean(x * x, axis=-1, keepdims=True) + RMS_EPS) * w


def _log_sigmoid(x):
    return jnp.minimum(x, 0.0) - jnp.log1p(jnp.exp(-jnp.abs(x)))


def _sigmoid(x):
    return 1.0 / (1.0 + jnp.exp(-x))


def _gelu(x):
    return 0.5 * x * (1.0 + lax.erf(x * (0.5 ** 0.5)))


def _dot(a, b):
    return jnp.dot(a, b, preferred_element_type=F32)


def _dot_nt(a, b):
    return lax.dot_general(a, b, (((1,), (1,)), ((), ())), preferred_element_type=F32)


def _ext_rows(main, nxt, prv, tin, tiles_per_seq):
    nxt = jnp.where(tin == tiles_per_seq - 1, 0.0, nxt)
    prv = jnp.where(tin == 0, 0.0, prv)
    return jnp.concatenate([main, nxt, prv], axis=0)


def _conv3(ext, rows, w, b):
    n = ext.shape[0]
    up = pltpu.roll(ext, 1, axis=0)[:rows]
    dn = pltpu.roll(ext, n - 1, axis=0)[:rows]
    return up * w[0:1] + ext[:rows] * w[1:2] + dn * w[2:3] + b


def _chunk_stats(ig, lf):
    tm = ig.shape[0]
    ii = lax.broadcasted_iota(jnp.int32, (CHUNK, CHUNK), 0)
    jj = lax.broadcasted_iota(jnp.int32, (CHUNK, CHUNK), 1)
    tril = (ii >= jj).astype(F32)
    triu = (jj >= ii).astype(F32)
    hi = lax.Precision.HIGHEST
    a = jnp.concatenate([
        jnp.where(jj < HEADS,
                  jnp.dot(tril, lf[r:r + CHUNK], precision=hi, preferred_element_type=F32),
                  jnp.dot(triu, lf[r:r + CHUNK], precision=hi, preferred_element_type=F32))
        for r in range(0, tm, CHUNK)], axis=0)
    b = ig - a
    r = lax.broadcasted_iota(jnp.int32, (tm, LANES), 0) % CHUNK
    pre = suf = b
    s = 1
    while s < CHUNK:
        pre = jnp.maximum(pre, jnp.where(r >= s, pltpu.roll(pre, s, axis=0), -jnp.inf))
        suf = jnp.maximum(suf, jnp.where(r < CHUNK - s, pltpu.roll(suf, tm - s, axis=0), -jnp.inf))
        s *= 2
    cm = jnp.where(lax.broadcasted_iota(jnp.int32, (tm, LANES), 1) < HEADS, pre, suf)
    return a, b, cm


def _rope_table_kernel(inv_ref, cos_ref, sin_ref):
    tm = cos_ref.shape[0]
    pos = (pl.program_id(0) * tm + lax.broadcasted_iota(jnp.int32, (tm, HD), 0)).astype(F32)
    ang = pos * inv_ref[...]
    cos_ref[...] = jnp.cos(ang)
    sin = jnp.sin(ang)
    lane = lax.broadcasted_iota(jnp.int32, (tm, HD), 1)
    sin_ref[...] = jnp.where(lane < HD // 2, -sin, sin)


def _in_proj_kernel(x_ref, xn_ref, xp_ref, cos_ref, sin_ref, nw_ref, wr_ref, wqk_ref, wm_ref, wg_ref, gb_ref, cw_ref,
                    cb_ref, qv_ref, kt_ref, og_ref, st_ref, br_ref, *, tiles_per_seq):
    tm = x_ref.shape[0]
    tin = pl.program_id(0) % tiles_per_seq
    nw = nw_ref[...]
    a_ext = _ext_rows(_rms(x_ref[...], nw), _rms(xn_ref[...], nw), _rms(xp_ref[...], nw), tin,
                      tiles_per_seq).astype(BF16)
    a = a_ext[:tm]
    scale = HD ** -0.5

    PW = 2 * HD

    def rope(t, i):
        th = t[:, i * HD:(i + 1) * HD]
        return th * cos_ref[...] + pltpu.roll(th, HD // 2, axis=1) * sin_ref[...]

    def put_rq(t, lo):
        for i in range(PW // HD):
            qv_ref[:, lo + i * HD:lo + (i + 1) * HD] = (rope(t, i) * scale).astype(BF16)

    def put_keys(row_lo, k):
        kt = k.T.astype(BF16)
        for ch in range(tm // CHUNK):
            kt_ref[ch, row_lo:row_lo + HD, :] = kt[:, ch * CHUNK:(ch + 1) * CHUNK]

    def put_rk(t, lo):
        for i in range(PW // HD):
            put_keys(lo + i * HD, rope(t, i))

    def conv_silu(t, lo):
        c = _conv3(t, tm, cw_ref[:, lo:lo + PW], cb_ref[:, lo:lo + PW])
        return c * _sigmoid(c)

    def put_mq(t, lo):
        qv_ref[:, 2 * GW + lo:2 * GW + lo + PW] = (conv_silu(t, lo) * scale).astype(BF16)

    def put_mk(t, lo):
        c = conv_silu(t, GW + lo)
        for i in range(PW // HD):
            put_keys(GW + lo + i * HD, c[:, i * HD:(i + 1) * HD])

    def put_gates(t, lo):
        g = t + gb_ref[...]
        acum, b, cmax = _chunk_stats(g[:, :LANES], _log_sigmoid(g[:, LANES:]))
        lane = lax.broadcasted_iota(jnp.int32, acum.shape, 1)
        st_ref[...] = jnp.where(lane < 2 * HEADS, acum, pltpu.roll(cmax, 2 * HEADS, axis=1))
        for ch in range(tm // CHUNK):
            br_ref[ch * SUBLANES:(ch + 1) * SUBLANES, :] = b[ch * CHUNK:(ch + 1) * CHUNK].T[:SUBLANES]

    def put(ref, base):
        def store(t, lo):
            ref[:, base + lo:base + lo + PW] = t.astype(BF16)
        return store

    def pieces(lhs, w_ref, w_lo, consume):
        return [(lhs, w_ref, w_lo + lo, consume, lo) for lo in range(0, GW, PW)]

    rq, rk = pieces(a, wr_ref, 0, put_rq), pieces(a, wr_ref, GW, put_rk)
    mq, mk = pieces(a_ext, wqk_ref, 0, put_mq), pieces(a_ext, wqk_ref, GW, put_mk)
    rv, rg = pieces(a, wr_ref, 2 * GW, put(qv_ref, GW)), pieces(a, wr_ref, 3 * GW, put(og_ref, 0))
    mv, mo = pieces(a, wm_ref, 0, put(qv_ref, 3 * GW)), pieces(a, wm_ref, GW, put(og_ref, GW))
    gates = [(a, wg_ref, 0, put_gates, 0)]
    order = [(gates[0],), (mq[0], rv[0]), (rq[0], rv[1]), (mq[1], rg[0]), (rq[1], rg[1]), (mk[0], mv[0]),
             (rk[0], mv[1]), (mk[1], mo[0]), (rk[1], mo[1])]
    project = lambda stage: [_dot(piece[0], piece[1][:, piece[2]:piece[2] + PW]) for piece in stage]
    pending = project(order[0])
    for k, stage in enumerate(order):
        ready = pending
        if k + 1 < len(order):
            pending = project(order[k + 1])
        for piece, t in zip(stage, ready):
            piece[3](t, piece[4])


def _mixer_kernel(lg_ref, qf_ref, ktf_ref, stf_ref, brf_ref, qb_ref, ktb_ref, stb_ref, brb_ref,
                  of_ref, ob_ref, dm_ref, qdf_ref, qdb_ref, kd_ref, cd_ref, sf_ref, sb_ref, c_ref, m_ref):
    L = CHUNK
    ii = lax.broadcasted_iota(jnp.int32, (L, L), 0)
    jj = lax.broadcasted_iota(jnp.int32, (L, L), 1)

    @pl.when(pl.program_id(1) == 0)
    def _init():
        lg = _log_sigmoid(lg_ref[...])
        lgf, lgb = lg[0:1], lg[1:2]
        row = lax.broadcasted_iota(jnp.int32, (L, GW), 0).astype(F32)
        qdf_ref[...] = jnp.exp((row + 1.0) * lgf)
        qdb_ref[...] = jnp.exp((L - row) * lgb)
        cd_ref[...] = jnp.exp(float(L) * lg)
        diff = (ii - jj).astype(F32)
        col = lax.broadcasted_iota(jnp.int32, (1, L), 1).astype(F32)
        for h in range(HEADS):
            sl = slice(h * HD, (h + 1) * HD)
            dm_ref[h] = jnp.where(diff >= 0, jnp.exp(diff * lgf[:, sl]), jnp.exp(-diff * lgb[:, sl]))
            kd_ref[h:h + 1, :] = jnp.exp((L - 1.0 - col) * lgf[:, sl])
            kd_ref[HEADS + h:HEADS + h + 1, :] = jnp.exp(col * lgb[:, sl])
        sf_ref[...] = jnp.zeros_like(sf_ref)
        sb_ref[...] = jnp.zeros_like(sb_ref)
        c_ref[...] = jnp.zeros_like(c_ref)
        m_ref[...] = jnp.zeros_like(m_ref)

    n_sub = qf_ref.shape[0] // L
    H = range(HEADS)
    C = range(2 * HEADS)
    sls = [slice(h * HD, (h + 1) * HD) for h in H]
    tril = ii >= jj
    triu = jj >= ii
    is_f = jj % (2 * HEADS) < HEADS
    ones_blk = jnp.ones((L, HD), BF16)

    def chunk(i, carry):
        i_b = n_sub - 1 - i
        rows_f = pl.ds(pl.multiple_of(i * L, L), L)
        rows_b = pl.ds(pl.multiple_of(i_b * L, L), L)
        part = lambda ref, rows, g, h: ref[rows, g * GW + h * HD:g * GW + (h + 1) * HD]

        def retention():
            qf, vf = (lambda h: part(qf_ref, rows_f, 0, h)), (lambda h: part(qf_ref, rows_f, 1, h))
            qb, vb = (lambda h: part(qb_ref, rows_b, 0, h)), (lambda h: part(qb_ref, rows_b, 1, h))
            ktf = lambda h: ktf_ref[i, sls[h], :]
            ktb = lambda h: ktb_ref[i_b, sls[h], :]
            qkc = [_dot(qf(h), jnp.concatenate([ktf(h), sf_ref[h].astype(BF16)], axis=1)) for h in H]
            p = [(qkc[h][:, :L] * dm_ref[h]).astype(BF16) for h in H]
            cross_b = [_dot(qb(h), sb_ref[h].astype(BF16)) for h in H]
            kdf = [(ktf(h).astype(F32) * kd_ref[h:h + 1, :]).astype(BF16) for h in H]
            pv = [_dot(jnp.concatenate([p[h], kdf[h]], axis=0), vf(h)) for h in H]
            for h in H:
                of_ref[rows_f, sls[h]] = (pv[h][:L] + qkc[h][:, L:] * qdf_ref[:, sls[h]]).astype(of_ref.dtype)
                ob_ref[rows_b, sls[h]] = (cross_b[h] * qdb_ref[:, sls[h]]).astype(ob_ref.dtype)
            upd_b = [_dot((ktb(h).astype(F32) * kd_ref[HEADS + h:HEADS + h + 1, :]).astype(BF16), vb(h)) for h in H]
            for h in H:
                sf_ref[h] = cd_ref[0:1, sls[h]] * sf_ref[h] + pv[h][L:]
                sb_ref[h] = cd_ref[1:2, sls[h]] * sb_ref[h] + upd_b[h]

        st = jnp.where(is_f, stf_ref[rows_f, :], stb_ref[rows_b, :])
        a_col = lambda c: st[:, c:c + 1]
        cm_col = lambda c: st[:, 2 * HEADS + c:2 * HEADS + c + 1]
        b_rows = jnp.where(lax.broadcasted_iota(jnp.int32, (SUBLANES, L), 0) < HEADS,
                           brf_ref[pl.ds(pl.multiple_of(i * SUBLANES, SUBLANES), SUBLANES), :],
                           brb_ref[pl.ds(pl.multiple_of(i_b * SUBLANES, SUBLANES), SUBLANES), :])

        def mlstm_group(cs):
            back = {c: c >= HEADS for c in cs}
            hd = {c: c % HEADS for c in cs}
            last = {c: 0 if back[c] else L - 1 for c in cs}
            q = lambda c: part(qb_ref, rows_b, 2, hd[c]) if back[c] else part(qf_ref, rows_f, 2, hd[c])
            v = lambda c: part(qb_ref, rows_b, 3, hd[c]) if back[c] else part(qf_ref, rows_f, 3, hd[c])
            kt = lambda c: (ktb_ref[i_b, GW + hd[c] * HD:GW + (hd[c] + 1) * HD, :] if back[c] else
                            ktf_ref[i, GW + hd[c] * HD:GW + (hd[c] + 1) * HD, :])
            v1 = lambda c: jnp.concatenate([v(c), ones_blk], axis=1)
            m_old = {c: m_ref[c:c + 1, 0:1] for c in cs}
            mu = {c: jnp.maximum(jnp.broadcast_to(cm_col(c), (L, L)), m_old[c]) for c in cs}
            e = {c: jnp.exp(jnp.where(triu if back[c] else tril, b_rows[c:c + 1, :] - mu[c], -jnp.inf)) for c in cs}
            qk = {c: _dot(q(c), kt(c)) for c in cs}
            s = {c: (qk[c] * e[c]).astype(BF16) for c in cs}
            qc = {c: _dot(q(c), c_ref[c].astype(BF16)) for c in cs}
            mu_last = {c: mu[c][last[c]:last[c] + 1, 0:1] for c in cs}
            kw = {c: (kt(c).astype(F32) * jnp.exp(b_rows[c:c + 1, :] - mu_last[c])).astype(BF16) for c in cs}
            sv = {c: _dot(jnp.concatenate([s[c], kw[c]], axis=0), v1(c)) for c in cs}
            for c in cs:
                w = jnp.exp(m_old[c] - mu[c])
                floor = jnp.exp(-(jnp.broadcast_to(a_col(c), (L, L)) + mu[c]))
                den = jnp.maximum(jnp.abs(w * qc[c][:, HD:] + sv[c][:L, HD:]), floor)
                dst, rows = (ob_ref, rows_b) if back[c] else (of_ref, rows_f)
                h_c = (w * qc[c][:, :HD] + sv[c][:L, :HD]) / den
                dst[rows, GW + hd[c] * HD:GW + (hd[c] + 1) * HD] = h_c.astype(dst.dtype)
            for c in cs:
                c_ref[c] = jnp.exp(m_old[c] - mu_last[c]) * c_ref[c] + sv[c][L:]
                m_ref[c:c + 1, :] = jnp.broadcast_to(a_col(c)[last[c]:last[c] + 1] + mu_last[c], (1, LANES))

        retention()
        for g in range(0, 2 * HEADS, MLSTM_GROUP):
            mlstm_group(range(g, g + MLSTM_GROUP))
        return carry

    lax.fori_loop(0, n_sub, chunk, 0)


def _mix_out_pieces(rows, tin, tiles_per_seq, gn_ref, wo_ref, nfw_ref, emit):
    of_ref, ofn_ref, ofp_ref, ob_ref, obn_ref, obp_ref, og_ref, ogn_ref, ogp_ref, x_ref, xn_ref, xp_ref = rows
    tm = x_ref.shape[0]
    n_ext = tm + 2 * SUBLANES
    v = {}

    def ext(m, nx, pv):
        return jnp.concatenate([m[...].astype(F32), nx[...].astype(F32)[:SUBLANES],
                                pv[...].astype(F32)[pv.shape[0] - SUBLANES:]], axis=0)

    def load():
        v["y"] = ext(of_ref, ofn_ref, ofp_ref) + ext(ob_ref, obn_ref, obp_ref)
        v["z"] = ext(og_ref, ogn_ref, ogp_ref)
        v["h"] = ext(x_ref, xn_ref, xp_ref)
        v["parts"] = []

    def head(g):
        sl = slice(g * HD, (g + 1) * HD)
        yg = v["y"][:, sl]
        d = yg - jnp.mean(yg, axis=-1, keepdims=True)
        n = d * lax.rsqrt(jnp.mean(d * d, axis=-1, keepdims=True) + HN_EPS) * gn_ref[:, sl]
        zg = v["z"][:, sl]
        gate = zg * _sigmoid(zg) if g < HEADS else _sigmoid(zg)
        v["parts"].append((n * gate).astype(BF16))
        if g % 2 == 1:
            pair = jnp.concatenate(v["parts"][-2:], axis=1)
            v["h"] = v["h"] + _dot(pair, wo_ref[(g - 1) * HD:(g + 1) * HD, :])

    def finish():
        r = lax.broadcasted_iota(jnp.int32, (n_ext, 1), 0)
        pad = ((r >= tm) & (r < tm + SUBLANES) & (tin == tiles_per_seq - 1)) | ((r >= tm + SUBLANES) & (tin == 0))
        emit(jnp.where(pad, 0.0, _rms(v["h"], nfw_ref[...])).astype(BF16), v["h"][:tm])

    return [load] + [functools.partial(head, g) for g in range(2 * HEADS)] + [finish]


def _out_ffn_kernel(*refs, tiles_per_seq):
    rows = refs[:12]
    (p_ref, gn_ref, wo_ref, nfw_ref, wg_ref, wu_ref, cw_ref, cb_ref, wd_ref, pnw_ref, pwg_ref, pgb_ref, pwp_ref, fnw_ref,
     y_ref, act_ref) = refs[12:]
    tm = p_ref.shape[0]
    stage = {}
    tin = pl.program_id(0) % tiles_per_seq
    for piece in _mix_out_pieces(rows, tin, tiles_per_seq, gn_ref, wo_ref, nfw_ref,
                                 lambda f_ext, h: stage.update(f_ext=f_ext, h=h)):
        piece()
    f_ext = stage["f_ext"]
    f = f_ext[:tm]

    fb = FF_BLOCK
    nblk = wg_ref.shape[1] // fb
    blk = lambda ref, j: ref[:, j * fb:(j + 1) * fb]
    proj = lambda j: (_dot(f_ext, blk(wg_ref, j)), _dot(f, blk(wu_ref, j)))
    nxt = proj(0)
    for j in range(nblk):
        u_ext, up = nxt
        if j + 1 < nblk:
            nxt = proj(j + 1)
        act_ref[:, j * fb:(j + 1) * fb] = (_gelu(_conv3(u_ext, tm, blk(cw_ref, j), blk(cb_ref, j))) * up).astype(BF16)
    h = stage["h"] + _dot(act_ref[...], wd_ref[...])
    gate = _sigmoid(_dot(_rms(h, pnw_ref[...]).astype(BF16), pwg_ref[...]) + pgb_ref[...])
    h = h + _dot(p_ref[...].astype(BF16), pwp_ref[...]) * gate
    y_ref[...] = _rms(h, fnw_ref[...])


def _const_spec(shape):
    nd = len(shape)
    return pl.BlockSpec(shape, lambda *_: (0,) * nd, pipeline_mode=pl.Buffered(1))


def _row_specs(tm, width, n_rows, halo=SUBLANES):
    per = tm // halo
    last = n_rows // halo - 1
    return [
        pl.BlockSpec((tm, width), lambda i: (i, 0)),
        pl.BlockSpec((halo, width), lambda i: (jnp.minimum((i + 1) * per, last), 0)),
        pl.BlockSpec((halo, width), lambda i: (jnp.maximum(i * per - 1, 0), 0)),
    ]


def _params(sem):
    return pltpu.CompilerParams(dimension_semantics=sem, vmem_limit_bytes=VMEM_LIMIT)


def _layer(x, p, w):
    bsz, seq, d = x.shape
    n = bsz * seq
    tm = ROW_TILE
    assert seq % tm == 0 and seq % CHUNK == 0
    tiles_per_seq = seq // tm
    x2 = x.reshape(n, d)
    row = lambda width: pl.BlockSpec((tm, width), lambda i: (i, 0))

    rows_per_chunk = SUBLANES
    ti = IN_TILE
    assert seq % ti == 0
    row_i = lambda width: pl.BlockSpec((ti, width), lambda i: (i, 0))
    pos_tile = pl.BlockSpec((ti, HD), lambda i: (i % (seq // ti), 0))
    qv, kt, og, stats, brow = pl.pallas_call(
        functools.partial(_in_proj_kernel, tiles_per_seq=seq // ti),
        grid=(n // ti,),
        in_specs=_row_specs(ti, d, n) + [pos_tile, pos_tile] + [_const_spec(a.shape) for a in w["in_proj"]],
        out_specs=[row_i(4 * GW), pl.BlockSpec((ti // CHUNK, 2 * GW, CHUNK), lambda i: (i, 0, 0)), row_i(2 * GW),
                   row_i(LANES), pl.BlockSpec((ti // CHUNK * rows_per_chunk, LANES), lambda i: (i, 0))],
        out_shape=[jax.ShapeDtypeStruct((n, 4 * GW), BF16), jax.ShapeDtypeStruct((n // CHUNK, 2 * GW, CHUNK), BF16),
                   jax.ShapeDtypeStruct((n, 2 * GW), BF16), jax.ShapeDtypeStruct((n, LANES), F32),
                   jax.ShapeDtypeStruct((n // CHUNK * rows_per_chunk, LANES), F32)],
        compiler_params=_params(("parallel",)),
        name="in_proj",
    )(x2, x2, x2, *w["rope"], *w["in_proj"])

    cb = MIX_CHUNKS
    assert seq % (cb * CHUNK) == 0
    nb = seq // (cb * CHUNK)
    fwd_i = lambda b, t: b * nb + t
    bwd_i = lambda b, t: b * nb + nb - 1 - t

    def block_specs(idx):
        return [pl.BlockSpec((cb * CHUNK, 4 * GW), lambda b, t: (idx(b, t), 0)),
                pl.BlockSpec((cb, 2 * GW, CHUNK), lambda b, t: (idx(b, t), 0, 0)),
                pl.BlockSpec((cb * CHUNK, LANES), lambda b, t: (idx(b, t), 0)),
                pl.BlockSpec((cb * rows_per_chunk, LANES), lambda b, t: (idx(b, t), 0))]

    out_f, out_b = pl.pallas_call(
        _mixer_kernel,
        grid=(bsz, nb),
        in_specs=[_const_spec(w["ret_logit"].shape)] + block_specs(fwd_i) + block_specs(bwd_i),
        out_specs=[pl.BlockSpec((cb * CHUNK, 2 * GW), lambda b, t: (fwd_i(b, t), 0)),
                   pl.BlockSpec((cb * CHUNK, 2 * GW), lambda b, t: (bwd_i(b, t), 0))],
        out_shape=[jax.ShapeDtypeStruct((n, 2 * GW), BF16)] * 2,
        scratch_shapes=[
            pltpu.VMEM((HEADS, CHUNK, CHUNK), F32),
            pltpu.VMEM((CHUNK, GW), F32), pltpu.VMEM((CHUNK, GW), F32),
            pltpu.VMEM((2 * HEADS, CHUNK), F32),
            pltpu.VMEM((2, GW), F32),
            pltpu.VMEM((HEADS, HD, HD), F32), pltpu.VMEM((HEADS, HD, HD), F32),
            pltpu.VMEM((2 * HEADS, HD, 2 * HD), F32),
            pltpu.VMEM((2 * HEADS, LANES), F32),
        ],
        compiler_params=_params(("arbitrary", "arbitrary")),
        name="mixers",
    )(w["ret_logit"], *([qv, kt, stats, brow] * 2))

    p2 = p.reshape(n, p.shape[-1])
    bf16_rows = 2 * SUBLANES
    mix_rows = _row_specs(tm, d, n, bf16_rows) * 3 + _row_specs(tm, d, n)
    y = pl.pallas_call(
        functools.partial(_out_ffn_kernel, tiles_per_seq=tiles_per_seq),
        grid=(n // tm,),
        in_specs=mix_rows + [row(p2.shape[-1])] + [_const_spec(a.shape) for a in (*w["mix_out"], *w["ffn"])],
        out_specs=row(d),
        out_shape=jax.ShapeDtypeStruct((n, d), F32),
        scratch_shapes=[pltpu.VMEM((tm, w["d_ff"]), BF16)],
        compiler_params=_params(("parallel",)),
        name="out_ffn",
    )(*((out_f,) * 3 + (out_b,) * 3 + (og,) * 3 + (x2,) * 3), p2, *w["mix_out"], *w["ffn"])
    return y.reshape(bsz, seq, d)


def kernel(x_prompt, x_sample, p_prompt, p_sample, norm_mix_w, w_in, mlstm_conv_w, mlstm_conv_b, mlstm_gate_b, ret_decay_logit, ret_gn_w, mlstm_gn_w, w_out, norm_ffn_w, ffn_w_gate, ffn_w_up, ffn_conv_w, ffn_conv_b, ffn_w_down, ple_w_proj, ple_norm_w, ple_w_gate, ple_gate_b, final_norm_w):
    depth = w_in.shape[0]
    d = x_prompt.shape[-1]
    d_ff = ffn_w_gate.shape[-1]
    assert d_ff % FF_BLOCK == 0
    rowv = lambda a: a.reshape(1, -1).astype(F32)
    inv = ROPE_BASE ** (-jnp.arange(0, HD, 2, dtype=F32) / HD)
    inv = jnp.concatenate([inv, inv]).reshape(1, HD)
    max_seq = max(x_prompt.shape[1], x_sample.shape[1])
    assert max_seq % ROW_TILE == 0
    rope = pl.pallas_call(
        _rope_table_kernel,
        grid=(max_seq // ROW_TILE,),
        in_specs=[_const_spec(inv.shape)],
        out_specs=[pl.BlockSpec((ROW_TILE, HD), lambda i: (i, 0))] * 2,
        out_shape=[jax.ShapeDtypeStruct((max_seq, HD), F32)] * 2,
        compiler_params=_params(("parallel",)),
        name="rope_tables",
    )(inv)

    assert depth == 1
    outs = [x_prompt, x_sample]
    for l in range(depth):
        wi = w_in[l].astype(BF16)
        n_gate = 2 * HEADS
        wgate = jnp.zeros((d, 2 * LANES), BF16)
        wgate = wgate.at[:, 0:n_gate].set(wi[:, 8 * GW:8 * GW + n_gate])
        wgate = wgate.at[:, LANES:LANES + n_gate].set(wi[:, 8 * GW + n_gate:8 * GW + 2 * n_gate])
        gb = mlstm_gate_b[l].astype(F32).reshape(-1)
        gbias = jnp.zeros((1, 2 * LANES), F32)
        gbias = gbias.at[0, 0:n_gate].set(gb[:n_gate]).at[0, LANES:LANES + n_gate].set(gb[n_gate:])
        w = {
            "rope": rope,
            "in_proj": [rowv(norm_mix_w[l]), wi[:, 0:4 * GW], wi[:, 4 * GW:6 * GW], wi[:, 6 * GW:8 * GW], wgate,
                        gbias, mlstm_conv_w[l].astype(F32), rowv(mlstm_conv_b[l])],
            "ret_logit": jnp.repeat(ret_decay_logit[l].astype(F32), HD, axis=1),
            "d_ff": d_ff,
            "mix_out": [jnp.concatenate([rowv(ret_gn_w[l]), rowv(mlstm_gn_w[l])], axis=1), w_out[l].astype(BF16),
                        rowv(norm_ffn_w[l])],
            "ffn": [ffn_w_gate[l].astype(BF16), ffn_w_up[l].astype(BF16), ffn_conv_w[l].astype(F32),
                    rowv(ffn_conv_b[l]), ffn_w_down[l].astype(BF16), rowv(ple_norm_w[l]), ple_w_gate[l].astype(BF16),
                    rowv(ple_gate_b[l]), ple_w_proj[l].astype(BF16), rowv(final_norm_w)],
        }
        outs = [_layer(h, p[l], w) for h, p in zip(outs, (p_prompt, p_sample))]
    return tuple(outs)
```

```python
import functools

import jax
import jax.numpy as jnp
from jax import lax
from jax.experimental import pallas as pl
from jax.experimental.pallas import tpu as pltpu

F32 = jnp.float32
BF16 = jnp.bfloat16

HEADS = 4
HD = 128
GW = HEADS * HD
CHUNK = 128
ROPE_BASE = 10000.0
RMS_EPS = 1e-6
HN_EPS = 1e-5
SUBLANES = 8
LANES = 128
ROW_TILE = 512
IN_TILE = 1024
FF_BLOCK = 256
MIX_CHUNKS = 8
MLSTM_GROUP = 4
VMEM_LIMIT = 56 * 1024 * 1024


def _rms(x, w):
    return x * lax.rsqrt(jnp.mean(x * x, axis=-1, keepdims=True) + RMS_EPS) * w


def _log_sigmoid(x):
    return jnp.minimum(x, 0.0) - jnp.log1p(jnp.exp(-jnp.abs(x)))


def _sigmoid(x):
    return 1.0 / (1.0 + jnp.exp(-x))


def _gelu(x):
    return 0.5 * x * (1.0 + lax.erf(x * (0.5 ** 0.5)))


def _dot(a, b):
    return jnp.dot(a, b, preferred_element_type=F32)


def _dot_nt(a, b):
    return lax.dot_general(a, b, (((1,), (1,)), ((), ())), preferred_element_type=F32)


def _ext_rows(main, nxt, prv, tin, tiles_per_seq):
    nxt = jnp.where(tin == tiles_per_seq - 1, 0.0, nxt)
    prv = jnp.where(tin == 0, 0.0, prv)
    return jnp.concatenate([main, nxt, prv], axis=0)


def _conv3(ext, rows, w, b):
    n = ext.shape[0]
    up = pltpu.roll(ext, 1, axis=0)[:rows]
    dn = pltpu.roll(ext, n - 1, axis=0)[:rows]
    return up * w[0:1] + ext[:rows] * w[1:2] + dn * w[2:3] + b


def _chunk_stats(ig, lf):
    tm = ig.shape[0]
    ii = lax.broadcasted_iota(jnp.int32, (CHUNK, CHUNK), 0)
    jj = lax.broadcasted_iota(jnp.int32, (CHUNK, CHUNK), 1)
    tril = (ii >= jj).astype(F32)
    triu = (jj >= ii).astype(F32)
    hi = lax.Precision.HIGHEST
    a = jnp.concatenate([
        jnp.where(jj < HEADS,
                  jnp.dot(tril, lf[r:r + CHUNK], precision=hi, preferred_element_type=F32),
                  jnp.dot(triu, lf[r:r + CHUNK], precision=hi, preferred_element_type=F32))
        for r in range(0, tm, CHUNK)], axis=0)
    b = ig - a
    r = lax.broadcasted_iota(jnp.int32, (tm, LANES), 0) % CHUNK
    pre = suf = b
    s = 1
    while s < CHUNK:
        pre = jnp.maximum(pre, jnp.where(r >= s, pltpu.roll(pre, s, axis=0), -jnp.inf))
        suf = jnp.maximum(suf, jnp.where(r < CHUNK - s, pltpu.roll(suf, tm - s, axis=0), -jnp.inf))
        s *= 2
    cm = jnp.where(lax.broadcasted_iota(jnp.int32, (tm, LANES), 1) < HEADS, pre, suf)
    return a, b, cm


def _rope_table_kernel(inv_ref, cos_ref, sin_ref):
    tm = cos_ref.shape[0]
    pos = (pl.program_id(0) * tm + lax.broadcasted_iota(jnp.int32, (tm, HD), 0)).astype(F32)
    ang = pos * inv_ref[...]
    cos_ref[...] = jnp.cos(ang)
    sin = jnp.sin(ang)
    lane = lax.broadcasted_iota(jnp.int32, (tm, HD), 1)
    sin_ref[...] = jnp.where(lane < HD // 2, -sin, sin)


def _in_proj_kernel(x_ref, xn_ref, xp_ref, cos_ref, sin_ref, nw_ref, w_ref, wg_ref, gb_ref, cw_ref, cb_ref,
                    qv_ref, kt_ref, og_ref, st_ref, br_ref, *, tiles_per_seq):
    tm = x_ref.shape[0]
    tin = pl.program_id(0) % tiles_per_seq
    x_ext = _ext_rows(x_ref[...], xn_ref[...], xp_ref[...], tin, tiles_per_seq)
    inv_rms = lax.rsqrt(jnp.mean(x_ext * x_ext, axis=-1, keepdims=True) + RMS_EPS)
    kb = 2 * LANES
    a_blocks, gate_proj = [], None
    for k in range(0, x_ext.shape[1], kb):
        a_k = (x_ext[:, k:k + kb] * inv_rms * nw_ref[:, k:k + kb]).astype(BF16)
        a_blocks.append(a_k)
        part = _dot(a_k[:tm], wg_ref[k:k + kb, :])
        gate_proj = part if gate_proj is None else gate_proj + part
    a_ext = jnp.concatenate(a_blocks, axis=1)
    a = a_ext[:tm]
    scale = HD ** -0.5

    PW = 2 * HD

    def rope(t, i):
        th = t[:, i * HD:(i + 1) * HD]
        return th * cos_ref[...] + pltpu.roll(th, HD // 2, axis=1) * sin_ref[...]

    def put_rq(t, lo):
        for i in range(PW // HD):
            qv_ref[:, lo + i * HD:lo + (i + 1) * HD] = (rope(t, i) * scale).astype(BF16)

    def put_keys(row_lo, k):
        kt = k.T.astype(BF16)
        for ch in range(tm // CHUNK):
            kt_ref[ch, row_lo:row_lo + HD, :] = kt[:, ch * CHUNK:(ch + 1) * CHUNK]

    def put_rk(t, lo):
        for i in range(PW // HD):
            put_keys(lo + i * HD, rope(t, i))

    def conv_silu(t, lo):
        c = _conv3(t, tm, cw_ref[:, lo:lo + PW], cb_ref[:, lo:lo + PW])
        return c * _sigmoid(c)

    def put_mq(t, lo):
        qv_ref[:, 2 * GW + lo:2 * GW + lo + PW] = (conv_silu(t, lo) * scale).astype(BF16)

    def put_mk(t, lo):
        c = conv_silu(t, GW + lo)
        for i in range(PW // HD):
            put_keys(GW + lo + i * HD, c[:, i * HD:(i + 1) * HD])

    def put_gates(t, lo):
        g = t + gb_ref[...]
        acum, b, cmax = _chunk_stats(g[:, :LANES], _log_sigmoid(g[:, LANES:]))
        lane = lax.broadcasted_iota(jnp.int32, acum.shape, 1)
        st_ref[...] = jnp.where(lane < 2 * HEADS, acum, pltpu.roll(cmax, 2 * HEADS, axis=1))
        for ch in range(tm // CHUNK):
            br_ref[ch * SUBLANES:(ch + 1) * SUBLANES, :] = b[ch * CHUNK:(ch + 1) * CHUNK].T[:SUBLANES]

    def put(ref, base):
        def store(t, lo):
            ref[:, base + lo:base + lo + PW] = t.astype(BF16)
        return store

    def pieces(lhs, w_ref, w_lo, consume):
        return [(lhs, w_ref, w_lo + lo, consume, lo) for lo in range(0, GW, PW)]

    rq, rk = pieces(a, w_ref, 0, put_rq), pieces(a, w_ref, GW, put_rk)
    rv, rg = pieces(a, w_ref, 2 * GW, put(qv_ref, GW)), pieces(a, w_ref, 3 * GW, put(og_ref, 0))
    mq, mk = pieces(a_ext, w_ref, 4 * GW, put_mq), pieces(a_ext, w_ref, 5 * GW, put_mk)
    mv, mo = pieces(a, w_ref, 6 * GW, put(qv_ref, 3 * GW)), pieces(a, w_ref, 7 * GW, put(og_ref, GW))
    gates = [(None, None, 0, put_gates, 0)]
    order = [(gates[0],), (mq[0], rv[0]), (rq[0], rv[1]), (mq[1], rg[0]), (rq[1], rg[1]), (mk[0], mv[0]),
             (rk[0], mv[1]), (mk[1], mo[0]), (rk[1], mo[1])]
    project = lambda stage: [_dot(piece[0], piece[1][:, piece[2]:piece[2] + PW]) for piece in stage]
    pending = [gate_proj]
    for k, stage in enumerate(order):
        ready = pending
        if k + 1 < len(order):
            pending = project(order[k + 1])
        for piece, t in zip(stage, ready):
            piece[3](t, piece[4])


def _mixer_kernel(lg_ref, qf_ref, ktf_ref, stf_ref, brf_ref, qb_ref, ktb_ref, stb_ref, brb_ref,
                  of_ref, ob_ref, dm_ref, qdf_ref, qdb_ref, kd_ref, cd_ref, sf_ref, sb_ref, c_ref, m_ref):
    L = CHUNK
    ii = lax.broadcasted_iota(jnp.int32, (L, L), 0)
    jj = lax.broadcasted_iota(jnp.int32, (L, L), 1)

    @pl.when(pl.program_id(1) == 0)
    def _init():
        lg = _log_sigmoid(lg_ref[...])
        lgf, lgb = lg[0:1], lg[1:2]
        row = lax.broadcasted_iota(jnp.int32, (L, GW), 0).astype(F32)
        qdf_ref[...] = jnp.exp((row + 1.0) * lgf)
        qdb_ref[...] = jnp.exp((L - row) * lgb)
        cd_ref[...] = jnp.exp(float(L) * lg)
        diff = (ii - jj).astype(F32)
        col = lax.broadcasted_iota(jnp.int32, (1, L), 1).astype(F32)
        for h in range(HEADS):
            sl = slice(h * HD, (h + 1) * HD)
            dm_ref[h] = jnp.where(diff >= 0, jnp.exp(diff * lgf[:, sl]), jnp.exp(-diff * lgb[:, sl]))
            kd_ref[h:h + 1, :] = jnp.exp((L - 1.0 - col) * lgf[:, sl])
            kd_ref[HEADS + h:HEADS + h + 1, :] = jnp.exp(col * lgb[:, sl])
        sf_ref[...] = jnp.zeros_like(sf_ref)
        sb_ref[...] = jnp.zeros_like(sb_ref)
        c_ref[...] = jnp.zeros_like(c_ref)
        m_ref[...] = jnp.zeros_like(m_ref)

    n_sub = qf_ref.shape[0] // L
    H = range(HEADS)
    C = range(2 * HEADS)
    sls = [slice(h * HD, (h + 1) * HD) for h in H]
    tril = ii >= jj
    triu = jj >= ii
    is_f = jj % (2 * HEADS) < HEADS
    ones_blk = jnp.ones((L, HD), BF16)

    def chunk(i, carry):
        i_b = n_sub - 1 - i
        rows_f = pl.ds(pl.multiple_of(i * L, L), L)
        rows_b = pl.ds(pl.multiple_of(i_b * L, L), L)
        part = lambda ref, rows, g, h: ref[rows, g * GW + h * HD:g * GW + (h + 1) * HD]

        def retention():
            qf, vf = (lambda h: part(qf_ref, rows_f, 0, h)), (lambda h: part(qf_ref, rows_f, 1, h))
            qb, vb = (lambda h: part(qb_ref, rows_b, 0, h)), (lambda h: part(qb_ref, rows_b, 1, h))
            ktf = lambda h: ktf_ref[i, sls[h], :]
            ktb = lambda h: ktb_ref[i_b, sls[h], :]
            qkc = [_dot(qf(h), jnp.concatenate([ktf(h), sf_ref[h].astype(BF16)], axis=1)) for h in H]
            p = [(qkc[h][:, :L] * dm_ref[h]).astype(BF16) for h in H]
            cross_b = [_dot(qb(h), sb_ref[h].astype(BF16)) for h in H]
            kdf = [(ktf(h).astype(F32) * kd_ref[h:h + 1, :]).astype(BF16) for h in H]
            pv = [_dot(jnp.concatenate([p[h], kdf[h]], axis=0), vf(h)) for h in H]
            for h in H:
                of_ref[rows_f, sls[h]] = (pv[h][:L] + qkc[h][:, L:] * qdf_ref[:, sls[h]]).astype(of_ref.dtype)
                ob_ref[rows_b, sls[h]] = (cross_b[h] * qdb_ref[:, sls[h]]).astype(ob_ref.dtype)
            upd_b = [_dot((ktb(h).astype(F32) * kd_ref[HEADS + h:HEADS + h + 1, :]).astype(BF16), vb(h)) for h in H]
            for h in H:
                sf_ref[h] = cd_ref[0:1, sls[h]] * sf_ref[h] + pv[h][L:]
                sb_ref[h] = cd_ref[1:2, sls[h]] * sb_ref[h] + upd_b[h]

        st = jnp.where(is_f, stf_ref[rows_f, :], stb_ref[rows_b, :])
        a_col = lambda c: st[:, c:c + 1]
        cm_col = lambda c: st[:, 2 * HEADS + c:2 * HEADS + c + 1]
        b_rows = jnp.where(lax.broadcasted_iota(jnp.int32, (SUBLANES, L), 0) < HEADS,
                           brf_ref[pl.ds(pl.multiple_of(i * SUBLANES, SUBLANES), SUBLANES), :],
                           brb_ref[pl.ds(pl.multiple_of(i_b * SUBLANES, SUBLANES), SUBLANES), :])

        def mlstm_group(cs):
            back = {c: c >= HEADS for c in cs}
            hd = {c: c % HEADS for c in cs}
            last = {c: 0 if back[c] else L - 1 for c in cs}
            q = lambda c: part(qb_ref, rows_b, 2, hd[c]) if back[c] else part(qf_ref, rows_f, 2, hd[c])
            v = lambda c: part(qb_ref, rows_b, 3, hd[c]) if back[c] else part(qf_ref, rows_f, 3, hd[c])
            kt = lambda c: (ktb_ref[i_b, GW + hd[c] * HD:GW + (hd[c] + 1) * HD, :] if back[c] else
                            ktf_ref[i, GW + hd[c] * HD:GW + (hd[c] + 1) * HD, :])
            v1 = lambda c: jnp.concatenate([v(c), ones_blk], axis=1)
            m_old = {c: m_ref[c:c + 1, 0:1] for c in cs}
            mu = {c: jnp.maximum(jnp.broadcast_to(cm_col(c), (L, L)), m_old[c]) for c in cs}
            e = {c: jnp.exp(jnp.where(triu if back[c] else tril, b_rows[c:c + 1, :] - mu[c], -jnp.inf)) for c in cs}
            qk = {c: _dot(q(c), kt(c)) for c in cs}
            s = {c: (qk[c] * e[c]).astype(BF16) for c in cs}
            qc = {c: _dot(q(c), c_ref[c].astype(BF16)) for c in cs}
            mu_last = {c: mu[c][last[c]:last[c] + 1, 0:1] for c in cs}
            kw = {c: (kt(c).astype(F32) * jnp.exp(b_rows[c:c + 1, :] - mu_last[c])).astype(BF16) for c in cs}
            sv = {c: _dot(jnp.concatenate([s[c], kw[c]], axis=0), v1(c)) for c in cs}
            for c in cs:
                w = jnp.exp(m_old[c] - mu[c])
                floor = jnp.exp(-(jnp.broadcast_to(a_col(c), (L, L)) + mu[c]))
                den = jnp.maximum(jnp.abs(w * qc[c][:, HD:] + sv[c][:L, HD:]), floor)
                dst, rows = (ob_ref, rows_b) if back[c] else (of_ref, rows_f)
                h_c = (w * qc[c][:, :HD] + sv[c][:L, :HD]) / den
                dst[rows, GW + hd[c] * HD:GW + (hd[c] + 1) * HD] = h_c.astype(dst.dtype)
            for c in cs:
                c_ref[c] = jnp.exp(m_old[c] - mu_last[c]) * c_ref[c] + sv[c][L:]
                m_ref[c:c + 1, :] = jnp.broadcast_to(a_col(c)[last[c]:last[c] + 1] + mu_last[c], (1, LANES))

        retention()
        for g in range(0, 2 * HEADS, MLSTM_GROUP):
            mlstm_group(range(g, g + MLSTM_GROUP))
        return carry

    lax.fori_loop(0, n_sub, chunk, 0)


def _mix_out_pieces(rows, tin, tiles_per_seq, gn_ref, wo_ref, nfw_ref, emit):
    of_ref, ofn_ref, ofp_ref, ob_ref, obn_ref, obp_ref, og_ref, ogn_ref, ogp_ref, x_ref, xn_ref, xp_ref = rows
    tm = x_ref.shape[0]
    n_ext = tm + 2 * SUBLANES
    v = {}

    def ext(m, nx, pv):
        return jnp.concatenate([m[...].astype(F32), nx[...].astype(F32)[:SUBLANES],
                                pv[...].astype(F32)[pv.shape[0] - SUBLANES:]], axis=0)

    def load():
        v["y"] = ext(of_ref, ofn_ref, ofp_ref) + ext(ob_ref, obn_ref, obp_ref)
        v["z"] = ext(og_ref, ogn_ref, ogp_ref)
        v["h"] = ext(x_ref, xn_ref, xp_ref)
        v["parts"] = []

    def head(g):
        sl = slice(g * HD, (g + 1) * HD)
        yg = v["y"][:, sl]
        d = yg - jnp.mean(yg, axis=-1, keepdims=True)
        n = d * lax.rsqrt(jnp.mean(d * d, axis=-1, keepdims=True) + HN_EPS) * gn_ref[:, sl]
        zg = v["z"][:, sl]
        gate = zg * _sigmoid(zg) if g < HEADS else _sigmoid(zg)
        v["parts"].append((n * gate).astype(BF16))
        if g % 2 == 1:
            pair = jnp.concatenate(v["parts"][-2:], axis=1)
            v["h"] = v["h"] + _dot(pair, wo_ref[(g - 1) * HD:(g + 1) * HD, :])

    def finish():
        r = lax.broadcasted_iota(jnp.int32, (n_ext, 1), 0)
        pad = ((r >= tm) & (r < tm + SUBLANES) & (tin == tiles_per_seq - 1)) | ((r >= tm + SUBLANES) & (tin == 0))
        emit(jnp.where(pad, 0.0, _rms(v["h"], nfw_ref[...])).astype(BF16), v["h"][:tm])

    return [load] + [functools.partial(head, g) for g in range(2 * HEADS)] + [finish]


def _out_ffn_kernel(*refs, tiles_per_seq):
    rows = refs[:12]
    (p_ref, gn_ref, wo_ref, nfw_ref, wg_ref, wu_ref, cw_ref, cb_ref, wd_ref, pnw_ref, pwg_ref, pgb_ref, pwp_ref, fnw_ref,
     y_ref, act_ref) = refs[12:]
    tm = p_ref.shape[0]
    stage = {}
    tin = pl.program_id(0) % tiles_per_seq
    for piece in _mix_out_pieces(rows, tin, tiles_per_seq, gn_ref, wo_ref, nfw_ref,
                                 lambda f_ext, h: stage.update(f_ext=f_ext, h=h)):
        piece()
    f_ext = stage["f_ext"]
    f = f_ext[:tm]

    fb = FF_BLOCK
    nblk = wg_ref.shape[1] // fb
    blk = lambda ref, j: ref[:, j * fb:(j + 1) * fb]
    proj = lambda j: (_dot(f_ext, blk(wg_ref, j)), _dot(f, blk(wu_ref, j)))
    nxt = proj(0)
    for j in range(nblk):
        u_ext, up = nxt
        if j + 1 < nblk:
            nxt = proj(j + 1)
        act_ref[:, j * fb:(j + 1) * fb] = (_gelu(_conv3(u_ext, tm, blk(cw_ref, j), blk(cb_ref, j))) * up).astype(BF16)
    h = stage["h"] + _dot(act_ref[...], wd_ref[...])
    gate = _sigmoid(_dot(_rms(h, pnw_ref[...]).astype(BF16), pwg_ref[...]) + pgb_ref[...])
    h = h + _dot(p_ref[...].astype(BF16), pwp_ref[...]) * gate
    y_ref[...] = _rms(h, fnw_ref[...])


def _const_spec(shape):
    nd = len(shape)
    return pl.BlockSpec(shape, lambda *_: (0,) * nd, pipeline_mode=pl.Buffered(1))


def _row_specs(tm, width, n_rows, halo=SUBLANES):
    per = tm // halo
    last = n_rows // halo - 1
    return [
        pl.BlockSpec((tm, width), lambda i: (i, 0)),
        pl.BlockSpec((halo, width), lambda i: (jnp.minimum((i + 1) * per, last), 0)),
        pl.BlockSpec((halo, width), lambda i: (jnp.maximum(i * per - 1, 0), 0)),
    ]


def _params(sem):
    return pltpu.CompilerParams(dimension_semantics=sem, vmem_limit_bytes=VMEM_LIMIT)


def _layer(x, p, w):
    bsz, seq, d = x.shape
    n = bsz * seq
    tm = ROW_TILE
    assert seq % tm == 0 and seq % CHUNK == 0
    tiles_per_seq = seq // tm
    x2 = x.reshape(n, d)
    row = lambda width: pl.BlockSpec((tm, width), lambda i: (i, 0))

    rows_per_chunk = SUBLANES
    ti = IN_TILE
    assert seq % ti == 0
    row_i = lambda width: pl.BlockSpec((ti, width), lambda i: (i, 0))
    pos_tile = pl.BlockSpec((ti, HD), lambda i: (i % (seq // ti), 0))
    qv, kt, og, stats, brow = pl.pallas_call(
        functools.partial(_in_proj_kernel, tiles_per_seq=seq // ti),
        grid=(n // ti,),
        in_specs=_row_specs(ti, d, n) + [pos_tile, pos_tile] + [_const_spec(a.shape) for a in w["in_proj"]],
        out_specs=[row_i(4 * GW), pl.BlockSpec((ti // CHUNK, 2 * GW, CHUNK), lambda i: (i, 0, 0)), row_i(2 * GW),
                   row_i(LANES), pl.BlockSpec((ti // CHUNK * rows_per_chunk, LANES), lambda i: (i, 0))],
        out_shape=[jax.ShapeDtypeStruct((n, 4 * GW), BF16), jax.ShapeDtypeStruct((n // CHUNK, 2 * GW, CHUNK), BF16),
                   jax.ShapeDtypeStruct((n, 2 * GW), BF16), jax.ShapeDtypeStruct((n, LANES), F32),
                   jax.ShapeDtypeStruct((n // CHUNK * rows_per_chunk, LANES), F32)],
        compiler_params=_params(("parallel",)),
        name="in_proj",
    )(x2, x2, x2, *w["rope"], *w["in_proj"])

    cb = MIX_CHUNKS
    assert seq % (cb * CHUNK) == 0
    nb = seq // (cb * CHUNK)
    fwd_i = lambda b, t: b * nb + t
    bwd_i = lambda b, t: b * nb + nb - 1 - t

    def block_specs(idx):
        return [pl.BlockSpec((cb * CHUNK, 4 * GW), lambda b, t: (idx(b, t), 0)),
                pl.BlockSpec((cb, 2 * GW, CHUNK), lambda b, t: (idx(b, t), 0, 0)),
                pl.BlockSpec((cb * CHUNK, LANES), lambda b, t: (idx(b, t), 0)),
                pl.BlockSpec((cb * rows_per_chunk, LANES), lambda b, t: (idx(b, t), 0))]

    out_f, out_b = pl.pallas_call(
        _mixer_kernel,
        grid=(bsz, nb),
        in_specs=[_const_spec(w["ret_logit"].shape)] + block_specs(fwd_i) + block_specs(bwd_i),
        out_specs=[pl.BlockSpec((cb * CHUNK, 2 * GW), lambda b, t: (fwd_i(b, t), 0)),
                   pl.BlockSpec((cb * CHUNK, 2 * GW), lambda b, t: (bwd_i(b, t), 0))],
        out_shape=[jax.ShapeDtypeStruct((n, 2 * GW), BF16)] * 2,
        scratch_shapes=[
            pltpu.VMEM((HEADS, CHUNK, CHUNK), F32),
            pltpu.VMEM((CHUNK, GW), F32), pltpu.VMEM((CHUNK, GW), F32),
            pltpu.VMEM((2 * HEADS, CHUNK), F32),
            pltpu.VMEM((2, GW), F32),
            pltpu.VMEM((HEADS, HD, HD), F32), pltpu.VMEM((HEADS, HD, HD), F32),
            pltpu.VMEM((2 * HEADS, HD, 2 * HD), F32),
            pltpu.VMEM((2 * HEADS, LANES), F32),
        ],
        compiler_params=_params(("arbitrary", "arbitrary")),
        name="mixers",
    )(w["ret_logit"], *([qv, kt, stats, brow] * 2))

    p2 = p.reshape(n, p.shape[-1])
    bf16_rows = 2 * SUBLANES
    mix_rows = _row_specs(tm, d, n, bf16_rows) * 3 + _row_specs(tm, d, n)
    y = pl.pallas_call(
        functools.partial(_out_ffn_kernel, tiles_per_seq=tiles_per_seq),
        grid=(n // tm,),
        in_specs=mix_rows + [row(p2.shape[-1])] + [_const_spec(a.shape) for a in (*w["mix_out"], *w["ffn"])],
        out_specs=row(d),
        out_shape=jax.ShapeDtypeStruct((n, d), F32),
        scratch_shapes=[pltpu.VMEM((tm, w["d_ff"]), BF16)],
        compiler_params=_params(("parallel",)),
        name="out_ffn",
    )(*((out_f,) * 3 + (out_b,) * 3 + (og,) * 3 + (x2,) * 3), p2, *w["mix_out"], *w["ffn"])
    return y.reshape(bsz, seq, d)


def kernel(x_prompt, x_sample, p_prompt, p_sample, norm_mix_w, w_in, mlstm_conv_w, mlstm_conv_b, mlstm_gate_b, ret_decay_logit, ret_gn_w, mlstm_gn_w, w_out, norm_ffn_w, ffn_w_gate, ffn_w_up, ffn_conv_w, ffn_conv_b, ffn_w_down, ple_w_proj, ple_norm_w, ple_w_gate, ple_gate_b, final_norm_w):
    depth = w_in.shape[0]
    d = x_prompt.shape[-1]
    d_ff = ffn_w_gate.shape[-1]
    assert d_ff % FF_BLOCK == 0
    rowv = lambda a: a.reshape(1, -1).astype(F32)
    inv = ROPE_BASE ** (-jnp.arange(0, HD, 2, dtype=F32) / HD)
    inv = jnp.concatenate([inv, inv]).reshape(1, HD)
    max_seq = max(x_prompt.shape[1], x_sample.shape[1])
    assert max_seq % ROW_TILE == 0
    rope = pl.pallas_call(
        _rope_table_kernel,
        grid=(max_seq // ROW_TILE,),
        in_specs=[_const_spec(inv.shape)],
        out_specs=[pl.BlockSpec((ROW_TILE, HD), lambda i: (i, 0))] * 2,
        out_shape=[jax.ShapeDtypeStruct((max_seq, HD), F32)] * 2,
        compiler_params=_params(("parallel",)),
        name="rope_tables",
    )(inv)

    assert depth == 1
    outs = [x_prompt, x_sample]
    for l in range(depth):
        wi = w_in[l].astype(BF16)
        n_gate = 2 * HEADS
        wgate = jnp.zeros((d, 2 * LANES), BF16)
        wgate = wgate.at[:, 0:n_gate].set(wi[:, 8 * GW:8 * GW + n_gate])
        wgate = wgate.at[:, LANES:LANES + n_gate].set(wi[:, 8 * GW + n_gate:8 * GW + 2 * n_gate])
        gb = mlstm_gate_b[l].astype(F32).reshape(-1)
        gbias = jnp.zeros((1, 2 * LANES), F32)
        gbias = gbias.at[0, 0:n_gate].set(gb[:n_gate]).at[0, LANES:LANES + n_gate].set(gb[n_gate:])
        w = {
            "rope": rope,
            "in_proj": [rowv(norm_mix_w[l]), wi, wgate, gbias, mlstm_conv_w[l].astype(F32), rowv(mlstm_conv_b[l])],
            "ret_logit": jnp.repeat(ret_decay_logit[l].astype(F32), HD, axis=1),
            "d_ff": d_ff,
            "mix_out": [jnp.concatenate([rowv(ret_gn_w[l]), rowv(mlstm_gn_w[l])], axis=1), w_out[l].astype(BF16),
                        rowv(norm_ffn_w[l])],
            "ffn": [ffn_w_gate[l].astype(BF16), ffn_w_up[l].astype(BF16), ffn_conv_w[l].astype(F32),
                    rowv(ffn_conv_b[l]), ffn_w_down[l].astype(BF16), rowv(ple_norm_w[l]), ple_w_gate[l].astype(BF16),
                    rowv(ple_gate_b[l]), ple_w_proj[l].astype(BF16), rowv(final_norm_w)],
        }
        outs = [_layer(h, p[l], w) for h, p in zip(outs, (p_prompt, p_sample))]
    return tuple(outs)
```

```python
import functools

import jax
import jax.numpy as jnp
from jax import lax
from jax.experimental import pallas as pl
from jax.experimental.pallas import tpu as pltpu

F32 = jnp.float32
BF16 = jnp.bfloat16

HEADS = 4
HD = 128
GW = HEADS * HD
CHUNK = 128
ROPE_BASE = 10000.0
RMS_EPS = 1e-6
HN_EPS = 1e-5
SUBLANES = 8
LANES = 128
ROW_TILE = 512
IN_TILE = 1024
FF_BLOCK = 256
MIX_CHUNKS = 8
MLSTM_GROUP = 4
VMEM_LIMIT = 56 * 1024 * 1024


def _rms(x, w):
    return x * lax.rsqrt(jnp.mean(x * x, axis=-1, keepdims=True) + RMS_EPS) * w


def _rms_project(x, w_ref, projections, kb):
    inv = lax.rsqrt(jnp.mean(x * x, axis=-1, keepdims=True) + RMS_EPS)
    blocks, outs = [], [None] * len(projections)
    for k in range(0, x.shape[1], kb):
        xk = (x[:, k:k + kb] * inv * w_ref[:, k:k + kb]).astype(BF16)
        blocks.append(xk)
        for i, (rows, p_ref, lo, width) in enumerate(projections):
            part = _dot(xk[:rows], p_ref[k:k + kb, lo:lo + width])
            outs[i] = part if outs[i] is None else outs[i] + part
    return jnp.concatenate(blocks, axis=1), outs


def _log_sigmoid(x):
    return jnp.minimum(x, 0.0) - jnp.log1p(jnp.exp(-jnp.abs(x)))


def _sigmoid(x):
    return 1.0 / (1.0 + jnp.exp(-x))


def _gelu(x):
    return 0.5 * x * (1.0 + lax.erf(x * (0.5 ** 0.5)))


def _dot(a, b):
    return jnp.dot(a, b, preferred_element_type=F32)


def _dot_nt(a, b):
    return lax.dot_general(a, b, (((1,), (1,)), ((), ())), preferred_element_type=F32)


def _ext_rows(main, nxt, prv, tin, tiles_per_seq):
    nxt = jnp.where(tin == tiles_per_seq - 1, 0.0, nxt)
    prv = jnp.where(tin == 0, 0.0, prv)
    return jnp.concatenate([main, nxt, prv], axis=0)


def _conv3(ext, rows, w, b):
    n = ext.shape[0]
    up = pltpu.roll(ext, 1, axis=0)[:rows]
    dn = pltpu.roll(ext, n - 1, axis=0)[:rows]
    return up * w[0:1] + ext[:rows] * w[1:2] + dn * w[2:3] + b


def _chunk_stats(ig, lf):
    tm = ig.shape[0]
    ii = lax.broadcasted_iota(jnp.int32, (CHUNK, CHUNK), 0)
    jj = lax.broadcasted_iota(jnp.int32, (CHUNK, CHUNK), 1)
    tril = (ii >= jj).astype(F32)
    triu = (jj >= ii).astype(F32)
    hi = lax.Precision.HIGHEST
    a = jnp.concatenate([
        jnp.where(jj < HEADS,
                  jnp.dot(tril, lf[r:r + CHUNK], precision=hi, preferred_element_type=F32),
                  jnp.dot(triu, lf[r:r + CHUNK], precision=hi, preferred_element_type=F32))
        for r in range(0, tm, CHUNK)], axis=0)
    b = ig - a
    r = lax.broadcasted_iota(jnp.int32, (tm, LANES), 0) % CHUNK
    pre = suf = b
    s = 1
    while s < CHUNK:
        pre = jnp.maximum(pre, jnp.where(r >= s, pltpu.roll(pre, s, axis=0), -jnp.inf))
        suf = jnp.maximum(suf, jnp.where(r < CHUNK - s, pltpu.roll(suf, tm - s, axis=0), -jnp.inf))
        s *= 2
    cm = jnp.where(lax.broadcasted_iota(jnp.int32, (tm, LANES), 1) < HEADS, pre, suf)
    return a, b, cm


def _rope_table_kernel(inv_ref, cos_ref, sin_ref, cr_ref, sr_ref):
    tm = cos_ref.shape[0]

    @pl.when(pl.program_id(0) == 0)
    def _offsets():
        off = lax.broadcasted_iota(jnp.int32, (tm, HD), 0).astype(F32) * inv_ref[...]
        cr_ref[...] = jnp.cos(off)
        sr_ref[...] = jnp.sin(off)

    base = (pl.program_id(0) * tm).astype(F32) * inv_ref[...]
    cb, sb = jnp.cos(base), jnp.sin(base)
    cos_ref[...] = cb * cr_ref[...] - sb * sr_ref[...]
    sin = sb * cr_ref[...] + cb * sr_ref[...]
    lane = lax.broadcasted_iota(jnp.int32, (tm, HD), 1)
    sin_ref[...] = jnp.where(lane < HD // 2, -sin, sin)


def _in_proj_kernel(x_ref, xn_ref, xp_ref, cos_ref, sin_ref, nw_ref, w_ref, wg_ref, gb_ref, cw_ref, cb_ref,
                    qv_ref, kt_ref, og_ref, st_ref, br_ref, *, tiles_per_seq):
    tm = x_ref.shape[0]
    tin = pl.program_id(0) % tiles_per_seq
    x_ext = _ext_rows(x_ref[...], xn_ref[...], xp_ref[...], tin, tiles_per_seq)
    a_ext, (gate_proj,) = _rms_project(x_ext, nw_ref, [(tm, wg_ref, 0, wg_ref.shape[1])], 2 * LANES)
    a = a_ext[:tm]
    scale = HD ** -0.5

    PW = 2 * HD

    def rope(t, i):
        th = t[:, i * HD:(i + 1) * HD]
        return th * cos_ref[...] + pltpu.roll(th, HD // 2, axis=1) * sin_ref[...]

    def put_rq(t, lo):
        for i in range(PW // HD):
            qv_ref[:, lo + i * HD:lo + (i + 1) * HD] = (rope(t, i) * scale).astype(BF16)

    def put_keys(row_lo, k):
        kt = k.T.astype(BF16)
        for ch in range(tm // CHUNK):
            kt_ref[ch, row_lo:row_lo + HD, :] = kt[:, ch * CHUNK:(ch + 1) * CHUNK]

    def put_rk(t, lo):
        for i in range(PW // HD):
            put_keys(lo + i * HD, rope(t, i))

    def conv_silu(t, lo):
        c = _conv3(t, tm, cw_ref[:, lo:lo + PW], cb_ref[:, lo:lo + PW])
        return c * _sigmoid(c)

    def put_mq(t, lo):
        qv_ref[:, 2 * GW + lo:2 * GW + lo + PW] = (conv_silu(t, lo) * scale).astype(BF16)

    def put_mk(t, lo):
        c = conv_silu(t, GW + lo)
        for i in range(PW // HD):
            put_keys(GW + lo + i * HD, c[:, i * HD:(i + 1) * HD])

    def put_gates(t, lo):
        g = t + gb_ref[...]
        acum, b, cmax = _chunk_stats(g[:, :LANES], _log_sigmoid(g[:, LANES:]))
        lane = lax.broadcasted_iota(jnp.int32, acum.shape, 1)
        st_ref[...] = jnp.where(lane < 2 * HEADS, acum, pltpu.roll(cmax, 2 * HEADS, axis=1))
        for ch in range(tm // CHUNK):
            br_ref[ch * SUBLANES:(ch + 1) * SUBLANES, :] = b[ch * CHUNK:(ch + 1) * CHUNK].T[:SUBLANES]

    def put(ref, base):
        def store(t, lo):
            ref[:, base + lo:base + lo + PW] = t.astype(BF16)
        return store

    def pieces(lhs, w_ref, w_lo, consume):
        return [(lhs, w_ref, w_lo + lo, consume, lo) for lo in range(0, GW, PW)]

    rq, rk = pieces(a, w_ref, 0, put_rq), pieces(a, w_ref, GW, put_rk)
    rv, rg = pieces(a, w_ref, 2 * GW, put(qv_ref, GW)), pieces(a, w_ref, 3 * GW, put(og_ref, 0))
    mq, mk = pieces(a_ext, w_ref, 4 * GW, put_mq), pieces(a_ext, w_ref, 5 * GW, put_mk)
    mv, mo = pieces(a, w_ref, 6 * GW, put(qv_ref, 3 * GW)), pieces(a, w_ref, 7 * GW, put(og_ref, GW))
    gates = [(None, None, 0, put_gates, 0)]
    order = [(gates[0],), (mq[0], rv[0]), (rq[0], rv[1]), (mq[1], rg[0]), (rq[1], rg[1]), (mk[0], mv[0]),
             (rk[0], mv[1]), (mk[1], mo[0]), (rk[1], mo[1])]
    project = lambda stage: [_dot(piece[0], piece[1][:, piece[2]:piece[2] + PW]) for piece in stage]
    pending = [gate_proj]
    for k, stage in enumerate(order):
        ready = pending
        if k + 1 < len(order):
            pending = project(order[k + 1])
        for piece, t in zip(stage, ready):
            piece[3](t, piece[4])


def _mixer_kernel(lg_ref, qf_ref, ktf_ref, stf_ref, brf_ref, qb_ref, ktb_ref, stb_ref, brb_ref,
                  of_ref, ob_ref, dm_ref, qdf_ref, qdb_ref, kd_ref, cd_ref, sf_ref, sb_ref, c_ref, m_ref):
    L = CHUNK
    ii = lax.broadcasted_iota(jnp.int32, (L, L), 0)
    jj = lax.broadcasted_iota(jnp.int32, (L, L), 1)

    @pl.when(pl.program_id(1) == 0)
    def _init():
        lg = _log_sigmoid(lg_ref[...])
        lgf, lgb = lg[0:1], lg[1:2]
        row = lax.broadcasted_iota(jnp.int32, (L, GW), 0).astype(F32)
        qdf_ref[...] = jnp.exp((row + 1.0) * lgf)
        qdb_ref[...] = jnp.exp((L - row) * lgb)
        cd_ref[...] = jnp.exp(float(L) * lg)
        diff = (ii - jj).astype(F32)
        col = lax.broadcasted_iota(jnp.int32, (1, L), 1).astype(F32)
        for h in range(HEADS):
            sl = slice(h * HD, (h + 1) * HD)
            dm_ref[h] = jnp.where(diff >= 0, jnp.exp(diff * lgf[:, sl]), jnp.exp(-diff * lgb[:, sl]))
            kd_ref[h:h + 1, :] = jnp.exp((L - 1.0 - col) * lgf[:, sl])
            kd_ref[HEADS + h:HEADS + h + 1, :] = jnp.exp(col * lgb[:, sl])
        sf_ref[...] = jnp.zeros_like(sf_ref)
        sb_ref[...] = jnp.zeros_like(sb_ref)
        c_ref[...] = jnp.zeros_like(c_ref)
        m_ref[...] = jnp.zeros_like(m_ref)

    n_sub = qf_ref.shape[0] // L
    H = range(HEADS)
    C = range(2 * HEADS)
    sls = [slice(h * HD, (h + 1) * HD) for h in H]
    tril = ii >= jj
    triu = jj >= ii
    is_f = jj % (2 * HEADS) < HEADS
    ones_blk = jnp.ones((L, HD), BF16)

    def chunk(i, carry):
        i_b = n_sub - 1 - i
        rows_f = pl.ds(pl.multiple_of(i * L, L), L)
        rows_b = pl.ds(pl.multiple_of(i_b * L, L), L)
        part = lambda ref, rows, g, h: ref[rows, g * GW + h * HD:g * GW + (h + 1) * HD]

        def retention():
            qf, vf = (lambda h: part(qf_ref, rows_f, 0, h)), (lambda h: part(qf_ref, rows_f, 1, h))
            qb, vb = (lambda h: part(qb_ref, rows_b, 0, h)), (lambda h: part(qb_ref, rows_b, 1, h))
            ktf = lambda h: ktf_ref[i, sls[h], :]
            ktb = lambda h: ktb_ref[i_b, sls[h], :]
            qkc = [_dot(qf(h), jnp.concatenate([ktf(h), sf_ref[h].astype(BF16)], axis=1)) for h in H]
            p = [(qkc[h][:, :L] * dm_ref[h]).astype(BF16) for h in H]
            cross_b = [_dot(qb(h), sb_ref[h].astype(BF16)) for h in H]
            kdf = [(ktf(h).astype(F32) * kd_ref[h:h + 1, :]).astype(BF16) for h in H]
            pv = [_dot(jnp.concatenate([p[h], kdf[h]], axis=0), vf(h)) for h in H]
            for h in H:
                of_ref[rows_f, sls[h]] = (pv[h][:L] + qkc[h][:, L:] * qdf_ref[:, sls[h]]).astype(of_ref.dtype)
                ob_ref[rows_b, sls[h]] = (cross_b[h] * qdb_ref[:, sls[h]]).astype(ob_ref.dtype)
            upd_b = [_dot((ktb(h).astype(F32) * kd_ref[HEADS + h:HEADS + h + 1, :]).astype(BF16), vb(h)) for h in H]
            for h in H:
                sf_ref[h] = cd_ref[0:1, sls[h]] * sf_ref[h] + pv[h][L:]
                sb_ref[h] = cd_ref[1:2, sls[h]] * sb_ref[h] + upd_b[h]

        st = jnp.where(is_f, stf_ref[rows_f, :], stb_ref[rows_b, :])
        a_col = lambda c: st[:, c:c + 1]
        cm_col = lambda c: st[:, 2 * HEADS + c:2 * HEADS + c + 1]
        b_rows = jnp.where(lax.broadcasted_iota(jnp.int32, (SUBLANES, L), 0) < HEADS,
                           brf_ref[pl.ds(pl.multiple_of(i * SUBLANES, SUBLANES), SUBLANES), :],
                           brb_ref[pl.ds(pl.multiple_of(i_b * SUBLANES, SUBLANES), SUBLANES), :])

        def mlstm_group(cs):
            back = {c: c >= HEADS for c in cs}
            hd = {c: c % HEADS for c in cs}
            last = {c: 0 if back[c] else L - 1 for c in cs}
            q = lambda c: part(qb_ref, rows_b, 2, hd[c]) if back[c] else part(qf_ref, rows_f, 2, hd[c])
            v = lambda c: part(qb_ref, rows_b, 3, hd[c]) if back[c] else part(qf_ref, rows_f, 3, hd[c])
            kt = lambda c: (ktb_ref[i_b, GW + hd[c] * HD:GW + (hd[c] + 1) * HD, :] if back[c] else
                            ktf_ref[i, GW + hd[c] * HD:GW + (hd[c] + 1) * HD, :])
            v1 = lambda c: jnp.concatenate([v(c), ones_blk], axis=1)
            m_old = {c: m_ref[c:c + 1, 0:1] for c in cs}
            mu = {c: jnp.maximum(jnp.broadcast_to(cm_col(c), (L, L)), m_old[c]) for c in cs}
            e = {c: jnp.exp(jnp.where(triu if back[c] else tril, b_rows[c:c + 1, :] - mu[c], -jnp.inf)) for c in cs}
            qk = {c: _dot(q(c), kt(c)) for c in cs}
            s = {c: (qk[c] * e[c]).astype(BF16) for c in cs}
            qc = {c: _dot(q(c), c_ref[c].astype(BF16)) for c in cs}
            mu_last = {c: mu[c][last[c]:last[c] + 1, 0:1] for c in cs}
            kw = {c: (kt(c).astype(F32) * jnp.exp(b_rows[c:c + 1, :] - mu_last[c])).astype(BF16) for c in cs}
            sv = {c: _dot(jnp.concatenate([s[c], kw[c]], axis=0), v1(c)) for c in cs}
            for c in cs:
                w = jnp.exp(m_old[c] - mu[c])
                floor = jnp.exp(-(jnp.broadcast_to(a_col(c), (L, L)) + mu[c]))
                den = jnp.maximum(jnp.abs(w * qc[c][:, HD:] + sv[c][:L, HD:]), floor)
                dst, rows = (ob_ref, rows_b) if back[c] else (of_ref, rows_f)
                h_c = (w * qc[c][:, :HD] + sv[c][:L, :HD]) / den
                dst[rows, GW + hd[c] * HD:GW + (hd[c] + 1) * HD] = h_c.astype(dst.dtype)
            for c in cs:
                c_ref[c] = jnp.exp(m_old[c] - mu_last[c]) * c_ref[c] + sv[c][L:]
                m_ref[c:c + 1, :] = jnp.broadcast_to(a_col(c)[last[c]:last[c] + 1] + mu_last[c], (1, LANES))

        retention()
        for g in range(0, 2 * HEADS, MLSTM_GROUP):
            mlstm_group(range(g, g + MLSTM_GROUP))
        return carry

    lax.fori_loop(0, n_sub, chunk, 0)


def _mix_out_pieces(rows, tin, tiles_per_seq, gn_ref, wo_ref, nfw_ref, emit):
    of_ref, ofn_ref, ofp_ref, ob_ref, obn_ref, obp_ref, og_ref, ogn_ref, ogp_ref, x_ref, xn_ref, xp_ref = rows
    tm = x_ref.shape[0]
    n_ext = tm + 2 * SUBLANES
    v = {}

    def ext(m, nx, pv):
        return jnp.concatenate([m[...].astype(F32), nx[...].astype(F32)[:SUBLANES],
                                pv[...].astype(F32)[pv.shape[0] - SUBLANES:]], axis=0)

    def load():
        v["y"] = ext(of_ref, ofn_ref, ofp_ref) + ext(ob_ref, obn_ref, obp_ref)
        v["z"] = ext(og_ref, ogn_ref, ogp_ref)
        v["h"] = ext(x_ref, xn_ref, xp_ref)
        v["parts"] = []

    def head(g):
        sl = slice(g * HD, (g + 1) * HD)
        yg = v["y"][:, sl]
        d = yg - jnp.mean(yg, axis=-1, keepdims=True)
        n = d * lax.rsqrt(jnp.mean(d * d, axis=-1, keepdims=True) + HN_EPS) * gn_ref[:, sl]
        zg = v["z"][:, sl]
        gate = zg * _sigmoid(zg) if g < HEADS else _sigmoid(zg)
        v["parts"].append((n * gate).astype(BF16))
        if g % 2 == 1:
            pair = jnp.concatenate(v["parts"][-2:], axis=1)
            v["h"] = v["h"] + _dot(pair, wo_ref[(g - 1) * HD:(g + 1) * HD, :])

    def finish():
        r = lax.broadcasted_iota(jnp.int32, (n_ext, 1), 0)
        pad = ((r >= tm) & (r < tm + SUBLANES) & (tin == tiles_per_seq - 1)) | ((r >= tm + SUBLANES) & (tin == 0))
        emit(jnp.where(pad, 0.0, _rms(v["h"], nfw_ref[...])).astype(BF16), v["h"][:tm])

    return [load] + [functools.partial(head, g) for g in range(2 * HEADS)] + [finish]


def _out_ffn_kernel(*refs, tiles_per_seq):
    rows = refs[:12]
    (p_ref, gn_ref, wo_ref, nfw_ref, wg_ref, wu_ref, cw_ref, cb_ref, wd_ref, pnw_ref, pwg_ref, pgb_ref, pwp_ref, fnw_ref,
     y_ref, act_ref) = refs[12:]
    tm = p_ref.shape[0]
    stage = {}
    tin = pl.program_id(0) % tiles_per_seq
    for piece in _mix_out_pieces(rows, tin, tiles_per_seq, gn_ref, wo_ref, nfw_ref,
                                 lambda f_ext, h: stage.update(f_ext=f_ext, h=h)):
        piece()
    f_ext = stage["f_ext"]
    f = f_ext[:tm]

    fb = FF_BLOCK
    nblk = wg_ref.shape[1] // fb
    blk = lambda ref, j: ref[:, j * fb:(j + 1) * fb]
    proj = lambda j: (_dot(f_ext, blk(wg_ref, j)), _dot(f, blk(wu_ref, j)))
    nxt = proj(0)
    for j in range(nblk):
        u_ext, up = nxt
        if j + 1 < nblk:
            nxt = proj(j + 1)
        act_ref[:, j * fb:(j + 1) * fb] = (_gelu(_conv3(u_ext, tm, blk(cw_ref, j), blk(cb_ref, j))) * up).astype(BF16)
    h = stage["h"] + _dot(act_ref[...], wd_ref[...])
    gate = _sigmoid(_dot(_rms(h, pnw_ref[...]).astype(BF16), pwg_ref[...]) + pgb_ref[...])
    h = h + _dot(p_ref[...].astype(BF16), pwp_ref[...]) * gate
    y_ref[...] = _rms(h, fnw_ref[...])


def _const_spec(shape):
    nd = len(shape)
    return pl.BlockSpec(shape, lambda *_: (0,) * nd, pipeline_mode=pl.Buffered(1))


def _row_specs(tm, width, n_rows, halo=SUBLANES):
    per = tm // halo
    last = n_rows // halo - 1
    return [
        pl.BlockSpec((tm, width), lambda i: (i, 0)),
        pl.BlockSpec((halo, width), lambda i: (jnp.minimum((i + 1) * per, last), 0)),
        pl.BlockSpec((halo, width), lambda i: (jnp.maximum(i * per - 1, 0), 0)),
    ]


def _params(sem):
    return pltpu.CompilerParams(dimension_semantics=sem, vmem_limit_bytes=VMEM_LIMIT)


def _layer(x, p, w):
    bsz, seq, d = x.shape
    n = bsz * seq
    tm = ROW_TILE
    assert seq % tm == 0 and seq % CHUNK == 0
    tiles_per_seq = seq // tm
    x2 = x.reshape(n, d)
    row = lambda width: pl.BlockSpec((tm, width), lambda i: (i, 0))

    rows_per_chunk = SUBLANES
    ti = IN_TILE
    assert seq % ti == 0
    row_i = lambda width: pl.BlockSpec((ti, width), lambda i: (i, 0))
    pos_tile = pl.BlockSpec((ti, HD), lambda i: (i % (seq // ti), 0))
    qv, kt, og, stats, brow = pl.pallas_call(
        functools.partial(_in_proj_kernel, tiles_per_seq=seq // ti),
        grid=(n // ti,),
        in_specs=_row_specs(ti, d, n) + [pos_tile, pos_tile] + [_const_spec(a.shape) for a in w["in_proj"]],
        out_specs=[row_i(4 * GW), pl.BlockSpec((ti // CHUNK, 2 * GW, CHUNK), lambda i: (i, 0, 0)), row_i(2 * GW),
                   row_i(LANES), pl.BlockSpec((ti // CHUNK * rows_per_chunk, LANES), lambda i: (i, 0))],
        out_shape=[jax.ShapeDtypeStruct((n, 4 * GW), BF16), jax.ShapeDtypeStruct((n // CHUNK, 2 * GW, CHUNK), BF16),
                   jax.ShapeDtypeStruct((n, 2 * GW), BF16), jax.ShapeDtypeStruct((n, LANES), F32),
                   jax.ShapeDtypeStruct((n // CHUNK * rows_per_chunk, LANES), F32)],
        compiler_params=_params(("parallel",)),
        name="in_proj",
    )(x2, x2, x2, *w["rope"], *w["in_proj"])

    cb = MIX_CHUNKS
    assert seq % (cb * CHUNK) == 0
    nb = seq // (cb * CHUNK)
    fwd_i = lambda b, t: b * nb + t
    bwd_i = lambda b, t: b * nb + nb - 1 - t

    def block_specs(idx):
        return [pl.BlockSpec((cb * CHUNK, 4 * GW), lambda b, t: (idx(b, t), 0)),
                pl.BlockSpec((cb, 2 * GW, CHUNK), lambda b, t: (idx(b, t), 0, 0)),
                pl.BlockSpec((cb * CHUNK, LANES), lambda b, t: (idx(b, t), 0)),
                pl.BlockSpec((cb * rows_per_chunk, LANES), lambda b, t: (idx(b, t), 0))]

    out_f, out_b = pl.pallas_call(
        _mixer_kernel,
        grid=(bsz, nb),
        in_specs=[_const_spec(w["ret_logit"].shape)] + block_specs(fwd_i) + block_specs(bwd_i),
        out_specs=[pl.BlockSpec((cb * CHUNK, 2 * GW), lambda b, t: (fwd_i(b, t), 0)),
                   pl.BlockSpec((cb * CHUNK, 2 * GW), lambda b, t: (bwd_i(b, t), 0))],
        out_shape=[jax.ShapeDtypeStruct((n, 2 * GW), BF16)] * 2,
        scratch_shapes=[
            pltpu.VMEM((HEADS, CHUNK, CHUNK), F32),
            pltpu.VMEM((CHUNK, GW), F32), pltpu.VMEM((CHUNK, GW), F32),
            pltpu.VMEM((2 * HEADS, CHUNK), F32),
            pltpu.VMEM((2, GW), F32),
            pltpu.VMEM((HEADS, HD, HD), F32), pltpu.VMEM((HEADS, HD, HD), F32),
            pltpu.VMEM((2 * HEADS, HD, 2 * HD), F32),
            pltpu.VMEM((2 * HEADS, LANES), F32),
        ],
        compiler_params=_params(("arbitrary", "arbitrary")),
        name="mixers",
    )(w["ret_logit"], *([qv, kt, stats, brow] * 2))

    p2 = p.reshape(n, p.shape[-1])
    bf16_rows = 2 * SUBLANES
    mix_rows = _row_specs(tm, d, n, bf16_rows) * 3 + _row_specs(tm, d, n)
    y = pl.pallas_call(
        functools.partial(_out_ffn_kernel, tiles_per_seq=tiles_per_seq),
        grid=(n // tm,),
        in_specs=mix_rows + [row(p2.shape[-1])] + [_const_spec(a.shape) for a in (*w["mix_out"], *w["ffn"])],
        out_specs=row(d),
        out_shape=jax.ShapeDtypeStruct((n, d), F32),
        scratch_shapes=[pltpu.VMEM((tm, w["d_ff"]), BF16)],
        compiler_params=_params(("parallel",)),
        name="out_ffn",
    )(*((out_f,) * 3 + (out_b,) * 3 + (og,) * 3 + (x2,) * 3), p2, *w["mix_out"], *w["ffn"])
    return y.reshape(bsz, seq, d)


def kernel(x_prompt, x_sample, p_prompt, p_sample, norm_mix_w, w_in, mlstm_conv_w, mlstm_conv_b, mlstm_gate_b, ret_decay_logit, ret_gn_w, mlstm_gn_w, w_out, norm_ffn_w, ffn_w_gate, ffn_w_up, ffn_conv_w, ffn_conv_b, ffn_w_down, ple_w_proj, ple_norm_w, ple_w_gate, ple_gate_b, final_norm_w):
    depth = w_in.shape[0]
    d = x_prompt.shape[-1]
    d_ff = ffn_w_gate.shape[-1]
    assert d_ff % FF_BLOCK == 0
    rowv = lambda a: a.reshape(1, -1).astype(F32)
    inv = ROPE_BASE ** (-jnp.arange(0, HD, 2, dtype=F32) / HD)
    inv = jnp.concatenate([inv, inv]).reshape(1, HD)
    max_seq = max(x_prompt.shape[1], x_sample.shape[1])
    assert max_seq % ROW_TILE == 0
    rope = pl.pallas_call(
        _rope_table_kernel,
        grid=(max_seq // ROW_TILE,),
        in_specs=[_const_spec(inv.shape)],
        out_specs=[pl.BlockSpec((ROW_TILE, HD), lambda i: (i, 0))] * 2,
        out_shape=[jax.ShapeDtypeStruct((max_seq, HD), F32)] * 2,
        scratch_shapes=[pltpu.VMEM((ROW_TILE, HD), F32)] * 2,
        compiler_params=_params(("arbitrary",)),
        name="rope_tables",
    )(inv)

    assert depth == 1
    outs = [x_prompt, x_sample]
    for l in range(depth):
        wi = w_in[l].astype(BF16)
        n_gate = 2 * HEADS
        wgate = jnp.zeros((d, 2 * LANES), BF16)
        wgate = wgate.at[:, 0:n_gate].set(wi[:, 8 * GW:8 * GW + n_gate])
        wgate = wgate.at[:, LANES:LANES + n_gate].set(wi[:, 8 * GW + n_gate:8 * GW + 2 * n_gate])
        gb = mlstm_gate_b[l].astype(F32).reshape(-1)
        gbias = jnp.zeros((1, 2 * LANES), F32)
        gbias = gbias.at[0, 0:n_gate].set(gb[:n_gate]).at[0, LANES:LANES + n_gate].set(gb[n_gate:])
        w = {
            "rope": rope,
            "in_proj": [rowv(norm_mix_w[l]), wi, wgate, gbias, mlstm_conv_w[l].astype(F32), rowv(mlstm_conv_b[l])],
            "ret_logit": jnp.repeat(ret_decay_logit[l].astype(F32), HD, axis=1),
            "d_ff": d_ff,
            "mix_out": [jnp.concatenate([rowv(ret_gn_w[l]), rowv(mlstm_gn_w[l])], axis=1), w_out[l].astype(BF16),
                        rowv(norm_ffn_w[l])],
            "ffn": [ffn_w_gate[l].astype(BF16), ffn_w_up[l].astype(BF16), ffn_conv_w[l].astype(F32),
                    rowv(ffn_conv_b[l]), ffn_w_down[l].astype(BF16), rowv(ple_norm_w[l]), ple_w_gate[l].astype(BF16),
                    rowv(ple_gate_b[l]), ple_w_proj[l].astype(BF16), rowv(final_norm_w)],
        }
        outs = [_layer(h, p[l], w) for h, p in zip(outs, (p_prompt, p_sample))]
    return tuple(outs)
```

```python
import functools

import jax
import jax.numpy as jnp
from jax import lax
from jax.experimental import pallas as pl
from jax.experimental.pallas import tpu as pltpu

F32 = jnp.float32
BF16 = jnp.bfloat16

HEADS = 4
HD = 128
GW = HEADS * HD
CHUNK = 128
ROPE_BASE = 10000.0
RMS_EPS = 1e-6
HN_EPS = 1e-5
SUBLANES = 8
LANES = 128
ROW_TILE = 512
IN_TILE = 1024
FF_BLOCK = 256
MIX_CHUNKS = 8
MLSTM_GROUP = 4
VMEM_LIMIT = 56 * 1024 * 1024


def _rms(x, w):
    return x * lax.rsqrt(jnp.mean(x * x, axis=-1, keepdims=True) + RMS_EPS) * w


def _rms_project(x, w_ref, projections, kb):
    inv = lax.rsqrt(jnp.mean(x * x, axis=-1, keepdims=True) + RMS_EPS)
    blocks, outs = [], [None] * len(projections)
    for k in range(0, x.shape[1], kb):
        xk = (x[:, k:k + kb] * inv * w_ref[:, k:k + kb]).astype(BF16)
        blocks.append(xk)
        for i, (rows, p_ref, lo, width) in enumerate(projections):
            part = _dot(xk[:rows], p_ref[k:k + kb, lo:lo + width])
            outs[i] = part if outs[i] is None else outs[i] + part
    return jnp.concatenate(blocks, axis=1), outs


def _log_sigmoid(x):
    return jnp.minimum(x, 0.0) - jnp.log1p(jnp.exp(-jnp.abs(x)))


def _sigmoid(x):
    return 1.0 / (1.0 + jnp.exp(-x))


def _gelu(x):
    return 0.5 * x * (1.0 + lax.erf(x * (0.5 ** 0.5)))


def _dot(a, b):
    return jnp.dot(a, b, preferred_element_type=F32)


def _dot_nt(a, b):
    return lax.dot_general(a, b, (((1,), (1,)), ((), ())), preferred_element_type=F32)


def _ext_rows(main, nxt, prv, tin, tiles_per_seq):
    nxt = jnp.where(tin == tiles_per_seq - 1, 0.0, nxt)
    prv = jnp.where(tin == 0, 0.0, prv)
    return jnp.concatenate([main, nxt, prv], axis=0)


def _conv3(ext, rows, w, b):
    n = ext.shape[0]
    up = pltpu.roll(ext, 1, axis=0)[:rows]
    dn = pltpu.roll(ext, n - 1, axis=0)[:rows]
    return up * w[0:1] + ext[:rows] * w[1:2] + dn * w[2:3] + b


def _chunk_stats(ig, lf):
    tm = ig.shape[0]
    ii = lax.broadcasted_iota(jnp.int32, (CHUNK, CHUNK), 0)
    jj = lax.broadcasted_iota(jnp.int32, (CHUNK, CHUNK), 1)
    tril = (ii >= jj).astype(F32)
    triu = (jj >= ii).astype(F32)
    hi = lax.Precision.HIGHEST
    a = jnp.concatenate([
        jnp.where(jj < HEADS,
                  jnp.dot(tril, lf[r:r + CHUNK], precision=hi, preferred_element_type=F32),
                  jnp.dot(triu, lf[r:r + CHUNK], precision=hi, preferred_element_type=F32))
        for r in range(0, tm, CHUNK)], axis=0)
    b = ig - a
    r = lax.broadcasted_iota(jnp.int32, (tm, LANES), 0) % CHUNK
    pre = suf = b
    s = 1
    while s < CHUNK:
        pre = jnp.maximum(pre, jnp.where(r >= s, pltpu.roll(pre, s, axis=0), -jnp.inf))
        suf = jnp.maximum(suf, jnp.where(r < CHUNK - s, pltpu.roll(suf, tm - s, axis=0), -jnp.inf))
        s *= 2
    cm = jnp.where(lax.broadcasted_iota(jnp.int32, (tm, LANES), 1) < HEADS, pre, suf)
    return a, b, cm


def _rope_table_kernel(inv_ref, cos_ref, sin_ref, cr_ref, sr_ref):
    tm = cos_ref.shape[0]

    @pl.when(pl.program_id(0) == 0)
    def _offsets():
        off = lax.broadcasted_iota(jnp.int32, (tm, HD), 0).astype(F32) * inv_ref[...]
        cr_ref[...] = jnp.cos(off)
        sr_ref[...] = jnp.sin(off)

    base = (pl.program_id(0) * tm).astype(F32) * inv_ref[...]
    cb, sb = jnp.cos(base), jnp.sin(base)
    cos_ref[...] = cb * cr_ref[...] - sb * sr_ref[...]
    sin = sb * cr_ref[...] + cb * sr_ref[...]
    lane = lax.broadcasted_iota(jnp.int32, (tm, HD), 1)
    sin_ref[...] = jnp.where(lane < HD // 2, -sin, sin)


def _in_proj_kernel(x_ref, xn_ref, xp_ref, cos_ref, sin_ref, nw_ref, w_ref, wg_ref, gb_ref, cw_ref, cb_ref,
                    qv_ref, kt_ref, og_ref, st_ref, br_ref, *, tiles_per_seq):
    tm = x_ref.shape[0]
    tin = pl.program_id(0) % tiles_per_seq
    x_ext = _ext_rows(x_ref[...], xn_ref[...], xp_ref[...], tin, tiles_per_seq)
    a_ext, (gate_proj,) = _rms_project(x_ext, nw_ref, [(tm, wg_ref, 0, wg_ref.shape[1])], 2 * LANES)
    a = a_ext[:tm]
    scale = HD ** -0.5

    PW = 2 * HD

    def rope(t, i):
        th = t[:, i * HD:(i + 1) * HD]
        return th * cos_ref[...] + pltpu.roll(th, HD // 2, axis=1) * sin_ref[...]

    def put_rq(t, lo):
        for i in range(PW // HD):
            qv_ref[:, lo + i * HD:lo + (i + 1) * HD] = (rope(t, i) * scale).astype(BF16)

    def put_keys(row_lo, k):
        kt = k.T.astype(BF16)
        for ch in range(tm // CHUNK):
            kt_ref[ch, row_lo:row_lo + HD, :] = kt[:, ch * CHUNK:(ch + 1) * CHUNK]

    def put_rk(t, lo):
        for i in range(PW // HD):
            put_keys(lo + i * HD, rope(t, i))

    def conv_silu(t, lo):
        c = _conv3(t, tm, cw_ref[:, lo:lo + PW], cb_ref[:, lo:lo + PW])
        return c * _sigmoid(c)

    def put_mq(t, lo):
        qv_ref[:, 2 * GW + lo:2 * GW + lo + PW] = (conv_silu(t, lo) * scale).astype(BF16)

    def put_mk(t, lo):
        c = conv_silu(t, GW + lo)
        for i in range(PW // HD):
            put_keys(GW + lo + i * HD, c[:, i * HD:(i + 1) * HD])

    def put_gates(t, lo):
        g = t + gb_ref[...]
        acum, b, cmax = _chunk_stats(g[:, :LANES], _log_sigmoid(g[:, LANES:]))
        lane = lax.broadcasted_iota(jnp.int32, acum.shape, 1)
        st_ref[...] = jnp.where(lane < 2 * HEADS, acum, pltpu.roll(cmax, 2 * HEADS, axis=1))
        for ch in range(tm // CHUNK):
            br_ref[ch * SUBLANES:(ch + 1) * SUBLANES, :] = b[ch * CHUNK:(ch + 1) * CHUNK].T[:SUBLANES]

    def put(ref, base):
        def store(t, lo):
            ref[:, base + lo:base + lo + PW] = t.astype(BF16)
        return store

    def pieces(lhs, w_ref, w_lo, consume):
        return [(lhs, w_ref, w_lo + lo, consume, lo) for lo in range(0, GW, PW)]

    rq, rk = pieces(a, w_ref, 0, put_rq), pieces(a, w_ref, GW, put_rk)
    rv, rg = pieces(a, w_ref, 2 * GW, put(qv_ref, GW)), pieces(a, w_ref, 3 * GW, put(og_ref, 0))
    mq, mk = pieces(a_ext, w_ref, 4 * GW, put_mq), pieces(a_ext, w_ref, 5 * GW, put_mk)
    mv, mo = pieces(a, w_ref, 6 * GW, put(qv_ref, 3 * GW)), pieces(a, w_ref, 7 * GW, put(og_ref, GW))
    gates = [(None, None, 0, put_gates, 0)]
    order = [(gates[0],), (mq[0], rv[0]), (rq[0], rv[1]), (mq[1], rg[0]), (rq[1], rg[1]), (mk[0], mv[0]),
             (rk[0], mv[1]), (mk[1], mo[0]), (rk[1], mo[1])]
    project = lambda stage: [_dot(piece[0], piece[1][:, piece[2]:piece[2] + PW]) for piece in stage]
    pending = [gate_proj]
    for k, stage in enumerate(order):
        ready = pending
        if k + 1 < len(order):
            pending = project(order[k + 1])
        for piece, t in zip(stage, ready):
            piece[3](t, piece[4])


def _mixer_kernel(lg_ref, qf_ref, ktf_ref, stf_ref, brf_ref, qb_ref, ktb_ref, stb_ref, brb_ref,
                  of_ref, ob_ref, dm_ref, qdf_ref, qdb_ref, kd_ref, cd_ref, sf_ref, sb_ref, c_ref, m_ref):
    L = CHUNK
    ii = lax.broadcasted_iota(jnp.int32, (L, L), 0)
    jj = lax.broadcasted_iota(jnp.int32, (L, L), 1)

    @pl.when(pl.program_id(1) == 0)
    def _init():
        lg = _log_sigmoid(lg_ref[...])
        lgf, lgb = lg[0:1], lg[1:2]
        row = lax.broadcasted_iota(jnp.int32, (L, GW), 0).astype(F32)
        qdf_ref[...] = jnp.exp((row + 1.0) * lgf)
        qdb_ref[...] = jnp.exp((L - row) * lgb)
        cd_ref[...] = jnp.exp(float(L) * lg)
        diff = (ii - jj).astype(F32)
        col = lax.broadcasted_iota(jnp.int32, (1, L), 1).astype(F32)
        for h in range(HEADS):
            sl = slice(h * HD, (h + 1) * HD)
            dm_ref[h] = jnp.where(diff >= 0, jnp.exp(diff * lgf[:, sl]), jnp.exp(-diff * lgb[:, sl]))
            kd_ref[h:h + 1, :] = jnp.exp((L - 1.0 - col) * lgf[:, sl])
            kd_ref[HEADS + h:HEADS + h + 1, :] = jnp.exp(col * lgb[:, sl])
        sf_ref[...] = jnp.zeros_like(sf_ref)
        sb_ref[...] = jnp.zeros_like(sb_ref)
        c_ref[...] = jnp.zeros_like(c_ref)
        m_ref[...] = jnp.zeros_like(m_ref)

    n_sub = qf_ref.shape[0] // L
    H = range(HEADS)
    C = range(2 * HEADS)
    sls = [slice(h * HD, (h + 1) * HD) for h in H]
    tril = ii >= jj
    triu = jj >= ii
    is_f = jj % (2 * HEADS) < HEADS
    ones_blk = jnp.ones((L, HD), BF16)

    def chunk(i, carry):
        i_b = n_sub - 1 - i
        rows_f = pl.ds(pl.multiple_of(i * L, L), L)
        rows_b = pl.ds(pl.multiple_of(i_b * L, L), L)
        part = lambda ref, rows, g, h: ref[rows, g * GW + h * HD:g * GW + (h + 1) * HD]

        def retention():
            qf, vf = (lambda h: part(qf_ref, rows_f, 0, h)), (lambda h: part(qf_ref, rows_f, 1, h))
            qb, vb = (lambda h: part(qb_ref, rows_b, 0, h)), (lambda h: part(qb_ref, rows_b, 1, h))
            ktf = lambda h: ktf_ref[i, sls[h], :]
            ktb = lambda h: ktb_ref[i_b, sls[h], :]
            qkc = [_dot(qf(h), jnp.concatenate([ktf(h), sf_ref[h].astype(BF16)], axis=1)) for h in H]
            p = [(qkc[h][:, :L] * dm_ref[h]).astype(BF16) for h in H]
            cross_b = [_dot(qb(h), sb_ref[h].astype(BF16)) for h in H]
            kdf = [(ktf(h).astype(F32) * kd_ref[h:h + 1, :]).astype(BF16) for h in H]
            pv = [_dot(jnp.concatenate([p[h], kdf[h]], axis=0), vf(h)) for h in H]
            for h in H:
                of_ref[rows_f, sls[h]] = (pv[h][:L] + qkc[h][:, L:] * qdf_ref[:, sls[h]]).astype(of_ref.dtype)
                ob_ref[rows_b, sls[h]] = (cross_b[h] * qdb_ref[:, sls[h]]).astype(ob_ref.dtype)
            upd_b = [_dot((ktb(h).astype(F32) * kd_ref[HEADS + h:HEADS + h + 1, :]).astype(BF16), vb(h)) for h in H]
            for h in H:
                sf_ref[h] = cd_ref[0:1, sls[h]] * sf_ref[h] + pv[h][L:]
                sb_ref[h] = cd_ref[1:2, sls[h]] * sb_ref[h] + upd_b[h]

        st = jnp.where(is_f, stf_ref[rows_f, :], stb_ref[rows_b, :])
        a_col = lambda c: st[:, c:c + 1]
        cm_col = lambda c: st[:, 2 * HEADS + c:2 * HEADS + c + 1]
        b_rows = jnp.where(lax.broadcasted_iota(jnp.int32, (SUBLANES, L), 0) < HEADS,
                           brf_ref[pl.ds(pl.multiple_of(i * SUBLANES, SUBLANES), SUBLANES), :],
                           brb_ref[pl.ds(pl.multiple_of(i_b * SUBLANES, SUBLANES), SUBLANES), :])

        def mlstm_group(cs):
            back = {c: c >= HEADS for c in cs}
            hd = {c: c % HEADS for c in cs}
            last = {c: 0 if back[c] else L - 1 for c in cs}
            q = lambda c: part(qb_ref, rows_b, 2, hd[c]) if back[c] else part(qf_ref, rows_f, 2, hd[c])
            v = lambda c: part(qb_ref, rows_b, 3, hd[c]) if back[c] else part(qf_ref, rows_f, 3, hd[c])
            kt = lambda c: (ktb_ref[i_b, GW + hd[c] * HD:GW + (hd[c] + 1) * HD, :] if back[c] else
                            ktf_ref[i, GW + hd[c] * HD:GW + (hd[c] + 1) * HD, :])
            v1 = lambda c: jnp.concatenate([v(c), ones_blk], axis=1)
            m_old = {c: m_ref[c:c + 1, 0:1] for c in cs}
            mu = {c: jnp.maximum(jnp.broadcast_to(cm_col(c), (L, L)), m_old[c]) for c in cs}
            e = {c: jnp.exp(jnp.where(triu if back[c] else tril, b_rows[c:c + 1, :] - mu[c], -jnp.inf)) for c in cs}
            qk = {c: _dot(q(c), kt(c)) for c in cs}
            s = {c: (qk[c] * e[c]).astype(BF16) for c in cs}
            qc = {c: _dot(q(c), c_ref[c].astype(BF16)) for c in cs}
            mu_last = {c: mu[c][last[c]:last[c] + 1, 0:1] for c in cs}
            kw = {c: (kt(c).astype(F32) * jnp.exp(b_rows[c:c + 1, :] - mu_last[c])).astype(BF16) for c in cs}
            sv = {c: _dot(jnp.concatenate([s[c], kw[c]], axis=0), v1(c)) for c in cs}
            for c in cs:
                w = jnp.exp(m_old[c] - mu[c])
                floor = jnp.exp(-(jnp.broadcast_to(a_col(c), (L, L)) + mu[c]))
                den = jnp.maximum(jnp.abs(w * qc[c][:, HD:] + sv[c][:L, HD:]), floor)
                dst, rows = (ob_ref, rows_b) if back[c] else (of_ref, rows_f)
                h_c = (w * qc[c][:, :HD] + sv[c][:L, :HD]) / den
                dst[rows, GW + hd[c] * HD:GW + (hd[c] + 1) * HD] = h_c.astype(dst.dtype)
            for c in cs:
                c_ref[c] = jnp.exp(m_old[c] - mu_last[c]) * c_ref[c] + sv[c][L:]
                m_ref[c:c + 1, :] = jnp.broadcast_to(a_col(c)[last[c]:last[c] + 1] + mu_last[c], (1, LANES))

        retention()
        for g in range(0, 2 * HEADS, MLSTM_GROUP):
            mlstm_group(range(g, g + MLSTM_GROUP))
        return carry

    lax.fori_loop(0, n_sub, chunk, 0)


def _mix_out_pieces(rows, tin, tiles_per_seq, gn_ref, wo_ref, nfw_ref, emit):
    of_ref, ofn_ref, ofp_ref, ob_ref, obn_ref, obp_ref, og_ref, ogn_ref, ogp_ref, x_ref, xn_ref, xp_ref = rows
    tm = x_ref.shape[0]
    n_ext = tm + 2 * SUBLANES
    v = {}

    def ext(m, nx, pv):
        return jnp.concatenate([m[...].astype(F32), nx[...].astype(F32)[:SUBLANES],
                                pv[...].astype(F32)[pv.shape[0] - SUBLANES:]], axis=0)

    def load():
        v["y"] = ext(of_ref, ofn_ref, ofp_ref) + ext(ob_ref, obn_ref, obp_ref)
        v["z"] = ext(og_ref, ogn_ref, ogp_ref)
        v["h"] = ext(x_ref, xn_ref, xp_ref)
        v["parts"] = []

    def head(g):
        sl = slice(g * HD, (g + 1) * HD)
        yg = v["y"][:, sl]
        d = yg - jnp.mean(yg, axis=-1, keepdims=True)
        n = d * lax.rsqrt(jnp.mean(d * d, axis=-1, keepdims=True) + HN_EPS) * gn_ref[:, sl]
        zg = v["z"][:, sl]
        gate = zg * _sigmoid(zg) if g < HEADS else _sigmoid(zg)
        v["parts"].append((n * gate).astype(BF16))
        if g % 2 == 1:
            pair = jnp.concatenate(v["parts"][-2:], axis=1)
            v["h"] = v["h"] + _dot(pair, wo_ref[(g - 1) * HD:(g + 1) * HD, :])

    def finish():
        r = lax.broadcasted_iota(jnp.int32, (n_ext, 1), 0)
        pad = ((r >= tm) & (r < tm + SUBLANES) & (tin == tiles_per_seq - 1)) | ((r >= tm + SUBLANES) & (tin == 0))
        emit(jnp.where(pad, 0.0, _rms(v["h"], nfw_ref[...])).astype(BF16), v["h"][:tm])

    return [load] + [functools.partial(head, g) for g in range(2 * HEADS)] + [finish]


def _out_ffn_kernel(*refs, tiles_per_seq):
    rows = refs[:12]
    (p_ref, gn_ref, wo_ref, nfw_ref, wg_ref, wu_ref, cw_ref, cb_ref, wd_ref, pnw_ref, pwg_ref, pgb_ref, pwp_ref, fnw_ref,
     y_ref, act_ref) = refs[12:]
    tm = p_ref.shape[0]
    stage = {}
    tin = pl.program_id(0) % tiles_per_seq
    for piece in _mix_out_pieces(rows, tin, tiles_per_seq, gn_ref, wo_ref, nfw_ref,
                                 lambda f_ext, h: stage.update(f_ext=f_ext, h=h)):
        piece()
    f_ext = stage["f_ext"]
    f = f_ext[:tm]

    fb = FF_BLOCK
    nblk = wg_ref.shape[1] // fb
    blk = lambda ref, j: ref[:, j * fb:(j + 1) * fb]
    proj = lambda j: (_dot(f_ext, blk(wg_ref, j)), _dot(f, blk(wu_ref, j)))
    nxt = proj(0)
    for j in range(nblk):
        u_ext, up = nxt
        if j + 1 < nblk:
            nxt = proj(j + 1)
        act_ref[:, j * fb:(j + 1) * fb] = (_gelu(_conv3(u_ext, tm, blk(cw_ref, j), blk(cb_ref, j))) * up).astype(BF16)
    h = stage["h"] + _dot(act_ref[...], wd_ref[...])
    halves = [slice(0, tm // 2), slice(tm // 2, tm)]
    gate_pre = [_dot(_rms(h[r], pnw_ref[...]).astype(BF16), pwg_ref[...]) for r in halves]
    ple = [_dot(p_ref[r, :].astype(BF16), pwp_ref[...]) for r in halves]
    for r, g, e in zip(halves, gate_pre, ple):
        y_ref[r, :] = _rms(h[r] + e * _sigmoid(g + pgb_ref[...]), fnw_ref[...])


def _const_spec(shape):
    nd = len(shape)
    return pl.BlockSpec(shape, lambda *_: (0,) * nd, pipeline_mode=pl.Buffered(1))


def _row_specs(tm, width, n_rows, halo=SUBLANES):
    per = tm // halo
    last = n_rows // halo - 1
    return [
        pl.BlockSpec((tm, width), lambda i: (i, 0)),
        pl.BlockSpec((halo, width), lambda i: (jnp.minimum((i + 1) * per, last), 0)),
        pl.BlockSpec((halo, width), lambda i: (jnp.maximum(i * per - 1, 0), 0)),
    ]


def _params(sem):
    return pltpu.CompilerParams(dimension_semantics=sem, vmem_limit_bytes=VMEM_LIMIT)


def _layer(x, p, w):
    bsz, seq, d = x.shape
    n = bsz * seq
    tm = ROW_TILE
    assert seq % tm == 0 and seq % CHUNK == 0
    tiles_per_seq = seq // tm
    x2 = x.reshape(n, d)
    row = lambda width: pl.BlockSpec((tm, width), lambda i: (i, 0))

    rows_per_chunk = SUBLANES
    ti = IN_TILE
    assert seq % ti == 0
    row_i = lambda width: pl.BlockSpec((ti, width), lambda i: (i, 0))
    pos_tile = pl.BlockSpec((ti, HD), lambda i: (i % (seq // ti), 0))
    qv, kt, og, stats, brow = pl.pallas_call(
        functools.partial(_in_proj_kernel, tiles_per_seq=seq // ti),
        grid=(n // ti,),
        in_specs=_row_specs(ti, d, n) + [pos_tile, pos_tile] + [_const_spec(a.shape) for a in w["in_proj"]],
        out_specs=[row_i(4 * GW), pl.BlockSpec((ti // CHUNK, 2 * GW, CHUNK), lambda i: (i, 0, 0)), row_i(2 * GW),
                   row_i(LANES), pl.BlockSpec((ti // CHUNK * rows_per_chunk, LANES), lambda i: (i, 0))],
        out_shape=[jax.ShapeDtypeStruct((n, 4 * GW), BF16), jax.ShapeDtypeStruct((n // CHUNK, 2 * GW, CHUNK), BF16),
                   jax.ShapeDtypeStruct((n, 2 * GW), BF16), jax.ShapeDtypeStruct((n, LANES), F32),
                   jax.ShapeDtypeStruct((n // CHUNK * rows_per_chunk, LANES), F32)],
        compiler_params=_params(("parallel",)),
        name="in_proj",
    )(x2, x2, x2, *w["rope"], *w["in_proj"])

    cb = MIX_CHUNKS
    assert seq % (cb * CHUNK) == 0
    nb = seq // (cb * CHUNK)
    fwd_i = lambda b, t: b * nb + t
    bwd_i = lambda b, t: b * nb + nb - 1 - t

    def block_specs(idx):
        return [pl.BlockSpec((cb * CHUNK, 4 * GW), lambda b, t: (idx(b, t), 0)),
                pl.BlockSpec((cb, 2 * GW, CHUNK), lambda b, t: (idx(b, t), 0, 0)),
                pl.BlockSpec((cb * CHUNK, LANES), lambda b, t: (idx(b, t), 0)),
                pl.BlockSpec((cb * rows_per_chunk, LANES), lambda b, t: (idx(b, t), 0))]

    out_f, out_b = pl.pallas_call(
        _mixer_kernel,
        grid=(bsz, nb),
        in_specs=[_const_spec(w["ret_logit"].shape)] + block_specs(fwd_i) + block_specs(bwd_i),
        out_specs=[pl.BlockSpec((cb * CHUNK, 2 * GW), lambda b, t: (fwd_i(b, t), 0)),
                   pl.BlockSpec((cb * CHUNK, 2 * GW), lambda b, t: (bwd_i(b, t), 0))],
        out_shape=[jax.ShapeDtypeStruct((n, 2 * GW), BF16)] * 2,
        scratch_shapes=[
            pltpu.VMEM((HEADS, CHUNK, CHUNK), F32),
            pltpu.VMEM((CHUNK, GW), F32), pltpu.VMEM((CHUNK, GW), F32),
            pltpu.VMEM((2 * HEADS, CHUNK), F32),
            pltpu.VMEM((2, GW), F32),
            pltpu.VMEM((HEADS, HD, HD), F32), pltpu.VMEM((HEADS, HD, HD), F32),
            pltpu.VMEM((2 * HEADS, HD, 2 * HD), F32),
            pltpu.VMEM((2 * HEADS, LANES), F32),
        ],
        compiler_params=_params(("arbitrary", "arbitrary")),
        name="mixers",
    )(w["ret_logit"], *([qv, kt, stats, brow] * 2))

    p2 = p.reshape(n, p.shape[-1])
    bf16_rows = 2 * SUBLANES
    mix_rows = _row_specs(tm, d, n, bf16_rows) * 3 + _row_specs(tm, d, n)
    y = pl.pallas_call(
        functools.partial(_out_ffn_kernel, tiles_per_seq=tiles_per_seq),
        grid=(n // tm,),
        in_specs=mix_rows + [row(p2.shape[-1])] + [_const_spec(a.shape) for a in (*w["mix_out"], *w["ffn"])],
        out_specs=row(d),
        out_shape=jax.ShapeDtypeStruct((n, d), F32),
        scratch_shapes=[pltpu.VMEM((tm, w["d_ff"]), BF16)],
        compiler_params=_params(("parallel",)),
        name="out_ffn",
    )(*((out_f,) * 3 + (out_b,) * 3 + (og,) * 3 + (x2,) * 3), p2, *w["mix_out"], *w["ffn"])
    return y.reshape(bsz, seq, d)


def kernel(x_prompt, x_sample, p_prompt, p_sample, norm_mix_w, w_in, mlstm_conv_w, mlstm_conv_b, mlstm_gate_b, ret_decay_logit, ret_gn_w, mlstm_gn_w, w_out, norm_ffn_w, ffn_w_gate, ffn_w_up, ffn_conv_w, ffn_conv_b, ffn_w_down, ple_w_proj, ple_norm_w, ple_w_gate, ple_gate_b, final_norm_w):
    depth = w_in.shape[0]
    d = x_prompt.shape[-1]
    d_ff = ffn_w_gate.shape[-1]
    assert d_ff % FF_BLOCK == 0
    rowv = lambda a: a.reshape(1, -1).astype(F32)
    inv = ROPE_BASE ** (-jnp.arange(0, HD, 2, dtype=F32) / HD)
    inv = jnp.concatenate([inv, inv]).reshape(1, HD)
    max_seq = max(x_prompt.shape[1], x_sample.shape[1])
    assert max_seq % ROW_TILE == 0
    rope = pl.pallas_call(
        _rope_table_kernel,
        grid=(max_seq // ROW_TILE,),
        in_specs=[_const_spec(inv.shape)],
        out_specs=[pl.BlockSpec((ROW_TILE, HD), lambda i: (i, 0))] * 2,
        out_shape=[jax.ShapeDtypeStruct((max_seq, HD), F32)] * 2,
        scratch_shapes=[pltpu.VMEM((ROW_TILE, HD), F32)] * 2,
        compiler_params=_params(("arbitrary",)),
        name="rope_tables",
    )(inv)

    assert depth == 1
    outs = [x_prompt, x_sample]
    for l in range(depth):
        wi = w_in[l].astype(BF16)
        n_gate = 2 * HEADS
        wgate = jnp.zeros((d, 2 * LANES), BF16)
        wgate = wgate.at[:, 0:n_gate].set(wi[:, 8 * GW:8 * GW + n_gate])
        wgate = wgate.at[:, LANES:LANES + n_gate].set(wi[:, 8 * GW + n_gate:8 * GW + 2 * n_gate])
        gb = mlstm_gate_b[l].astype(F32).reshape(-1)
        gbias = jnp.zeros((1, 2 * LANES), F32)
        gbias = gbias.at[0, 0:n_gate].set(gb[:n_gate]).at[0, LANES:LANES + n_gate].set(gb[n_gate:])
        w = {
            "rope": rope,
            "in_proj": [rowv(norm_mix_w[l]), wi, wgate, gbias, mlstm_conv_w[l].astype(F32), rowv(mlstm_conv_b[l])],
            "ret_logit": jnp.repeat(ret_decay_logit[l].astype(F32), HD, axis=1),
            "d_ff": d_ff,
            "mix_out": [jnp.concatenate([rowv(ret_gn_w[l]), rowv(mlstm_gn_w[l])], axis=1), w_out[l].astype(BF16),
                        rowv(norm_ffn_w[l])],
            "ffn": [ffn_w_gate[l].astype(BF16), ffn_w_up[l].astype(BF16), ffn_conv_w[l].astype(F32),
                    rowv(ffn_conv_b[l]), ffn_w_down[l].astype(BF16), rowv(ple_norm_w[l]), ple_w_gate[l].astype(BF16),
                    rowv(ple_gate_b[l]), ple_w_proj[l].astype(BF16), rowv(final_norm_w)],
        }
        outs = [_layer(h, p[l], w) for h, p in zip(outs, (p_prompt, p_sample))]
    return tuple(outs)
```

```python
import functools

import jax
import jax.numpy as jnp
from jax import lax
from jax.experimental import pallas as pl
from jax.experimental.pallas import tpu as pltpu

F32 = jnp.float32
BF16 = jnp.bfloat16

HEADS = 4
HD = 128
GW = HEADS * HD
CHUNK = 128
ROPE_BASE = 10000.0
RMS_EPS = 1e-6
HN_EPS = 1e-5
SUBLANES = 8
LANES = 128
ROW_TILE = 512
IN_TILE = 1024
FF_BLOCK = 256
MIX_CHUNKS = 8
MLSTM_GROUP = 4
VMEM_LIMIT = 56 * 1024 * 1024


def _rms(x, w):
    return x * lax.rsqrt(jnp.mean(x * x, axis=-1, keepdims=True) + RMS_EPS) * w


def _rms_project(x, w_ref, projections, kb):
    inv = lax.rsqrt(jnp.mean(x * x, axis=-1, keepdims=True) + RMS_EPS)
    blocks, outs = [], [None] * len(projections)
    for k in range(0, x.shape[1], kb):
        xk = (x[:, k:k + kb] * inv * w_ref[:, k:k + kb]).astype(BF16)
        blocks.append(xk)
        for i, (rows, p_ref, lo, width) in enumerate(projections):
            part = _dot(xk[:rows], p_ref[k:k + kb, lo:lo + width])
            outs[i] = part if outs[i] is None else outs[i] + part
    return jnp.concatenate(blocks, axis=1), outs


def _log_sigmoid(x):
    return jnp.minimum(x, 0.0) - jnp.log1p(jnp.exp(-jnp.abs(x)))


def _sigmoid(x):
    return 1.0 / (1.0 + jnp.exp(-x))


def _gelu(x):
    return 0.5 * x * (1.0 + lax.erf(x * (0.5 ** 0.5)))


def _dot(a, b):
    return jnp.dot(a, b, preferred_element_type=F32)


def _dot_nt(a, b):
    return lax.dot_general(a, b, (((1,), (1,)), ((), ())), preferred_element_type=F32)


def _ext_rows(main, nxt, prv, tin, tiles_per_seq):
    nxt = jnp.where(tin == tiles_per_seq - 1, 0.0, nxt)
    prv = jnp.where(tin == 0, 0.0, prv)
    return jnp.concatenate([main, nxt, prv], axis=0)


def _conv3(ext, rows, w, b):
    n = ext.shape[0]
    up = pltpu.roll(ext, 1, axis=0)[:rows]
    dn = pltpu.roll(ext, n - 1, axis=0)[:rows]
    return up * w[0:1] + ext[:rows] * w[1:2] + dn * w[2:3] + b


def _chunk_stats(ig, lf):
    tm = ig.shape[0]
    ii = lax.broadcasted_iota(jnp.int32, (CHUNK, CHUNK), 0)
    jj = lax.broadcasted_iota(jnp.int32, (CHUNK, CHUNK), 1)
    tril = (ii >= jj).astype(F32)
    triu = (jj >= ii).astype(F32)
    hi = lax.Precision.HIGHEST
    a = jnp.concatenate([
        jnp.where(jj < HEADS,
                  jnp.dot(tril, lf[r:r + CHUNK], precision=hi, preferred_element_type=F32),
                  jnp.dot(triu, lf[r:r + CHUNK], precision=hi, preferred_element_type=F32))
        for r in range(0, tm, CHUNK)], axis=0)
    b = ig - a
    r = lax.broadcasted_iota(jnp.int32, (tm, LANES), 0) % CHUNK
    pre = suf = b
    s = 1
    while s < CHUNK:
        pre = jnp.maximum(pre, jnp.where(r >= s, pltpu.roll(pre, s, axis=0), -jnp.inf))
        suf = jnp.maximum(suf, jnp.where(r < CHUNK - s, pltpu.roll(suf, tm - s, axis=0), -jnp.inf))
        s *= 2
    cm = jnp.where(lax.broadcasted_iota(jnp.int32, (tm, LANES), 1) < HEADS, pre, suf)
    return a, b, cm


def _rope_table_kernel(inv_ref, cos_ref, sin_ref, cr_ref, sr_ref):
    tm = cos_ref.shape[0]

    @pl.when(pl.program_id(0) == 0)
    def _offsets():
        off = lax.broadcasted_iota(jnp.int32, (tm, HD), 0).astype(F32) * inv_ref[...]
        cr_ref[...] = jnp.cos(off)
        sr_ref[...] = jnp.sin(off)

    base = (pl.program_id(0) * tm).astype(F32) * inv_ref[...]
    cb, sb = jnp.cos(base), jnp.sin(base)
    cos_ref[...] = cb * cr_ref[...] - sb * sr_ref[...]
    sin = sb * cr_ref[...] + cb * sr_ref[...]
    lane = lax.broadcasted_iota(jnp.int32, (tm, HD), 1)
    sin_ref[...] = jnp.where(lane < HD // 2, -sin, sin)


def _in_proj_kernel(x_ref, xn_ref, xp_ref, cos_ref, sin_ref, nw_ref, w_ref, wg_ref, gb_ref, cw_ref, cb_ref,
                    qv_ref, kt_ref, og_ref, st_ref, br_ref, *, tiles_per_seq):
    tm = x_ref.shape[0]
    tin = pl.program_id(0) % tiles_per_seq
    x_ext = _ext_rows(x_ref[...], xn_ref[...], xp_ref[...], tin, tiles_per_seq)
    a_ext, (gate_proj,) = _rms_project(x_ext, nw_ref, [(tm, wg_ref, 0, wg_ref.shape[1])], 2 * LANES)
    a = a_ext[:tm]
    scale = HD ** -0.5

    PW = 2 * HD

    def rope(t, i):
        th = t[:, i * HD:(i + 1) * HD]
        return th * cos_ref[...] + pltpu.roll(th, HD // 2, axis=1) * sin_ref[...]

    def put_rq(t, lo):
        for i in range(PW // HD):
            qv_ref[:, lo + i * HD:lo + (i + 1) * HD] = (rope(t, i) * scale).astype(BF16)

    def put_keys(row_lo, k):
        kt = k.T.astype(BF16)
        for ch in range(tm // CHUNK):
            kt_ref[ch, row_lo:row_lo + HD, :] = kt[:, ch * CHUNK:(ch + 1) * CHUNK]

    def put_rk(t, lo):
        for i in range(PW // HD):
            put_keys(lo + i * HD, rope(t, i))

    def conv_silu(t, lo):
        c = _conv3(t, tm, cw_ref[:, lo:lo + PW], cb_ref[:, lo:lo + PW])
        return c * _sigmoid(c)

    def put_mq(t, lo):
        qv_ref[:, 2 * GW + lo:2 * GW + lo + PW] = (conv_silu(t, lo) * scale).astype(BF16)

    def put_mk(t, lo):
        c = conv_silu(t, GW + lo)
        for i in range(PW // HD):
            put_keys(GW + lo + i * HD, c[:, i * HD:(i + 1) * HD])

    def put_gates(t, lo):
        g = t + gb_ref[...]
        acum, b, cmax = _chunk_stats(g[:, :LANES], _log_sigmoid(g[:, LANES:]))
        lane = lax.broadcasted_iota(jnp.int32, acum.shape, 1)
        st_ref[...] = jnp.where(lane < 2 * HEADS, acum, pltpu.roll(cmax, 2 * HEADS, axis=1))
        for ch in range(tm // CHUNK):
            br_ref[ch * SUBLANES:(ch + 1) * SUBLANES, :] = b[ch * CHUNK:(ch + 1) * CHUNK].T[:SUBLANES]

    def put(ref, base):
        def store(t, lo):
            ref[:, base + lo:base + lo + PW] = t.astype(BF16)
        return store

    def pieces(lhs, w_ref, w_lo, consume):
        return [(lhs, w_ref, w_lo + lo, consume, lo) for lo in range(0, GW, PW)]

    rq, rk = pieces(a, w_ref, 0, put_rq), pieces(a, w_ref, GW, put_rk)
    rv, rg = pieces(a, w_ref, 2 * GW, put(qv_ref, GW)), pieces(a, w_ref, 3 * GW, put(og_ref, 0))
    mq, mk = pieces(a_ext, w_ref, 4 * GW, put_mq), pieces(a_ext, w_ref, 5 * GW, put_mk)
    mv, mo = pieces(a, w_ref, 6 * GW, put(qv_ref, 3 * GW)), pieces(a, w_ref, 7 * GW, put(og_ref, GW))
    gates = [(None, None, 0, put_gates, 0)]
    order = [(gates[0],), (mq[0], rv[0]), (rq[0], rv[1]), (mq[1], rg[0]), (rq[1], rg[1]), (mk[0], mv[0]),
             (rk[0], mv[1]), (mk[1], rk[1]), (mo[0], mo[1])]
    project = lambda stage: [_dot(piece[0], piece[1][:, piece[2]:piece[2] + PW]) for piece in stage]
    pending = [gate_proj]
    for k, stage in enumerate(order):
        ready = pending
        if k + 1 < len(order):
            pending = project(order[k + 1])
        for piece, t in zip(stage, ready):
            piece[3](t, piece[4])


def _mixer_kernel(lg_ref, qf_ref, ktf_ref, stf_ref, brf_ref, qb_ref, ktb_ref, stb_ref, brb_ref,
                  of_ref, ob_ref, dm_ref, qdf_ref, qdb_ref, kd_ref, cd_ref, sf_ref, sb_ref, c_ref, m_ref):
    L = CHUNK
    ii = lax.broadcasted_iota(jnp.int32, (L, L), 0)
    jj = lax.broadcasted_iota(jnp.int32, (L, L), 1)

    @pl.when(pl.program_id(1) == 0)
    def _init():
        lg = _log_sigmoid(lg_ref[...])
        lgf, lgb = lg[0:1], lg[1:2]
        row = lax.broadcasted_iota(jnp.int32, (L, GW), 0).astype(F32)
        qdf_ref[...] = jnp.exp((row + 1.0) * lgf)
        qdb_ref[...] = jnp.exp((L - row) * lgb)
        cd_ref[...] = jnp.exp(float(L) * lg)
        diff = (ii - jj).astype(F32)
        col = lax.broadcasted_iota(jnp.int32, (1, L), 1).astype(F32)
        for h in range(HEADS):
            sl = slice(h * HD, (h + 1) * HD)
            dm_ref[h] = jnp.where(diff >= 0, jnp.exp(diff * lgf[:, sl]), jnp.exp(-diff * lgb[:, sl]))
            kd_ref[h:h + 1, :] = jnp.exp((L - 1.0 - col) * lgf[:, sl])
            kd_ref[HEADS + h:HEADS + h + 1, :] = jnp.exp(col * lgb[:, sl])
        sf_ref[...] = jnp.zeros_like(sf_ref)
        sb_ref[...] = jnp.zeros_like(sb_ref)
        c_ref[...] = jnp.zeros_like(c_ref)
        m_ref[...] = jnp.zeros_like(m_ref)

    n_sub = qf_ref.shape[0] // L
    H = range(HEADS)
    C = range(2 * HEADS)
    sls = [slice(h * HD, (h + 1) * HD) for h in H]
    tril = ii >= jj
    triu = jj >= ii
    is_f = jj % (2 * HEADS) < HEADS
    ones_blk = jnp.ones((L, HD), BF16)

    def chunk(i, carry):
        i_b = n_sub - 1 - i
        rows_f = pl.ds(pl.multiple_of(i * L, L), L)
        rows_b = pl.ds(pl.multiple_of(i_b * L, L), L)
        part = lambda ref, rows, g, h: ref[rows, g * GW + h * HD:g * GW + (h + 1) * HD]

        def retention():
            qf, vf = (lambda h: part(qf_ref, rows_f, 0, h)), (lambda h: part(qf_ref, rows_f, 1, h))
            qb, vb = (lambda h: part(qb_ref, rows_b, 0, h)), (lambda h: part(qb_ref, rows_b, 1, h))
            ktf = lambda h: ktf_ref[i, sls[h], :]
            ktb = lambda h: ktb_ref[i_b, sls[h], :]
            qkc = [_dot(qf(h), jnp.concatenate([ktf(h), sf_ref[h].astype(BF16)], axis=1)) for h in H]
            p = [(qkc[h][:, :L] * dm_ref[h]).astype(BF16) for h in H]
            cross_b = [_dot(qb(h), sb_ref[h].astype(BF16)) for h in H]
            kdf = [(ktf(h).astype(F32) * kd_ref[h:h + 1, :]).astype(BF16) for h in H]
            pv = [_dot(jnp.concatenate([p[h], kdf[h]], axis=0), vf(h)) for h in H]
            for h in H:
                of_ref[rows_f, sls[h]] = (pv[h][:L] + qkc[h][:, L:] * qdf_ref[:, sls[h]]).astype(of_ref.dtype)
                ob_ref[rows_b, sls[h]] = (cross_b[h] * qdb_ref[:, sls[h]]).astype(ob_ref.dtype)
            upd_b = [_dot((ktb(h).astype(F32) * kd_ref[HEADS + h:HEADS + h + 1, :]).astype(BF16), vb(h)) for h in H]
            for h in H:
                sf_ref[h] = cd_ref[0:1, sls[h]] * sf_ref[h] + pv[h][L:]
                sb_ref[h] = cd_ref[1:2, sls[h]] * sb_ref[h] + upd_b[h]

        st = jnp.where(is_f, stf_ref[rows_f, :], stb_ref[rows_b, :])
        a_col = lambda c: st[:, c:c + 1]
        cm_col = lambda c: st[:, 2 * HEADS + c:2 * HEADS + c + 1]
        b_rows = jnp.where(lax.broadcasted_iota(jnp.int32, (SUBLANES, L), 0) < HEADS,
                           brf_ref[pl.ds(pl.multiple_of(i * SUBLANES, SUBLANES), SUBLANES), :],
                           brb_ref[pl.ds(pl.multiple_of(i_b * SUBLANES, SUBLANES), SUBLANES), :])

        def mlstm_group(cs):
            back = {c: c >= HEADS for c in cs}
            hd = {c: c % HEADS for c in cs}
            last = {c: 0 if back[c] else L - 1 for c in cs}
            q = lambda c: part(qb_ref, rows_b, 2, hd[c]) if back[c] else part(qf_ref, rows_f, 2, hd[c])
            v = lambda c: part(qb_ref, rows_b, 3, hd[c]) if back[c] else part(qf_ref, rows_f, 3, hd[c])
            kt = lambda c: (ktb_ref[i_b, GW + hd[c] * HD:GW + (hd[c] + 1) * HD, :] if back[c] else
                            ktf_ref[i, GW + hd[c] * HD:GW + (hd[c] + 1) * HD, :])
            v1 = lambda c: jnp.concatenate([v(c), ones_blk], axis=1)
            m_old = {c: m_ref[c:c + 1, 0:1] for c in cs}
            mu = {c: jnp.maximum(jnp.broadcast_to(cm_col(c), (L, L)), m_old[c]) for c in cs}
            e = {c: jnp.exp(jnp.where(triu if back[c] else tril, b_rows[c:c + 1, :] - mu[c], -jnp.inf)) for c in cs}
            qk = {c: _dot(q(c), kt(c)) for c in cs}
            s = {c: (qk[c] * e[c]).astype(BF16) for c in cs}
            qc = {c: _dot(q(c), c_ref[c].astype(BF16)) for c in cs}
            mu_last = {c: mu[c][last[c]:last[c] + 1, 0:1] for c in cs}
            kw = {c: (kt(c).astype(F32) * jnp.exp(b_rows[c:c + 1, :] - mu_last[c])).astype(BF16) for c in cs}
            sv = {c: _dot(jnp.concatenate([s[c], kw[c]], axis=0), v1(c)) for c in cs}
            for c in cs:
                w = jnp.exp(m_old[c] - mu[c])
                floor = jnp.exp(-(jnp.broadcast_to(a_col(c), (L, L)) + mu[c]))
                den = jnp.maximum(jnp.abs(w * qc[c][:, HD:] + sv[c][:L, HD:]), floor)
                dst, rows = (ob_ref, rows_b) if back[c] else (of_ref, rows_f)
                h_c = (w * qc[c][:, :HD] + sv[c][:L, :HD]) / den
                dst[rows, GW + hd[c] * HD:GW + (hd[c] + 1) * HD] = h_c.astype(dst.dtype)
            for c in cs:
                c_ref[c] = jnp.exp(m_old[c] - mu_last[c]) * c_ref[c] + sv[c][L:]
                m_ref[c:c + 1, :] = jnp.broadcast_to(a_col(c)[last[c]:last[c] + 1] + mu_last[c], (1, LANES))

        retention()
        for g in range(0, 2 * HEADS, MLSTM_GROUP):
            mlstm_group(range(g, g + MLSTM_GROUP))
        return carry

    lax.fori_loop(0, n_sub, chunk, 0)


def _mix_out_pieces(rows, tin, tiles_per_seq, gn_ref, wo_ref, nfw_ref, emit):
    of_ref, ofn_ref, ofp_ref, ob_ref, obn_ref, obp_ref, og_ref, ogn_ref, ogp_ref, x_ref, xn_ref, xp_ref = rows
    tm = x_ref.shape[0]
    n_ext = tm + 2 * SUBLANES
    v = {}

    def ext(m, nx, pv):
        return jnp.concatenate([m[...].astype(F32), nx[...].astype(F32)[:SUBLANES],
                                pv[...].astype(F32)[pv.shape[0] - SUBLANES:]], axis=0)

    def load():
        v["y"] = ext(of_ref, ofn_ref, ofp_ref) + ext(ob_ref, obn_ref, obp_ref)
        v["z"] = ext(og_ref, ogn_ref, ogp_ref)
        v["h"] = ext(x_ref, xn_ref, xp_ref)
        v["parts"] = []

    def head(g):
        sl = slice(g * HD, (g + 1) * HD)
        yg = v["y"][:, sl]
        d = yg - jnp.mean(yg, axis=-1, keepdims=True)
        n = d * lax.rsqrt(jnp.mean(d * d, axis=-1, keepdims=True) + HN_EPS) * gn_ref[:, sl]
        zg = v["z"][:, sl]
        gate = zg * _sigmoid(zg) if g < HEADS else _sigmoid(zg)
        v["parts"].append((n * gate).astype(BF16))
        if g % 2 == 1:
            pair = jnp.concatenate(v["parts"][-2:], axis=1)
            v["h"] = v["h"] + _dot(pair, wo_ref[(g - 1) * HD:(g + 1) * HD, :])

    def finish():
        r = lax.broadcasted_iota(jnp.int32, (n_ext, 1), 0)
        pad = ((r >= tm) & (r < tm + SUBLANES) & (tin == tiles_per_seq - 1)) | ((r >= tm + SUBLANES) & (tin == 0))
        emit(jnp.where(pad, 0.0, _rms(v["h"], nfw_ref[...])).astype(BF16), v["h"][:tm])

    return [load] + [functools.partial(head, g) for g in range(2 * HEADS)] + [finish]


def _out_ffn_kernel(*refs, tiles_per_seq):
    rows = refs[:12]
    (p_ref, gn_ref, wo_ref, nfw_ref, wg_ref, wu_ref, cw_ref, cb_ref, wd_ref, pnw_ref, pwg_ref, pgb_ref, pwp_ref, fnw_ref,
     y_ref, act_ref) = refs[12:]
    tm = p_ref.shape[0]
    stage = {}
    tin = pl.program_id(0) % tiles_per_seq
    for piece in _mix_out_pieces(rows, tin, tiles_per_seq, gn_ref, wo_ref, nfw_ref,
                                 lambda f_ext, h: stage.update(f_ext=f_ext, h=h)):
        piece()
    f_ext = stage["f_ext"]
    f = f_ext[:tm]

    fb = FF_BLOCK
    nblk = wg_ref.shape[1] // fb
    blk = lambda ref, j: ref[:, j * fb:(j + 1) * fb]
    proj = lambda j: (_dot(f_ext, blk(wg_ref, j)), _dot(f, blk(wu_ref, j)))
    nxt = proj(0)
    for j in range(nblk):
        u_ext, up = nxt
        if j + 1 < nblk:
            nxt = proj(j + 1)
        act_ref[:, j * fb:(j + 1) * fb] = (_gelu(_conv3(u_ext, tm, blk(cw_ref, j), blk(cb_ref, j))) * up).astype(BF16)
    h = stage["h"] + _dot(act_ref[...], wd_ref[...])
    halves = [slice(0, tm // 2), slice(tm // 2, tm)]
    gate_pre = [_dot(_rms(h[r], pnw_ref[...]).astype(BF16), pwg_ref[...]) for r in halves]
    ple = [_dot(p_ref[r, :].astype(BF16), pwp_ref[...]) for r in halves]
    for r, g, e in zip(halves, gate_pre, ple):
        y_ref[r, :] = _rms(h[r] + e * _sigmoid(g + pgb_ref[...]), fnw_ref[...])


def _const_spec(shape):
    nd = len(shape)
    return pl.BlockSpec(shape, lambda *_: (0,) * nd, pipeline_mode=pl.Buffered(1))


def _row_specs(tm, width, n_rows, halo=SUBLANES):
    per = tm // halo
    last = n_rows // halo - 1
    return [
        pl.BlockSpec((tm, width), lambda i: (i, 0)),
        pl.BlockSpec((halo, width), lambda i: (jnp.minimum((i + 1) * per, last), 0)),
        pl.BlockSpec((halo, width), lambda i: (jnp.maximum(i * per - 1, 0), 0)),
    ]


def _params(sem):
    return pltpu.CompilerParams(dimension_semantics=sem, vmem_limit_bytes=VMEM_LIMIT)


def _layer(x, p, w):
    bsz, seq, d = x.shape
    n = bsz * seq
    tm = ROW_TILE
    assert seq % tm == 0 and seq % CHUNK == 0
    tiles_per_seq = seq // tm
    x2 = x.reshape(n, d)
    row = lambda width: pl.BlockSpec((tm, width), lambda i: (i, 0))

    rows_per_chunk = SUBLANES
    ti = IN_TILE
    assert seq % ti == 0
    row_i = lambda width: pl.BlockSpec((ti, width), lambda i: (i, 0))
    pos_tile = pl.BlockSpec((ti, HD), lambda i: (i % (seq // ti), 0))
    qv, kt, og, stats, brow = pl.pallas_call(
        functools.partial(_in_proj_kernel, tiles_per_seq=seq // ti),
        grid=(n // ti,),
        in_specs=_row_specs(ti, d, n) + [pos_tile, pos_tile] + [_const_spec(a.shape) for a in w["in_proj"]],
        out_specs=[row_i(4 * GW), pl.BlockSpec((ti // CHUNK, 2 * GW, CHUNK), lambda i: (i, 0, 0)), row_i(2 * GW),
                   row_i(LANES), pl.BlockSpec((ti // CHUNK * rows_per_chunk, LANES), lambda i: (i, 0))],
        out_shape=[jax.ShapeDtypeStruct((n, 4 * GW), BF16), jax.ShapeDtypeStruct((n // CHUNK, 2 * GW, CHUNK), BF16),
                   jax.ShapeDtypeStruct((n, 2 * GW), BF16), jax.ShapeDtypeStruct((n, LANES), F32),
                   jax.ShapeDtypeStruct((n // CHUNK * rows_per_chunk, LANES), F32)],
        compiler_params=_params(("parallel",)),
        name="in_proj",
    )(x2, x2, x2, *w["rope"], *w["in_proj"])

    cb = MIX_CHUNKS
    assert seq % (cb * CHUNK) == 0
    nb = seq // (cb * CHUNK)
    fwd_i = lambda b, t: b * nb + t
    bwd_i = lambda b, t: b * nb + nb - 1 - t

    def block_specs(idx):
        return [pl.BlockSpec((cb * CHUNK, 4 * GW), lambda b, t: (idx(b, t), 0)),
                pl.BlockSpec((cb, 2 * GW, CHUNK), lambda b, t: (idx(b, t), 0, 0)),
                pl.BlockSpec((cb * CHUNK, LANES), lambda b, t: (idx(b, t), 0)),
                pl.BlockSpec((cb * rows_per_chunk, LANES), lambda b, t: (idx(b, t), 0))]

    out_f, out_b = pl.pallas_call(
        _mixer_kernel,
        grid=(bsz, nb),
        in_specs=[_const_spec(w["ret_logit"].shape)] + block_specs(fwd_i) + block_specs(bwd_i),
        out_specs=[pl.BlockSpec((cb * CHUNK, 2 * GW), lambda b, t: (fwd_i(b, t), 0)),
                   pl.BlockSpec((cb * CHUNK, 2 * GW), lambda b, t: (bwd_i(b, t), 0))],
        out_shape=[jax.ShapeDtypeStruct((n, 2 * GW), BF16)] * 2,
        scratch_shapes=[
            pltpu.VMEM((HEADS, CHUNK, CHUNK), F32),
            pltpu.VMEM((CHUNK, GW), F32), pltpu.VMEM((CHUNK, GW), F32),
            pltpu.VMEM((2 * HEADS, CHUNK), F32),
            pltpu.VMEM((2, GW), F32),
            pltpu.VMEM((HEADS, HD, HD), F32), pltpu.VMEM((HEADS, HD, HD), F32),
            pltpu.VMEM((2 * HEADS, HD, 2 * HD), F32),
            pltpu.VMEM((2 * HEADS, LANES), F32),
        ],
        compiler_params=_params(("arbitrary", "arbitrary")),
        name="mixers",
    )(w["ret_logit"], *([qv, kt, stats, brow] * 2))

    p2 = p.reshape(n, p.shape[-1])
    bf16_rows = 2 * SUBLANES
    mix_rows = _row_specs(tm, d, n, bf16_rows) * 3 + _row_specs(tm, d, n)
    y = pl.pallas_call(
        functools.partial(_out_ffn_kernel, tiles_per_seq=tiles_per_seq),
        grid=(n // tm,),
        in_specs=mix_rows + [row(p2.shape[-1])] + [_const_spec(a.shape) for a in (*w["mix_out"], *w["ffn"])],
        out_specs=row(d),
        out_shape=jax.ShapeDtypeStruct((n, d), F32),
        scratch_shapes=[pltpu.VMEM((tm, w["d_ff"]), BF16)],
        compiler_params=_params(("parallel",)),
        name="out_ffn",
    )(*((out_f,) * 3 + (out_b,) * 3 + (og,) * 3 + (x2,) * 3), p2, *w["mix_out"], *w["ffn"])
    return y.reshape(bsz, seq, d)


def kernel(x_prompt, x_sample, p_prompt, p_sample, norm_mix_w, w_in, mlstm_conv_w, mlstm_conv_b, mlstm_gate_b, ret_decay_logit, ret_gn_w, mlstm_gn_w, w_out, norm_ffn_w, ffn_w_gate, ffn_w_up, ffn_conv_w, ffn_conv_b, ffn_w_down, ple_w_proj, ple_norm_w, ple_w_gate, ple_gate_b, final_norm_w):
    depth = w_in.shape[0]
    d = x_prompt.shape[-1]
    d_ff = ffn_w_gate.shape[-1]
    assert d_ff % FF_BLOCK == 0
    rowv = lambda a: a.reshape(1, -1).astype(F32)
    inv = ROPE_BASE ** (-jnp.arange(0, HD, 2, dtype=F32) / HD)
    inv = jnp.concatenate([inv, inv]).reshape(1, HD)
    max_seq = max(x_prompt.shape[1], x_sample.shape[1])
    assert max_seq % ROW_TILE == 0
    rope = pl.pallas_call(
        _rope_table_kernel,
        grid=(max_seq // ROW_TILE,),
        in_specs=[_const_spec(inv.shape)],
        out_specs=[pl.BlockSpec((ROW_TILE, HD), lambda i: (i, 0))] * 2,
        out_shape=[jax.ShapeDtypeStruct((max_seq, HD), F32)] * 2,
        scratch_shapes=[pltpu.VMEM((ROW_TILE, HD), F32)] * 2,
        compiler_params=_params(("arbitrary",)),
        name="rope_tables",
    )(inv)

    assert depth == 1
    outs = [x_prompt, x_sample]
    for l in range(depth):
        wi = w_in[l].astype(BF16)
        n_gate = 2 * HEADS
        wgate = jnp.zeros((d, 2 * LANES), BF16)
        wgate = wgate.at[:, 0:n_gate].set(wi[:, 8 * GW:8 * GW + n_gate])
        wgate = wgate.at[:, LANES:LANES + n_gate].set(wi[:, 8 * GW + n_gate:8 * GW + 2 * n_gate])
        gb = mlstm_gate_b[l].astype(F32).reshape(-1)
        gbias = jnp.zeros((1, 2 * LANES), F32)
        gbias = gbias.at[0, 0:n_gate].set(gb[:n_gate]).at[0, LANES:LANES + n_gate].set(gb[n_gate:])
        w = {
            "rope": rope,
            "in_proj": [rowv(norm_mix_w[l]), wi, wgate, gbias, mlstm_conv_w[l].astype(F32), rowv(mlstm_conv_b[l])],
            "ret_logit": jnp.repeat(ret_decay_logit[l].astype(F32), HD, axis=1),
            "d_ff": d_ff,
            "mix_out": [jnp.concatenate([rowv(ret_gn_w[l]), rowv(mlstm_gn_w[l])], axis=1), w_out[l].astype(BF16),
                        rowv(norm_ffn_w[l])],
            "ffn": [ffn_w_gate[l].astype(BF16), ffn_w_up[l].astype(BF16), ffn_conv_w[l].astype(F32),
                    rowv(ffn_conv_b[l]), ffn_w_down[l].astype(BF16), rowv(ple_norm_w[l]), ple_w_gate[l].astype(BF16),
                    rowv(ple_gate_b[l]), ple_w_proj[l].astype(BF16), rowv(final_norm_w)],
        }
        outs = [_layer(h, p[l], w) for h, p in zip(outs, (p_prompt, p_sample))]
    return tuple(outs)
```

```python
import functools

import jax
import jax.numpy as jnp
from jax import lax
from jax.experimental import pallas as pl
from jax.experimental.pallas import tpu as pltpu

F32 = jnp.float32
BF16 = jnp.bfloat16

HEADS = 4
HD = 128
GW = HEADS * HD
CHUNK = 128
ROPE_BASE = 10000.0
RMS_EPS = 1e-6
HN_EPS = 1e-5
SUBLANES = 8
LANES = 128
ROW_TILE = 512
IN_TILE = 1024
FF_BLOCK = 256
MIX_CHUNKS = 8
MLSTM_GROUP = 4
VMEM_LIMIT = 56 * 1024 * 1024


def _rms(x, w):
    return x * lax.rsqrt(jnp.mean(x * x, axis=-1, keepdims=True) + RMS_EPS) * w


def _rms_project(x, w_ref, projections, kb):
    inv = lax.rsqrt(jnp.mean(x * x, axis=-1, keepdims=True) + RMS_EPS)
    blocks, outs = [], [None] * len(projections)
    for k in range(0, x.shape[1], kb):
        xk = (x[:, k:k + kb] * inv * w_ref[:, k:k + kb]).astype(BF16)
        blocks.append(xk)
        for i, (rows, p_ref, lo, width) in enumerate(projections):
            part = _dot(xk[:rows], p_ref[k:k + kb, lo:lo + width])
            outs[i] = part if outs[i] is None else outs[i] + part
    return jnp.concatenate(blocks, axis=1), outs


def _log_sigmoid(x):
    return jnp.minimum(x, 0.0) - jnp.log1p(jnp.exp(-jnp.abs(x)))


def _sigmoid(x):
    return 1.0 / (1.0 + jnp.exp(-x))


def _gelu(x):
    return 0.5 * x * (1.0 + lax.erf(x * (0.5 ** 0.5)))


def _dot(a, b):
    return jnp.dot(a, b, preferred_element_type=F32)


def _dot_nt(a, b):
    return lax.dot_general(a, b, (((1,), (1,)), ((), ())), preferred_element_type=F32)


def _ext_rows(main, nxt, prv, tin, tiles_per_seq):
    nxt = jnp.where(tin == tiles_per_seq - 1, 0.0, nxt)
    prv = jnp.where(tin == 0, 0.0, prv)
    return jnp.concatenate([main, nxt, prv], axis=0)


def _conv3(ext, rows, w, b):
    n = ext.shape[0]
    up = pltpu.roll(ext, 1, axis=0)[:rows]
    dn = pltpu.roll(ext, n - 1, axis=0)[:rows]
    return up * w[0:1] + ext[:rows] * w[1:2] + dn * w[2:3] + b


def _chunk_stats(ig, lf):
    tm = ig.shape[0]
    ii = lax.broadcasted_iota(jnp.int32, (CHUNK, CHUNK), 0)
    jj = lax.broadcasted_iota(jnp.int32, (CHUNK, CHUNK), 1)
    tril = (ii >= jj).astype(F32)
    triu = (jj >= ii).astype(F32)
    hi = lax.Precision.HIGHEST
    a = jnp.concatenate([
        jnp.where(jj < HEADS,
                  jnp.dot(tril, lf[r:r + CHUNK], precision=hi, preferred_element_type=F32),
                  jnp.dot(triu, lf[r:r + CHUNK], precision=hi, preferred_element_type=F32))
        for r in range(0, tm, CHUNK)], axis=0)
    b = ig - a
    r = lax.broadcasted_iota(jnp.int32, (tm, LANES), 0) % CHUNK
    pre = suf = b
    s = 1
    while s < CHUNK:
        pre = jnp.maximum(pre, jnp.where(r >= s, pltpu.roll(pre, s, axis=0), -jnp.inf))
        suf = jnp.maximum(suf, jnp.where(r < CHUNK - s, pltpu.roll(suf, tm - s, axis=0), -jnp.inf))
        s *= 2
    cm = jnp.where(lax.broadcasted_iota(jnp.int32, (tm, LANES), 1) < HEADS, pre, suf)
    return a, b, cm


def _rope_table_kernel(inv_ref, cos_ref, sin_ref, cr_ref, sr_ref):
    tm = cos_ref.shape[0]

    @pl.when(pl.program_id(0) == 0)
    def _offsets():
        off = lax.broadcasted_iota(jnp.int32, (tm, HD), 0).astype(F32) * inv_ref[...]
        cr_ref[...] = jnp.cos(off)
        sr_ref[...] = jnp.sin(off)

    base = (pl.program_id(0) * tm).astype(F32) * inv_ref[...]
    cb, sb = jnp.cos(base), jnp.sin(base)
    cos_ref[...] = cb * cr_ref[...] - sb * sr_ref[...]
    sin = sb * cr_ref[...] + cb * sr_ref[...]
    lane = lax.broadcasted_iota(jnp.int32, (tm, HD), 1)
    sin_ref[...] = jnp.where(lane < HD // 2, -sin, sin)


def _in_proj_kernel(x_ref, xn_ref, xp_ref, cos_ref, sin_ref, nw_ref, w_ref, wg_ref, gb_ref, cw_ref, cb_ref,
                    qv_ref, kt_ref, og_ref, st_ref, br_ref, *, tiles_per_seq):
    tm = x_ref.shape[0]
    tin = pl.program_id(0) % tiles_per_seq
    x_ext = _ext_rows(x_ref[...], xn_ref[...], xp_ref[...], tin, tiles_per_seq)
    a_ext, (gate_proj,) = _rms_project(x_ext, nw_ref, [(tm, wg_ref, 0, wg_ref.shape[1])], 2 * LANES)
    a = a_ext[:tm]
    scale = HD ** -0.5

    PW = 2 * HD

    def rope(t, i):
        th = t[:, i * HD:(i + 1) * HD]
        return th * cos_ref[...] + pltpu.roll(th, HD // 2, axis=1) * sin_ref[...]

    def put_rq(t, lo):
        for i in range(PW // HD):
            qv_ref[:, lo + i * HD:lo + (i + 1) * HD] = (rope(t, i) * scale).astype(BF16)

    def put_keys(row_lo, k):
        kt = k.T.astype(BF16)
        for ch in range(tm // CHUNK):
            kt_ref[ch, row_lo:row_lo + HD, :] = kt[:, ch * CHUNK:(ch + 1) * CHUNK]

    def put_rk(t, lo):
        for i in range(PW // HD):
            put_keys(lo + i * HD, rope(t, i))

    def conv_silu(t, lo):
        c = _conv3(t, tm, cw_ref[:, lo:lo + PW], cb_ref[:, lo:lo + PW])
        return c * _sigmoid(c)

    def put_mq(t, lo):
        qv_ref[:, 2 * GW + lo:2 * GW + lo + PW] = (conv_silu(t, lo) * scale).astype(BF16)

    def put_mk(t, lo):
        c = conv_silu(t, GW + lo)
        for i in range(PW // HD):
            put_keys(GW + lo + i * HD, c[:, i * HD:(i + 1) * HD])

    def put_gates(t, lo):
        g = t + gb_ref[...]
        acum, b, cmax = _chunk_stats(g[:, :LANES], _log_sigmoid(g[:, LANES:]))
        lane = lax.broadcasted_iota(jnp.int32, acum.shape, 1)
        st_ref[...] = jnp.where(lane < 2 * HEADS, acum, pltpu.roll(cmax, 2 * HEADS, axis=1))
        for ch in range(tm // CHUNK):
            br_ref[ch * SUBLANES:(ch + 1) * SUBLANES, :] = b[ch * CHUNK:(ch + 1) * CHUNK].T[:SUBLANES]

    def put(ref, base):
        def store(t, lo):
            ref[:, base + lo:base + lo + PW] = t.astype(BF16)
        return store

    def pieces(lhs, w_ref, w_lo, consume):
        return [(lhs, w_ref, w_lo + lo, consume, lo) for lo in range(0, GW, PW)]

    rq, rk = pieces(a, w_ref, 0, put_rq), pieces(a, w_ref, GW, put_rk)
    rv, rg = pieces(a, w_ref, 2 * GW, put(qv_ref, GW)), pieces(a, w_ref, 3 * GW, put(og_ref, 0))
    mq, mk = pieces(a_ext, w_ref, 4 * GW, put_mq), pieces(a_ext, w_ref, 5 * GW, put_mk)
    mv, mo = pieces(a, w_ref, 6 * GW, put(qv_ref, 3 * GW)), pieces(a, w_ref, 7 * GW, put(og_ref, GW))
    gates = [(None, None, 0, put_gates, 0)]
    order = [(gates[0],), (mq[0], rv[0]), (rq[0], rv[1]), (mq[1], rg[0]), (rq[1], rg[1]), (mk[0], mv[0]),
             (rk[0], mv[1]), (mk[1], rk[1]), (mo[0], mo[1])]
    project = lambda stage: [_dot(piece[0], piece[1][:, piece[2]:piece[2] + PW]) for piece in stage]
    pending = [gate_proj]
    for k, stage in enumerate(order):
        ready = pending
        if k + 1 < len(order):
            pending = project(order[k + 1])
        for piece, t in zip(stage, ready):
            piece[3](t, piece[4])


def _mixer_kernel(lg_ref, qf_ref, ktf_ref, stf_ref, brf_ref, qb_ref, ktb_ref, stb_ref, brb_ref,
                  of_ref, ob_ref, dm_ref, qdf_ref, qdb_ref, kd_ref, cd_ref, sf_ref, sb_ref, c_ref, m_ref):
    L = CHUNK
    ii = lax.broadcasted_iota(jnp.int32, (L, L), 0)
    jj = lax.broadcasted_iota(jnp.int32, (L, L), 1)

    @pl.when(pl.program_id(1) == 0)
    def _init():
        lg = _log_sigmoid(lg_ref[...])
        lgf, lgb = lg[0:1], lg[1:2]
        row = lax.broadcasted_iota(jnp.int32, (L, GW), 0).astype(F32)
        qdf_ref[...] = jnp.exp((row + 1.0) * lgf)
        qdb_ref[...] = jnp.exp((L - row) * lgb)
        cd_ref[...] = jnp.exp(float(L) * lg)
        diff = (ii - jj).astype(F32)
        col = lax.broadcasted_iota(jnp.int32, (1, L), 1).astype(F32)
        for h in range(HEADS):
            sl = slice(h * HD, (h + 1) * HD)
            dm_ref[h] = jnp.where(diff >= 0, jnp.exp(diff * lgf[:, sl]), jnp.exp(-diff * lgb[:, sl]))
            kd_ref[h:h + 1, :] = jnp.exp((L - 1.0 - col) * lgf[:, sl])
            kd_ref[HEADS + h:HEADS + h + 1, :] = jnp.exp(col * lgb[:, sl])
        sf_ref[...] = jnp.zeros_like(sf_ref)
        sb_ref[...] = jnp.zeros_like(sb_ref)
        c_ref[...] = jnp.zeros_like(c_ref)
        m_ref[...] = jnp.zeros_like(m_ref)

    n_sub = qf_ref.shape[0] // L
    H = range(HEADS)
    C = range(2 * HEADS)
    sls = [slice(h * HD, (h + 1) * HD) for h in H]
    tril = ii >= jj
    triu = jj >= ii
    is_f = jj % (2 * HEADS) < HEADS
    ones_blk = jnp.ones((L, HD), BF16)

    def chunk(i, carry):
        i_b = n_sub - 1 - i
        rows_f = pl.ds(pl.multiple_of(i * L, L), L)
        rows_b = pl.ds(pl.multiple_of(i_b * L, L), L)
        part = lambda ref, rows, g, h: ref[rows, g * GW + h * HD:g * GW + (h + 1) * HD]

        def retention():
            qf, vf = (lambda h: part(qf_ref, rows_f, 0, h)), (lambda h: part(qf_ref, rows_f, 1, h))
            qb, vb = (lambda h: part(qb_ref, rows_b, 0, h)), (lambda h: part(qb_ref, rows_b, 1, h))
            ktf = lambda h: ktf_ref[i, sls[h], :]
            ktb = lambda h: ktb_ref[i_b, sls[h], :]
            qkc = [_dot(qf(h), jnp.concatenate([ktf(h), sf_ref[h].astype(BF16)], axis=1)) for h in H]
            p = [(qkc[h][:, :L] * dm_ref[h]).astype(BF16) for h in H]
            cross_b = [_dot(qb(h), sb_ref[h].astype(BF16)) for h in H]
            kdf = [(ktf(h).astype(F32) * kd_ref[h:h + 1, :]).astype(BF16) for h in H]
            pv = [_dot(jnp.concatenate([p[h], kdf[h]], axis=0), vf(h)) for h in H]
            for h in H:
                of_ref[rows_f, sls[h]] = (pv[h][:L] + qkc[h][:, L:] * qdf_ref[:, sls[h]]).astype(of_ref.dtype)
                ob_ref[rows_b, sls[h]] = (cross_b[h] * qdb_ref[:, sls[h]]).astype(ob_ref.dtype)
            upd_b = [_dot((ktb(h).astype(F32) * kd_ref[HEADS + h:HEADS + h + 1, :]).astype(BF16), vb(h)) for h in H]
            for h in H:
                sf_ref[h] = cd_ref[0:1, sls[h]] * sf_ref[h] + pv[h][L:]
                sb_ref[h] = cd_ref[1:2, sls[h]] * sb_ref[h] + upd_b[h]

        st = jnp.where(is_f, stf_ref[rows_f, :], stb_ref[rows_b, :])
        a_col = lambda c: st[:, c:c + 1]
        cm_col = lambda c: st[:, 2 * HEADS + c:2 * HEADS + c + 1]
        b_rows = jnp.where(lax.broadcasted_iota(jnp.int32, (SUBLANES, L), 0) < HEADS,
                           brf_ref[pl.ds(pl.multiple_of(i * SUBLANES, SUBLANES), SUBLANES), :],
                           brb_ref[pl.ds(pl.multiple_of(i_b * SUBLANES, SUBLANES), SUBLANES), :])

        def mlstm_group(cs):
            back = {c: c >= HEADS for c in cs}
            hd = {c: c % HEADS for c in cs}
            last = {c: 0 if back[c] else L - 1 for c in cs}
            q = lambda c: part(qb_ref, rows_b, 2, hd[c]) if back[c] else part(qf_ref, rows_f, 2, hd[c])
            v = lambda c: part(qb_ref, rows_b, 3, hd[c]) if back[c] else part(qf_ref, rows_f, 3, hd[c])
            kt = lambda c: (ktb_ref[i_b, GW + hd[c] * HD:GW + (hd[c] + 1) * HD, :] if back[c] else
                            ktf_ref[i, GW + hd[c] * HD:GW + (hd[c] + 1) * HD, :])
            v1 = lambda c: jnp.concatenate([v(c), ones_blk], axis=1)
            m_old = {c: m_ref[c:c + 1, 0:1] for c in cs}
            mu = {c: jnp.maximum(jnp.broadcast_to(cm_col(c), (L, L)), m_old[c]) for c in cs}
            e = {c: jnp.exp(jnp.where(triu if back[c] else tril, b_rows[c:c + 1, :] - mu[c], -jnp.inf)) for c in cs}
            qk = {c: _dot(q(c), kt(c)) for c in cs}
            s = {c: (qk[c] * e[c]).astype(BF16) for c in cs}
            qc = {c: _dot(q(c), c_ref[c].astype(BF16)) for c in cs}
            mu_last = {c: mu[c][last[c]:last[c] + 1, 0:1] for c in cs}
            kw = {c: (kt(c).astype(F32) * jnp.exp(b_rows[c:c + 1, :] - mu_last[c])).astype(BF16) for c in cs}
            sv = {c: _dot(jnp.concatenate([s[c], kw[c]], axis=0), v1(c)) for c in cs}
            for c in cs:
                w = jnp.exp(m_old[c] - mu[c])
                floor = jnp.exp(-(jnp.broadcast_to(a_col(c), (L, L)) + mu[c]))
                den = jnp.maximum(jnp.abs(w * qc[c][:, HD:] + sv[c][:L, HD:]), floor)
                dst, rows = (ob_ref, rows_b) if back[c] else (of_ref, rows_f)
                h_c = (w * qc[c][:, :HD] + sv[c][:L, :HD]) / den
                dst[rows, GW + hd[c] * HD:GW + (hd[c] + 1) * HD] = h_c.astype(dst.dtype)
            for c in cs:
                c_ref[c] = jnp.exp(m_old[c] - mu_last[c]) * c_ref[c] + sv[c][L:]
                m_ref[c:c + 1, :] = jnp.broadcast_to(a_col(c)[last[c]:last[c] + 1] + mu_last[c], (1, LANES))

        retention()
        for g in range(0, 2 * HEADS, MLSTM_GROUP):
            mlstm_group(range(g, g + MLSTM_GROUP))
        return carry

    lax.fori_loop(0, n_sub, chunk, 0, unroll=2)


def _mix_out_pieces(rows, tin, tiles_per_seq, gn_ref, wo_ref, nfw_ref, emit):
    of_ref, ofn_ref, ofp_ref, ob_ref, obn_ref, obp_ref, og_ref, ogn_ref, ogp_ref, x_ref, xn_ref, xp_ref = rows
    tm = x_ref.shape[0]
    n_ext = tm + 2 * SUBLANES
    v = {}

    def ext(m, nx, pv):
        return jnp.concatenate([m[...].astype(F32), nx[...].astype(F32)[:SUBLANES],
                                pv[...].astype(F32)[pv.shape[0] - SUBLANES:]], axis=0)

    def load():
        v["y"] = ext(of_ref, ofn_ref, ofp_ref) + ext(ob_ref, obn_ref, obp_ref)
        v["z"] = ext(og_ref, ogn_ref, ogp_ref)
        v["h"] = ext(x_ref, xn_ref, xp_ref)
        v["parts"] = []

    def head(g):
        sl = slice(g * HD, (g + 1) * HD)
        yg = v["y"][:, sl]
        d = yg - jnp.mean(yg, axis=-1, keepdims=True)
        n = d * lax.rsqrt(jnp.mean(d * d, axis=-1, keepdims=True) + HN_EPS) * gn_ref[:, sl]
        zg = v["z"][:, sl]
        gate = zg * _sigmoid(zg) if g < HEADS else _sigmoid(zg)
        v["parts"].append((n * gate).astype(BF16))
        if g % 2 == 1:
            pair = jnp.concatenate(v["parts"][-2:], axis=1)
            v["h"] = v["h"] + _dot(pair, wo_ref[(g - 1) * HD:(g + 1) * HD, :])

    def finish():
        r = lax.broadcasted_iota(jnp.int32, (n_ext, 1), 0)
        pad = ((r >= tm) & (r < tm + SUBLANES) & (tin == tiles_per_seq - 1)) | ((r >= tm + SUBLANES) & (tin == 0))
        emit(jnp.where(pad, 0.0, _rms(v["h"], nfw_ref[...])).astype(BF16), v["h"][:tm])

    return [load] + [functools.partial(head, g) for g in range(2 * HEADS)] + [finish]


def _out_ffn_kernel(*refs, tiles_per_seq):
    rows = refs[:12]
    (p_ref, gn_ref, wo_ref, nfw_ref, wg_ref, wu_ref, cw_ref, cb_ref, wd_ref, pnw_ref, pwg_ref, pgb_ref, pwp_ref, fnw_ref,
     y_ref, act_ref) = refs[12:]
    tm = p_ref.shape[0]
    stage = {}
    tin = pl.program_id(0) % tiles_per_seq
    for piece in _mix_out_pieces(rows, tin, tiles_per_seq, gn_ref, wo_ref, nfw_ref,
                                 lambda f_ext, h: stage.update(f_ext=f_ext, h=h)):
        piece()
    f_ext = stage["f_ext"]
    f = f_ext[:tm]

    fb = FF_BLOCK
    nblk = wg_ref.shape[1] // fb
    blk = lambda ref, j: ref[:, j * fb:(j + 1) * fb]
    proj = lambda j: (_dot(f_ext, blk(wg_ref, j)), _dot(f, blk(wu_ref, j)))
    nxt = proj(0)
    for j in range(nblk):
        u_ext, up = nxt
        if j + 1 < nblk:
            nxt = proj(j + 1)
        act_ref[:, j * fb:(j + 1) * fb] = (_gelu(_conv3(u_ext, tm, blk(cw_ref, j), blk(cb_ref, j))) * up).astype(BF16)
    h = stage["h"] + _dot(act_ref[...], wd_ref[...])
    halves = [slice(0, tm // 2), slice(tm // 2, tm)]
    gate_pre = [_dot(_rms(h[r], pnw_ref[...]).astype(BF16), pwg_ref[...]) for r in halves]
    ple = [_dot(p_ref[r, :].astype(BF16), pwp_ref[...]) for r in halves]
    for r, g, e in zip(halves, gate_pre, ple):
        y_ref[r, :] = _rms(h[r] + e * _sigmoid(g + pgb_ref[...]), fnw_ref[...])


def _const_spec(shape):
    nd = len(shape)
    return pl.BlockSpec(shape, lambda *_: (0,) * nd, pipeline_mode=pl.Buffered(1))


def _row_specs(tm, width, n_rows, halo=SUBLANES):
    per = tm // halo
    last = n_rows // halo - 1
    return [
        pl.BlockSpec((tm, width), lambda i: (i, 0)),
        pl.BlockSpec((halo, width), lambda i: (jnp.minimum((i + 1) * per, last), 0)),
        pl.BlockSpec((halo, width), lambda i: (jnp.maximum(i * per - 1, 0), 0)),
    ]


def _params(sem):
    return pltpu.CompilerParams(dimension_semantics=sem, vmem_limit_bytes=VMEM_LIMIT)


def _layer(x, p, w):
    bsz, seq, d = x.shape
    n = bsz * seq
    tm = ROW_TILE
    assert seq % tm == 0 and seq % CHUNK == 0
    tiles_per_seq = seq // tm
    x2 = x.reshape(n, d)
    row = lambda width: pl.BlockSpec((tm, width), lambda i: (i, 0))

    rows_per_chunk = SUBLANES
    ti = IN_TILE
    assert seq % ti == 0
    row_i = lambda width: pl.BlockSpec((ti, width), lambda i: (i, 0))
    pos_tile = pl.BlockSpec((ti, HD), lambda i: (i % (seq // ti), 0))
    qv, kt, og, stats, brow = pl.pallas_call(
        functools.partial(_in_proj_kernel, tiles_per_seq=seq // ti),
        grid=(n // ti,),
        in_specs=_row_specs(ti, d, n) + [pos_tile, pos_tile] + [_const_spec(a.shape) for a in w["in_proj"]],
        out_specs=[row_i(4 * GW), pl.BlockSpec((ti // CHUNK, 2 * GW, CHUNK), lambda i: (i, 0, 0)), row_i(2 * GW),
                   row_i(LANES), pl.BlockSpec((ti // CHUNK * rows_per_chunk, LANES), lambda i: (i, 0))],
        out_shape=[jax.ShapeDtypeStruct((n, 4 * GW), BF16), jax.ShapeDtypeStruct((n // CHUNK, 2 * GW, CHUNK), BF16),
                   jax.ShapeDtypeStruct((n, 2 * GW), BF16), jax.ShapeDtypeStruct((n, LANES), F32),
                   jax.ShapeDtypeStruct((n // CHUNK * rows_per_chunk, LANES), F32)],
        compiler_params=_params(("parallel",)),
        name="in_proj",
    )(x2, x2, x2, *w["rope"], *w["in_proj"])

    cb = MIX_CHUNKS
    assert seq % (cb * CHUNK) == 0
    nb = seq // (cb * CHUNK)
    fwd_i = lambda b, t: b * nb + t
    bwd_i = lambda b, t: b * nb + nb - 1 - t

    def block_specs(idx):
        return [pl.BlockSpec((cb * CHUNK, 4 * GW), lambda b, t: (idx(b, t), 0)),
                pl.BlockSpec((cb, 2 * GW, CHUNK), lambda b, t: (idx(b, t), 0, 0)),
                pl.BlockSpec((cb * CHUNK, LANES), lambda b, t: (idx(b, t), 0)),
                pl.BlockSpec((cb * rows_per_chunk, LANES), lambda b, t: (idx(b, t), 0))]

    out_f, out_b = pl.pallas_call(
        _mixer_kernel,
        grid=(bsz, nb),
        in_specs=[_const_spec(w["ret_logit"].shape)] + block_specs(fwd_i) + block_specs(bwd_i),
        out_specs=[pl.BlockSpec((cb * CHUNK, 2 * GW), lambda b, t: (fwd_i(b, t), 0)),
                   pl.BlockSpec((cb * CHUNK, 2 * GW), lambda b, t: (bwd_i(b, t), 0))],
        out_shape=[jax.ShapeDtypeStruct((n, 2 * GW), BF16)] * 2,
        scratch_shapes=[
            pltpu.VMEM((HEADS, CHUNK, CHUNK), F32),
            pltpu.VMEM((CHUNK, GW), F32), pltpu.VMEM((CHUNK, GW), F32),
            pltpu.VMEM((2 * HEADS, CHUNK), F32),
            pltpu.VMEM((2, GW), F32),
            pltpu.VMEM((HEADS, HD, HD), F32), pltpu.VMEM((HEADS, HD, HD), F32),
            pltpu.VMEM((2 * HEADS, HD, 2 * HD), F32),
            pltpu.VMEM((2 * HEADS, LANES), F32),
        ],
        compiler_params=_params(("arbitrary", "arbitrary")),
        name="mixers",
    )(w["ret_logit"], *([qv, kt, stats, brow] * 2))

    p2 = p.reshape(n, p.shape[-1])
    bf16_rows = 2 * SUBLANES
    mix_rows = _row_specs(tm, d, n, bf16_rows) * 3 + _row_specs(tm, d, n)
    y = pl.pallas_call(
        functools.partial(_out_ffn_kernel, tiles_per_seq=tiles_per_seq),
        grid=(n // tm,),
        in_specs=mix_rows + [row(p2.shape[-1])] + [_const_spec(a.shape) for a in (*w["mix_out"], *w["ffn"])],
        out_specs=row(d),
        out_shape=jax.ShapeDtypeStruct((n, d), F32),
        scratch_shapes=[pltpu.VMEM((tm, w["d_ff"]), BF16)],
        compiler_params=_params(("parallel",)),
        name="out_ffn",
    )(*((out_f,) * 3 + (out_b,) * 3 + (og,) * 3 + (x2,) * 3), p2, *w["mix_out"], *w["ffn"])
    return y.reshape(bsz, seq, d)


def kernel(x_prompt, x_sample, p_prompt, p_sample, norm_mix_w, w_in, mlstm_conv_w, mlstm_conv_b, mlstm_gate_b, ret_decay_logit, ret_gn_w, mlstm_gn_w, w_out, norm_ffn_w, ffn_w_gate, ffn_w_up, ffn_conv_w, ffn_conv_b, ffn_w_down, ple_w_proj, ple_norm_w, ple_w_gate, ple_gate_b, final_norm_w):
    depth = w_in.shape[0]
    d = x_prompt.shape[-1]
    d_ff = ffn_w_gate.shape[-1]
    assert d_ff % FF_BLOCK == 0
    rowv = lambda a: a.reshape(1, -1).astype(F32)
    inv = ROPE_BASE ** (-jnp.arange(0, HD, 2, dtype=F32) / HD)
    inv = jnp.concatenate([inv, inv]).reshape(1, HD)
    max_seq = max(x_prompt.shape[1], x_sample.shape[1])
    assert max_seq % ROW_TILE == 0
    rope = pl.pallas_call(
        _rope_table_kernel,
        grid=(max_seq // ROW_TILE,),
        in_specs=[_const_spec(inv.shape)],
        out_specs=[pl.BlockSpec((ROW_TILE, HD), lambda i: (i, 0))] * 2,
        out_shape=[jax.ShapeDtypeStruct((max_seq, HD), F32)] * 2,
        scratch_shapes=[pltpu.VMEM((ROW_TILE, HD), F32)] * 2,
        compiler_params=_params(("arbitrary",)),
        name="rope_tables",
    )(inv)

    assert depth == 1
    outs = [x_prompt, x_sample]
    for l in range(depth):
        wi = w_in[l].astype(BF16)
        n_gate = 2 * HEADS
        wgate = jnp.zeros((d, 2 * LANES), BF16)
        wgate = wgate.at[:, 0:n_gate].set(wi[:, 8 * GW:8 * GW + n_gate])
        wgate = wgate.at[:, LANES:LANES + n_gate].set(wi[:, 8 * GW + n_gate:8 * GW + 2 * n_gate])
        gb = mlstm_gate_b[l].astype(F32).reshape(-1)
        gbias = jnp.zeros((1, 2 * LANES), F32)
        gbias = gbias.at[0, 0:n_gate].set(gb[:n_gate]).at[0, LANES:LANES + n_gate].set(gb[n_gate:])
        w = {
            "rope": rope,
            "in_proj": [rowv(norm_mix_w[l]), wi, wgate, gbias, mlstm_conv_w[l].astype(F32), rowv(mlstm_conv_b[l])],
            "ret_logit": jnp.repeat(ret_decay_logit[l].astype(F32), HD, axis=1),
            "d_ff": d_ff,
            "mix_out": [jnp.concatenate([rowv(ret_gn_w[l]), rowv(mlstm_gn_w[l])], axis=1), w_out[l].astype(BF16),
                        rowv(norm_ffn_w[l])],
            "ffn": [ffn_w_gate[l].astype(BF16), ffn_w_up[l].astype(BF16), ffn_conv_w[l].astype(F32),
                    rowv(ffn_conv_b[l]), ffn_w_down[l].astype(BF16), rowv(ple_norm_w[l]), ple_w_gate[l].astype(BF16),
                    rowv(ple_gate_b[l]), ple_w_proj[l].astype(BF16), rowv(final_norm_w)],
        }
        outs = [_layer(h, p[l], w) for h, p in zip(outs, (p_prompt, p_sample))]
    return tuple(outs)
```

```python
import functools

import jax
import jax.numpy as jnp
from jax import lax
from jax.experimental import pallas as pl
from jax.experimental.pallas import tpu as pltpu

F32 = jnp.float32
BF16 = jnp.bfloat16

HEADS = 4
HD = 128
GW = HEADS * HD
CHUNK = 128
ROPE_BASE = 10000.0
RMS_EPS = 1e-6
HN_EPS = 1e-5
SUBLANES = 8
LANES = 128
ROW_TILE = 512
IN_TILE = 1024
FF_BLOCK = 256
MIX_CHUNKS = 8
MLSTM_GROUP = 4
VMEM_LIMIT = 56 * 1024 * 1024


def _rms(x, w):
    return x * lax.rsqrt(jnp.mean(x * x, axis=-1, keepdims=True) + RMS_EPS) * w


def _rms_project(x, w_ref, projections, kb):
    inv = lax.rsqrt(jnp.mean(x * x, axis=-1, keepdims=True) + RMS_EPS)
    blocks, outs = [], [None] * len(projections)
    for k in range(0, x.shape[1], kb):
        xk = (x[:, k:k + kb] * inv * w_ref[:, k:k + kb]).astype(BF16)
        blocks.append(xk)
        for i, (rows, p_ref, lo, width) in enumerate(projections):
            part = _dot(xk[:rows], p_ref[k:k + kb, lo:lo + width])
            outs[i] = part if outs[i] is None else outs[i] + part
    return jnp.concatenate(blocks, axis=1), outs


def _log_sigmoid(x):
    return jnp.minimum(x, 0.0) - jnp.log1p(jnp.exp(-jnp.abs(x)))


def _sigmoid(x):
    return 1.0 / (1.0 + jnp.exp(-x))


def _gelu(x):
    return 0.5 * x * (1.0 + lax.erf(x * (0.5 ** 0.5)))


def _dot(a, b):
    return jnp.dot(a, b, preferred_element_type=F32)


def _dot_nt(a, b):
    return lax.dot_general(a, b, (((1,), (1,)), ((), ())), preferred_element_type=F32)


def _ext_rows(main, nxt, prv, tin, tiles_per_seq):
    nxt = jnp.where(tin == tiles_per_seq - 1, 0.0, nxt)
    prv = jnp.where(tin == 0, 0.0, prv)
    return jnp.concatenate([main, nxt, prv], axis=0)


def _conv3(ext, rows, w, b):
    n = ext.shape[0]
    up = pltpu.roll(ext, 1, axis=0)[:rows]
    dn = pltpu.roll(ext, n - 1, axis=0)[:rows]
    return up * w[0:1] + ext[:rows] * w[1:2] + dn * w[2:3] + b


def _chunk_stats(ig, lf):
    tm = ig.shape[0]
    ii = lax.broadcasted_iota(jnp.int32, (CHUNK, CHUNK), 0)
    jj = lax.broadcasted_iota(jnp.int32, (CHUNK, CHUNK), 1)
    tril = (ii >= jj).astype(F32)
    triu = (jj >= ii).astype(F32)
    hi = lax.Precision.HIGHEST
    a = jnp.concatenate([
        jnp.where(jj < HEADS,
                  jnp.dot(tril, lf[r:r + CHUNK], precision=hi, preferred_element_type=F32),
                  jnp.dot(triu, lf[r:r + CHUNK], precision=hi, preferred_element_type=F32))
        for r in range(0, tm, CHUNK)], axis=0)
    b = ig - a
    r = lax.broadcasted_iota(jnp.int32, (tm, LANES), 0) % CHUNK
    pre = suf = b
    s = 1
    while s < CHUNK:
        pre = jnp.maximum(pre, jnp.where(r >= s, pltpu.roll(pre, s, axis=0), -jnp.inf))
        suf = jnp.maximum(suf, jnp.where(r < CHUNK - s, pltpu.roll(suf, tm - s, axis=0), -jnp.inf))
        s *= 2
    cm = jnp.where(lax.broadcasted_iota(jnp.int32, (tm, LANES), 1) < HEADS, pre, suf)
    return a, b, cm


def _rope_table_kernel(inv_ref, cos_ref, sin_ref, cr_ref, sr_ref):
    tm = cos_ref.shape[0]

    @pl.when(pl.program_id(0) == 0)
    def _offsets():
        off = lax.broadcasted_iota(jnp.int32, (tm, HD), 0).astype(F32) * inv_ref[...]
        cr_ref[...] = jnp.cos(off)
        sr_ref[...] = jnp.sin(off)

    base = (pl.program_id(0) * tm).astype(F32) * inv_ref[...]
    cb, sb = jnp.cos(base), jnp.sin(base)
    cos_ref[...] = cb * cr_ref[...] - sb * sr_ref[...]
    sin = sb * cr_ref[...] + cb * sr_ref[...]
    lane = lax.broadcasted_iota(jnp.int32, (tm, HD), 1)
    sin_ref[...] = jnp.where(lane < HD // 2, -sin, sin)


def _in_proj_kernel(x_ref, xn_ref, xp_ref, cos_ref, sin_ref, nw_ref, w_ref, wg_ref, gb_ref, cw_ref, cb_ref,
                    qv_ref, kt_ref, og_ref, st_ref, br_ref, *, tiles_per_seq):
    tm = x_ref.shape[0]
    tin = pl.program_id(0) % tiles_per_seq
    x_ext = _ext_rows(x_ref[...], xn_ref[...], xp_ref[...], tin, tiles_per_seq)
    a_ext, (gate_proj,) = _rms_project(x_ext, nw_ref, [(tm, wg_ref, 0, wg_ref.shape[1])], 2 * LANES)
    a = a_ext[:tm]
    scale = HD ** -0.5

    PW = 2 * HD

    def rope(t, i):
        th = t[:, i * HD:(i + 1) * HD]
        return th * cos_ref[...] + pltpu.roll(th, HD // 2, axis=1) * sin_ref[...]

    def put_rq(t, lo):
        for i in range(PW // HD):
            qv_ref[:, lo + i * HD:lo + (i + 1) * HD] = (rope(t, i) * scale).astype(BF16)

    def put_keys(row_lo, k):
        kt = k.T.astype(BF16)
        for ch in range(tm // CHUNK):
            kt_ref[ch, row_lo:row_lo + HD, :] = kt[:, ch * CHUNK:(ch + 1) * CHUNK]

    def put_rk(t, lo):
        for i in range(PW // HD):
            put_keys(lo + i * HD, rope(t, i))

    def conv_silu(t, lo):
        c = _conv3(t, tm, cw_ref[:, lo:lo + PW], cb_ref[:, lo:lo + PW])
        return c * _sigmoid(c)

    def put_mq(t, lo):
        qv_ref[:, 2 * GW + lo:2 * GW + lo + PW] = (conv_silu(t, lo) * scale).astype(BF16)

    def put_mk(t, lo):
        c = conv_silu(t, GW + lo)
        for i in range(PW // HD):
            put_keys(GW + lo + i * HD, c[:, i * HD:(i + 1) * HD])

    def put_gates(t, lo):
        g = t + gb_ref[...]
        acum, b, cmax = _chunk_stats(g[:, :LANES], _log_sigmoid(g[:, LANES:]))
        lane = lax.broadcasted_iota(jnp.int32, acum.shape, 1)
        st_ref[...] = jnp.where(lane < 2 * HEADS, acum, pltpu.roll(cmax, 2 * HEADS, axis=1))
        for ch in range(tm // CHUNK):
            br_ref[ch * SUBLANES:(ch + 1) * SUBLANES, :] = b[ch * CHUNK:(ch + 1) * CHUNK].T[:SUBLANES]

    def put(ref, base):
        def store(t, lo):
            ref[:, base + lo:base + lo + PW] = t.astype(BF16)
        return store

    def pieces(lhs, w_ref, w_lo, consume):
        return [(lhs, w_ref, w_lo + lo, consume, lo) for lo in range(0, GW, PW)]

    rq, rk = pieces(a, w_ref, 0, put_rq), pieces(a, w_ref, GW, put_rk)
    rv, rg = pieces(a, w_ref, 2 * GW, put(qv_ref, GW)), pieces(a, w_ref, 3 * GW, put(og_ref, 0))
    mq, mk = pieces(a_ext, w_ref, 4 * GW, put_mq), pieces(a_ext, w_ref, 5 * GW, put_mk)
    mv, mo = pieces(a, w_ref, 6 * GW, put(qv_ref, 3 * GW)), pieces(a, w_ref, 7 * GW, put(og_ref, GW))
    gates = [(None, None, 0, put_gates, 0)]
    order = [(gates[0],), (mq[0], rv[0]), (rq[0], rv[1]), (mq[1], rg[0]), (rq[1], rg[1]), (mk[0], mv[0]),
             (rk[0], mv[1]), (mk[1], rk[1]), (mo[0], mo[1])]
    project = lambda stage: [_dot(piece[0], piece[1][:, piece[2]:piece[2] + PW]) for piece in stage]
    pending = [gate_proj]
    for k, stage in enumerate(order):
        ready = pending
        if k + 1 < len(order):
            pending = project(order[k + 1])
        for piece, t in zip(stage, ready):
            piece[3](t, piece[4])


def _mixer_kernel(lg_ref, qf_ref, ktf_ref, stf_ref, brf_ref, qb_ref, ktb_ref, stb_ref, brb_ref,
                  of_ref, ob_ref, dm_ref, qdf_ref, qdb_ref, kd_ref, cd_ref, sf_ref, sb_ref, c_ref, m_ref):
    L = CHUNK
    ii = lax.broadcasted_iota(jnp.int32, (L, L), 0)
    jj = lax.broadcasted_iota(jnp.int32, (L, L), 1)

    @pl.when(pl.program_id(1) == 0)
    def _init():
        lg = _log_sigmoid(lg_ref[...])
        lgf, lgb = lg[0:1], lg[1:2]
        row = lax.broadcasted_iota(jnp.int32, (L, GW), 0).astype(F32)
        qdf_ref[...] = jnp.exp((row + 1.0) * lgf)
        qdb_ref[...] = jnp.exp((L - row) * lgb)
        cd_ref[...] = jnp.exp(float(L) * lg)
        diff = (ii - jj).astype(F32)
        col = lax.broadcasted_iota(jnp.int32, (1, L), 1).astype(F32)
        for h in range(HEADS):
            sl = slice(h * HD, (h + 1) * HD)
            dm_ref[h] = jnp.where(diff >= 0, jnp.exp(diff * lgf[:, sl]), jnp.exp(-diff * lgb[:, sl]))
            kd_ref[h:h + 1, :] = jnp.exp((L - 1.0 - col) * lgf[:, sl])
            kd_ref[HEADS + h:HEADS + h + 1, :] = jnp.exp(col * lgb[:, sl])
        sf_ref[...] = jnp.zeros_like(sf_ref)
        sb_ref[...] = jnp.zeros_like(sb_ref)
        c_ref[...] = jnp.zeros_like(c_ref)
        m_ref[...] = jnp.zeros_like(m_ref)

    n_sub = qf_ref.shape[0] // L
    H = range(HEADS)
    C = range(2 * HEADS)
    sls = [slice(h * HD, (h + 1) * HD) for h in H]
    tril = ii >= jj
    triu = jj >= ii
    is_f = jj % (2 * HEADS) < HEADS
    ones_blk = jnp.ones((L, HD), BF16)

    def chunk(i, carry):
        i_b = n_sub - 1 - i
        rows_f = pl.ds(pl.multiple_of(i * L, L), L)
        rows_b = pl.ds(pl.multiple_of(i_b * L, L), L)
        part = lambda ref, rows, g, h: ref[rows, g * GW + h * HD:g * GW + (h + 1) * HD]

        def retention():
            qf, vf = (lambda h: part(qf_ref, rows_f, 0, h)), (lambda h: part(qf_ref, rows_f, 1, h))
            qb, vb = (lambda h: part(qb_ref, rows_b, 0, h)), (lambda h: part(qb_ref, rows_b, 1, h))
            ktf = lambda h: ktf_ref[i, sls[h], :]
            ktb = lambda h: ktb_ref[i_b, sls[h], :]
            qkc = [_dot(qf(h), jnp.concatenate([ktf(h), sf_ref[h].astype(BF16)], axis=1)) for h in H]
            p = [(qkc[h][:, :L] * dm_ref[h]).astype(BF16) for h in H]
            cross_b = [_dot(qb(h), sb_ref[h].astype(BF16)) for h in H]
            kdf = [(ktf(h).astype(F32) * kd_ref[h:h + 1, :]).astype(BF16) for h in H]
            pv = [_dot(jnp.concatenate([p[h], kdf[h]], axis=0), vf(h)) for h in H]
            for h in H:
                of_ref[rows_f, sls[h]] = (pv[h][:L] + qkc[h][:, L:] * qdf_ref[:, sls[h]]).astype(of_ref.dtype)
                ob_ref[rows_b, sls[h]] = (cross_b[h] * qdb_ref[:, sls[h]]).astype(ob_ref.dtype)
            upd_b = [_dot((ktb(h).astype(F32) * kd_ref[HEADS + h:HEADS + h + 1, :]).astype(BF16), vb(h)) for h in H]
            for h in H:
                sf_ref[h] = cd_ref[0:1, sls[h]] * sf_ref[h] + pv[h][L:]
                sb_ref[h] = cd_ref[1:2, sls[h]] * sb_ref[h] + upd_b[h]

        st = jnp.where(is_f, stf_ref[rows_f, :], stb_ref[rows_b, :])
        a_col = lambda c: st[:, c:c + 1]
        cm_col = lambda c: st[:, 2 * HEADS + c:2 * HEADS + c + 1]
        b_rows = jnp.where(lax.broadcasted_iota(jnp.int32, (SUBLANES, L), 0) < HEADS,
                           brf_ref[pl.ds(pl.multiple_of(i * SUBLANES, SUBLANES), SUBLANES), :],
                           brb_ref[pl.ds(pl.multiple_of(i_b * SUBLANES, SUBLANES), SUBLANES), :])

        def mlstm_group(cs):
            back = {c: c >= HEADS for c in cs}
            hd = {c: c % HEADS for c in cs}
            last = {c: 0 if back[c] else L - 1 for c in cs}
            q = lambda c: part(qb_ref, rows_b, 2, hd[c]) if back[c] else part(qf_ref, rows_f, 2, hd[c])
            v = lambda c: part(qb_ref, rows_b, 3, hd[c]) if back[c] else part(qf_ref, rows_f, 3, hd[c])
            kt = lambda c: (ktb_ref[i_b, GW + hd[c] * HD:GW + (hd[c] + 1) * HD, :] if back[c] else
                            ktf_ref[i, GW + hd[c] * HD:GW + (hd[c] + 1) * HD, :])
            v1 = lambda c: jnp.concatenate([v(c), ones_blk], axis=1)
            m_old = {c: m_ref[c:c + 1, 0:1] for c in cs}
            mu = {c: jnp.maximum(jnp.broadcast_to(cm_col(c), (L, L)), m_old[c]) for c in cs}
            e = {c: jnp.exp(jnp.where(triu if back[c] else tril, b_rows[c:c + 1, :] - mu[c], -jnp.inf)) for c in cs}
            qk = {c: _dot(q(c), kt(c)) for c in cs}
            s = {c: (qk[c] * e[c]).astype(BF16) for c in cs}
            qc = {c: _dot(q(c), c_ref[c].astype(BF16)) for c in cs}
            mu_last = {c: mu[c][last[c]:last[c] + 1, 0:1] for c in cs}
            kw = {c: (kt(c).astype(F32) * jnp.exp(b_rows[c:c + 1, :] - mu_last[c])).astype(BF16) for c in cs}
            sv = {c: _dot(jnp.concatenate([s[c], kw[c]], axis=0), v1(c)) for c in cs}
            for c in cs:
                w = jnp.exp(m_old[c] - mu[c])
                floor = jnp.exp(-(jnp.broadcast_to(a_col(c), (L, L)) + mu[c]))
                den = jnp.maximum(jnp.abs(w * qc[c][:, HD:] + sv[c][:L, HD:]), floor)
                dst, rows = (ob_ref, rows_b) if back[c] else (of_ref, rows_f)
                h_c = (w * qc[c][:, :HD] + sv[c][:L, :HD]) / den
                dst[rows, GW + hd[c] * HD:GW + (hd[c] + 1) * HD] = h_c.astype(dst.dtype)
            for c in cs:
                c_ref[c] = jnp.exp(m_old[c] - mu_last[c]) * c_ref[c] + sv[c][L:]
                m_ref[c:c + 1, :] = jnp.broadcast_to(a_col(c)[last[c]:last[c] + 1] + mu_last[c], (1, LANES))

        retention()
        for g in range(0, 2 * HEADS, MLSTM_GROUP):
            mlstm_group(range(g, g + MLSTM_GROUP))
        return carry

    lax.fori_loop(0, n_sub, chunk, 0, unroll=4)


def _mix_out_pieces(rows, tin, tiles_per_seq, gn_ref, wo_ref, nfw_ref, emit):
    of_ref, ofn_ref, ofp_ref, ob_ref, obn_ref, obp_ref, og_ref, ogn_ref, ogp_ref, x_ref, xn_ref, xp_ref = rows
    tm = x_ref.shape[0]
    n_ext = tm + 2 * SUBLANES
    v = {}

    def ext(m, nx, pv):
        return jnp.concatenate([m[...].astype(F32), nx[...].astype(F32)[:SUBLANES],
                                pv[...].astype(F32)[pv.shape[0] - SUBLANES:]], axis=0)

    def load():
        v["y"] = ext(of_ref, ofn_ref, ofp_ref) + ext(ob_ref, obn_ref, obp_ref)
        v["z"] = ext(og_ref, ogn_ref, ogp_ref)
        v["h"] = ext(x_ref, xn_ref, xp_ref)
        v["parts"] = []

    def head(g):
        sl = slice(g * HD, (g + 1) * HD)
        yg = v["y"][:, sl]
        d = yg - jnp.mean(yg, axis=-1, keepdims=True)
        n = d * lax.rsqrt(jnp.mean(d * d, axis=-1, keepdims=True) + HN_EPS) * gn_ref[:, sl]
        zg = v["z"][:, sl]
        gate = zg * _sigmoid(zg) if g < HEADS else _sigmoid(zg)
        v["parts"].append((n * gate).astype(BF16))
        if g % 2 == 1:
            pair = jnp.concatenate(v["parts"][-2:], axis=1)
            v["h"] = v["h"] + _dot(pair, wo_ref[(g - 1) * HD:(g + 1) * HD, :])

    def finish():
        r = lax.broadcasted_iota(jnp.int32, (n_ext, 1), 0)
        pad = ((r >= tm) & (r < tm + SUBLANES) & (tin == tiles_per_seq - 1)) | ((r >= tm + SUBLANES) & (tin == 0))
        emit(jnp.where(pad, 0.0, _rms(v["h"], nfw_ref[...])).astype(BF16), v["h"][:tm])

    return [load] + [functools.partial(head, g) for g in range(2 * HEADS)] + [finish]


def _out_ffn_kernel(*refs, tiles_per_seq):
    rows = refs[:12]
    (p_ref, gn_ref, wo_ref, nfw_ref, wg_ref, wu_ref, cw_ref, cb_ref, wd_ref, pnw_ref, pwg_ref, pgb_ref, pwp_ref, fnw_ref,
     y_ref, act_ref) = refs[12:]
    tm = p_ref.shape[0]
    stage = {}
    tin = pl.program_id(0) % tiles_per_seq
    for piece in _mix_out_pieces(rows, tin, tiles_per_seq, gn_ref, wo_ref, nfw_ref,
                                 lambda f_ext, h: stage.update(f_ext=f_ext, h=h)):
        piece()
    f_ext = stage["f_ext"]
    f = f_ext[:tm]

    fb = FF_BLOCK
    nblk = wg_ref.shape[1] // fb
    blk = lambda ref, j: ref[:, j * fb:(j + 1) * fb]
    proj = lambda j: (_dot(f_ext, blk(wg_ref, j)), _dot(f, blk(wu_ref, j)))
    nxt = proj(0)
    for j in range(nblk):
        u_ext, up = nxt
        if j + 1 < nblk:
            nxt = proj(j + 1)
        act_ref[:, j * fb:(j + 1) * fb] = (_gelu(_conv3(u_ext, tm, blk(cw_ref, j), blk(cb_ref, j))) * up).astype(BF16)
    h = stage["h"] + _dot(act_ref[...], wd_ref[...])
    halves = [slice(0, tm // 2), slice(tm // 2, tm)]
    gate_pre = [_dot(_rms(h[r], pnw_ref[...]).astype(BF16), pwg_ref[...]) for r in halves]
    ple = [_dot(p_ref[r, :].astype(BF16), pwp_ref[...]) for r in halves]
    for r, g, e in zip(halves, gate_pre, ple):
        y_ref[r, :] = _rms(h[r] + e * _sigmoid(g + pgb_ref[...]), fnw_ref[...])


def _const_spec(shape):
    nd = len(shape)
    return pl.BlockSpec(shape, lambda *_: (0,) * nd, pipeline_mode=pl.Buffered(1))


def _row_specs(tm, width, n_rows, halo=SUBLANES):
    per = tm // halo
    last = n_rows // halo - 1
    return [
        pl.BlockSpec((tm, width), lambda i: (i, 0)),
        pl.BlockSpec((halo, width), lambda i: (jnp.minimum((i + 1) * per, last), 0)),
        pl.BlockSpec((halo, width), lambda i: (jnp.maximum(i * per - 1, 0), 0)),
    ]


def _params(sem):
    return pltpu.CompilerParams(dimension_semantics=sem, vmem_limit_bytes=VMEM_LIMIT)


def _layer(x, p, w):
    bsz, seq, d = x.shape
    n = bsz * seq
    tm = ROW_TILE
    assert seq % tm == 0 and seq % CHUNK == 0
    tiles_per_seq = seq // tm
    x2 = x.reshape(n, d)
    row = lambda width: pl.BlockSpec((tm, width), lambda i: (i, 0))

    rows_per_chunk = SUBLANES
    ti = IN_TILE
    assert seq % ti == 0
    row_i = lambda width: pl.BlockSpec((ti, width), lambda i: (i, 0))
    pos_tile = pl.BlockSpec((ti, HD), lambda i: (i % (seq // ti), 0))
    qv, kt, og, stats, brow = pl.pallas_call(
        functools.partial(_in_proj_kernel, tiles_per_seq=seq // ti),
        grid=(n // ti,),
        in_specs=_row_specs(ti, d, n) + [pos_tile, pos_tile] + [_const_spec(a.shape) for a in w["in_proj"]],
        out_specs=[row_i(4 * GW), pl.BlockSpec((ti // CHUNK, 2 * GW, CHUNK), lambda i: (i, 0, 0)), row_i(2 * GW),
                   row_i(LANES), pl.BlockSpec((ti // CHUNK * rows_per_chunk, LANES), lambda i: (i, 0))],
        out_shape=[jax.ShapeDtypeStruct((n, 4 * GW), BF16), jax.ShapeDtypeStruct((n // CHUNK, 2 * GW, CHUNK), BF16),
                   jax.ShapeDtypeStruct((n, 2 * GW), BF16), jax.ShapeDtypeStruct((n, LANES), F32),
                   jax.ShapeDtypeStruct((n // CHUNK * rows_per_chunk, LANES), F32)],
        compiler_params=_params(("parallel",)),
        name="in_proj",
    )(x2, x2, x2, *w["rope"], *w["in_proj"])

    cb = MIX_CHUNKS
    assert seq % (cb * CHUNK) == 0
    nb = seq // (cb * CHUNK)
    fwd_i = lambda b, t: b * nb + t
    bwd_i = lambda b, t: b * nb + nb - 1 - t

    def block_specs(idx):
        return [pl.BlockSpec((cb * CHUNK, 4 * GW), lambda b, t: (idx(b, t), 0)),
                pl.BlockSpec((cb, 2 * GW, CHUNK), lambda b, t: (idx(b, t), 0, 0)),
                pl.BlockSpec((cb * CHUNK, LANES), lambda b, t: (idx(b, t), 0)),
                pl.BlockSpec((cb * rows_per_chunk, LANES), lambda b, t: (idx(b, t), 0))]

    out_f, out_b = pl.pallas_call(
        _mixer_kernel,
        grid=(bsz, nb),
        in_specs=[_const_spec(w["ret_logit"].shape)] + block_specs(fwd_i) + block_specs(bwd_i),
        out_specs=[pl.BlockSpec((cb * CHUNK, 2 * GW), lambda b, t: (fwd_i(b, t), 0)),
                   pl.BlockSpec((cb * CHUNK, 2 * GW), lambda b, t: (bwd_i(b, t), 0))],
        out_shape=[jax.ShapeDtypeStruct((n, 2 * GW), BF16)] * 2,
        scratch_shapes=[
            pltpu.VMEM((HEADS, CHUNK, CHUNK), F32),
            pltpu.VMEM((CHUNK, GW), F32), pltpu.VMEM((CHUNK, GW), F32),
            pltpu.VMEM((2 * HEADS, CHUNK), F32),
            pltpu.VMEM((2, GW), F32),
            pltpu.VMEM((HEADS, HD, HD), F32), pltpu.VMEM((HEADS, HD, HD), F32),
            pltpu.VMEM((2 * HEADS, HD, 2 * HD), F32),
            pltpu.VMEM((2 * HEADS, LANES), F32),
        ],
        compiler_params=_params(("arbitrary", "arbitrary")),
        name="mixers",
    )(w["ret_logit"], *([qv, kt, stats, brow] * 2))

    p2 = p.reshape(n, p.shape[-1])
    bf16_rows = 2 * SUBLANES
    mix_rows = _row_specs(tm, d, n, bf16_rows) * 3 + _row_specs(tm, d, n)
    y = pl.pallas_call(
        functools.partial(_out_ffn_kernel, tiles_per_seq=tiles_per_seq),
        grid=(n // tm,),
        in_specs=mix_rows + [row(p2.shape[-1])] + [_const_spec(a.shape) for a in (*w["mix_out"], *w["ffn"])],
        out_specs=row(d),
        out_shape=jax.ShapeDtypeStruct((n, d), F32),
        scratch_shapes=[pltpu.VMEM((tm, w["d_ff"]), BF16)],
        compiler_params=_params(("parallel",)),
        name="out_ffn",
    )(*((out_f,) * 3 + (out_b,) * 3 + (og,) * 3 + (x2,) * 3), p2, *w["mix_out"], *w["ffn"])
    return y.reshape(bsz, seq, d)


def kernel(x_prompt, x_sample, p_prompt, p_sample, norm_mix_w, w_in, mlstm_conv_w, mlstm_conv_b, mlstm_gate_b, ret_decay_logit, ret_gn_w, mlstm_gn_w, w_out, norm_ffn_w, ffn_w_gate, ffn_w_up, ffn_conv_w, ffn_conv_b, ffn_w_down, ple_w_proj, ple_norm_w, ple_w_gate, ple_gate_b, final_norm_w):
    depth = w_in.shape[0]
    d = x_prompt.shape[-1]
    d_ff = ffn_w_gate.shape[-1]
    assert d_ff % FF_BLOCK == 0
    rowv = lambda a: a.reshape(1, -1).astype(F32)
    inv = ROPE_BASE ** (-jnp.arange(0, HD, 2, dtype=F32) / HD)
    inv = jnp.concatenate([inv, inv]).reshape(1, HD)
    max_seq = max(x_prompt.shape[1], x_sample.shape[1])
    assert max_seq % ROW_TILE == 0
    rope = pl.pallas_call(
        _rope_table_kernel,
        grid=(max_seq // ROW_TILE,),
        in_specs=[_const_spec(inv.shape)],
        out_specs=[pl.BlockSpec((ROW_TILE, HD), lambda i: (i, 0))] * 2,
        out_shape=[jax.ShapeDtypeStruct((max_seq, HD), F32)] * 2,
        scratch_shapes=[pltpu.VMEM((ROW_TILE, HD), F32)] * 2,
        compiler_params=_params(("arbitrary",)),
        name="rope_tables",
    )(inv)

    assert depth == 1
    outs = [x_prompt, x_sample]
    for l in range(depth):
        wi = w_in[l].astype(BF16)
        n_gate = 2 * HEADS
        wgate = jnp.zeros((d, 2 * LANES), BF16)
        wgate = wgate.at[:, 0:n_gate].set(wi[:, 8 * GW:8 * GW + n_gate])
        wgate = wgate.at[:, LANES:LANES + n_gate].set(wi[:, 8 * GW + n_gate:8 * GW + 2 * n_gate])
        gb = mlstm_gate_b[l].astype(F32).reshape(-1)
        gbias = jnp.zeros((1, 2 * LANES), F32)
        gbias = gbias.at[0, 0:n_gate].set(gb[:n_gate]).at[0, LANES:LANES + n_gate].set(gb[n_gate:])
        w = {
            "rope": rope,
            "in_proj": [rowv(norm_mix_w[l]), wi, wgate, gbias, mlstm_conv_w[l].astype(F32), rowv(mlstm_conv_b[l])],
            "ret_logit": jnp.repeat(ret_decay_logit[l].astype(F32), HD, axis=1),
            "d_ff": d_ff,
            "mix_out": [jnp.concatenate([rowv(ret_gn_w[l]), rowv(mlstm_gn_w[l])], axis=1), w_out[l].astype(BF16),
                        rowv(norm_ffn_w[l])],
            "ffn": [ffn_w_gate[l].astype(BF16), ffn_w_up[l].astype(BF16), ffn_conv_w[l].astype(F32),
                    rowv(ffn_conv_b[l]), ffn_w_down[l].astype(BF16), rowv(ple_norm_w[l]), ple_w_gate[l].astype(BF16),
                    rowv(ple_gate_b[l]), ple_w_proj[l].astype(BF16), rowv(final_norm_w)],
        }
        outs = [_layer(h, p[l], w) for h, p in zip(outs, (p_prompt, p_sample))]
    return tuple(outs)
```

```python
import functools

import jax
import jax.numpy as jnp
from jax import lax
from jax.experimental import pallas as pl
from jax.experimental.pallas import tpu as pltpu

F32 = jnp.float32
BF16 = jnp.bfloat16

HEADS = 4
HD = 128
GW = HEADS * HD
CHUNK = 128
ROPE_BASE = 10000.0
RMS_EPS = 1e-6
HN_EPS = 1e-5
SUBLANES = 8
LANES = 128
ROW_TILE = 512
IN_TILE = 1024
FF_BLOCK = 256
MIX_CHUNKS = 8
MIX_UNROLL = 4
MLSTM_GROUP = 4
VMEM_LIMIT = 56 * 1024 * 1024


def _rms(x, w):
    return x * lax.rsqrt(jnp.mean(x * x, axis=-1, keepdims=True) + RMS_EPS) * w


def _rms_project(x, w_ref, projections, kb):
    inv = lax.rsqrt(jnp.mean(x * x, axis=-1, keepdims=True) + RMS_EPS)
    blocks, outs = [], [None] * len(projections)
    for k in range(0, x.shape[1], kb):
        xk = (x[:, k:k + kb] * inv * w_ref[:, k:k + kb]).astype(BF16)
        blocks.append(xk)
        for i, (rows, p_ref, lo, width) in enumerate(projections):
            part = _dot(xk[:rows], p_ref[k:k + kb, lo:lo + width])
            outs[i] = part if outs[i] is None else outs[i] + part
    return jnp.concatenate(blocks, axis=1), outs


def _log_sigmoid(x):
    return jnp.minimum(x, 0.0) - jnp.log1p(jnp.exp(-jnp.abs(x)))


def _sigmoid(x):
    return 1.0 / (1.0 + jnp.exp(-x))


def _gelu(x):
    return 0.5 * x * (1.0 + lax.erf(x * (0.5 ** 0.5)))


def _dot(a, b):
    return jnp.dot(a, b, preferred_element_type=F32)


def _ext_rows(main, nxt, prv, tin, tiles_per_seq):
    nxt = jnp.where(tin == tiles_per_seq - 1, 0.0, nxt)
    prv = jnp.where(tin == 0, 0.0, prv)
    return jnp.concatenate([main, nxt, prv], axis=0)


def _conv3(ext, rows, w, b):
    n = ext.shape[0]
    up = pltpu.roll(ext, 1, axis=0)[:rows]
    dn = pltpu.roll(ext, n - 1, axis=0)[:rows]
    return up * w[0:1] + ext[:rows] * w[1:2] + dn * w[2:3] + b


def _chunk_stats(ig, lf):
    tm = ig.shape[0]
    ii = lax.broadcasted_iota(jnp.int32, (CHUNK, CHUNK), 0)
    jj = lax.broadcasted_iota(jnp.int32, (CHUNK, CHUNK), 1)
    tril = (ii >= jj).astype(F32)
    triu = (jj >= ii).astype(F32)
    hi = lax.Precision.HIGHEST
    a = jnp.concatenate([
        jnp.where(jj < HEADS,
                  jnp.dot(tril, lf[r:r + CHUNK], precision=hi, preferred_element_type=F32),
                  jnp.dot(triu, lf[r:r + CHUNK], precision=hi, preferred_element_type=F32))
        for r in range(0, tm, CHUNK)], axis=0)
    b = ig - a
    r = lax.broadcasted_iota(jnp.int32, (tm, LANES), 0) % CHUNK
    pre = suf = b
    s = 1
    while s < CHUNK:
        pre = jnp.maximum(pre, jnp.where(r >= s, pltpu.roll(pre, s, axis=0), -jnp.inf))
        suf = jnp.maximum(suf, jnp.where(r < CHUNK - s, pltpu.roll(suf, tm - s, axis=0), -jnp.inf))
        s *= 2
    cm = jnp.where(lax.broadcasted_iota(jnp.int32, (tm, LANES), 1) < HEADS, pre, suf)
    return a, b, cm


def _rope_table_kernel(inv_ref, cos_ref, sin_ref, cr_ref, sr_ref):
    tm = cos_ref.shape[0]

    @pl.when(pl.program_id(0) == 0)
    def _offsets():
        off = lax.broadcasted_iota(jnp.int32, (tm, HD), 0).astype(F32) * inv_ref[...]
        cr_ref[...] = jnp.cos(off)
        sr_ref[...] = jnp.sin(off)

    base = (pl.program_id(0) * tm).astype(F32) * inv_ref[...]
    cb, sb = jnp.cos(base), jnp.sin(base)
    cos_ref[...] = cb * cr_ref[...] - sb * sr_ref[...]
    sin = sb * cr_ref[...] + cb * sr_ref[...]
    lane = lax.broadcasted_iota(jnp.int32, (tm, HD), 1)
    sin_ref[...] = jnp.where(lane < HD // 2, -sin, sin)


def _in_proj_kernel(x_ref, xn_ref, xp_ref, cos_ref, sin_ref, nw_ref, w_ref, wg_ref, gb_ref, cw_ref, cb_ref,
                    qv_ref, kt_ref, og_ref, st_ref, br_ref, *, tiles_per_seq):
    tm = x_ref.shape[0]
    tin = pl.program_id(0) % tiles_per_seq
    x_ext = _ext_rows(x_ref[...], xn_ref[...], xp_ref[...], tin, tiles_per_seq)
    a_ext, (gate_proj,) = _rms_project(x_ext, nw_ref, [(tm, wg_ref, 0, wg_ref.shape[1])], 2 * LANES)
    a = a_ext[:tm]
    scale = HD ** -0.5

    PW = 2 * HD

    def rope(t, i):
        th = t[:, i * HD:(i + 1) * HD]
        return th * cos_ref[...] + pltpu.roll(th, HD // 2, axis=1) * sin_ref[...]

    def put_rq(t, lo):
        for i in range(PW // HD):
            qv_ref[:, lo + i * HD:lo + (i + 1) * HD] = (rope(t, i) * scale).astype(BF16)

    def put_keys(row_lo, k):
        kt = k.T.astype(BF16)
        for ch in range(tm // CHUNK):
            kt_ref[ch, row_lo:row_lo + HD, :] = kt[:, ch * CHUNK:(ch + 1) * CHUNK]

    def put_rk(t, lo):
        for i in range(PW // HD):
            put_keys(lo + i * HD, rope(t, i))

    def conv_silu(t, lo):
        c = _conv3(t, tm, cw_ref[:, lo:lo + PW], cb_ref[:, lo:lo + PW])
        return c * _sigmoid(c)

    def put_mq(t, lo):
        qv_ref[:, 2 * GW + lo:2 * GW + lo + PW] = (conv_silu(t, lo) * scale).astype(BF16)

    def put_mk(t, lo):
        c = conv_silu(t, GW + lo)
        for i in range(PW // HD):
            put_keys(GW + lo + i * HD, c[:, i * HD:(i + 1) * HD])

    def put_gates(t, lo):
        g = t + gb_ref[...]
        acum, b, cmax = _chunk_stats(g[:, :LANES], _log_sigmoid(g[:, LANES:]))
        lane = lax.broadcasted_iota(jnp.int32, acum.shape, 1)
        st_ref[...] = jnp.where(lane < 2 * HEADS, acum, pltpu.roll(cmax, 2 * HEADS, axis=1))
        for ch in range(tm // CHUNK):
            br_ref[ch * SUBLANES:(ch + 1) * SUBLANES, :] = b[ch * CHUNK:(ch + 1) * CHUNK].T[:SUBLANES]

    def put(ref, base):
        def store(t, lo):
            ref[:, base + lo:base + lo + PW] = t.astype(BF16)
        return store

    def pieces(lhs, w_ref, w_lo, consume):
        return [(lhs, w_ref, w_lo + lo, consume, lo) for lo in range(0, GW, PW)]

    rq, rk = pieces(a, w_ref, 0, put_rq), pieces(a, w_ref, GW, put_rk)
    rv, rg = pieces(a, w_ref, 2 * GW, put(qv_ref, GW)), pieces(a, w_ref, 3 * GW, put(og_ref, 0))
    mq, mk = pieces(a_ext, w_ref, 4 * GW, put_mq), pieces(a_ext, w_ref, 5 * GW, put_mk)
    mv, mo = pieces(a, w_ref, 6 * GW, put(qv_ref, 3 * GW)), pieces(a, w_ref, 7 * GW, put(og_ref, GW))
    gates = [(None, None, 0, put_gates, 0)]
    order = [(gates[0],), (mq[0], rv[0]), (rq[0], rv[1]), (mq[1], rg[0]), (rq[1], rg[1]), (mk[0], mv[0]),
             (rk[0], mv[1]), (mk[1], rk[1]), (mo[0], mo[1])]
    project = lambda stage: [_dot(piece[0], piece[1][:, piece[2]:piece[2] + PW]) for piece in stage]
    pending = [gate_proj]
    for k, stage in enumerate(order):
        ready = pending
        if k + 1 < len(order):
            pending = project(order[k + 1])
        for piece, t in zip(stage, ready):
            piece[3](t, piece[4])


def _mixer_kernel(lg_ref, qf_ref, ktf_ref, stf_ref, brf_ref, qb_ref, ktb_ref, stb_ref, brb_ref,
                  of_ref, ob_ref, dm_ref, qdf_ref, qdb_ref, kd_ref, cd_ref, sf_ref, sb_ref, c_ref, m_ref):
    L = CHUNK
    ii = lax.broadcasted_iota(jnp.int32, (L, L), 0)
    jj = lax.broadcasted_iota(jnp.int32, (L, L), 1)

    @pl.when(pl.program_id(1) == 0)
    def _init():
        lg = _log_sigmoid(lg_ref[...])
        lgf, lgb = lg[0:1], lg[1:2]
        row = lax.broadcasted_iota(jnp.int32, (L, GW), 0).astype(F32)
        qdf_ref[...] = jnp.exp((row + 1.0) * lgf)
        qdb_ref[...] = jnp.exp((L - row) * lgb)
        cd_ref[...] = jnp.exp(float(L) * lg)
        diff = (ii - jj).astype(F32)
        col = lax.broadcasted_iota(jnp.int32, (1, L), 1).astype(F32)
        for h in range(HEADS):
            sl = slice(h * HD, (h + 1) * HD)
            dm_ref[h] = jnp.where(diff >= 0, jnp.exp(diff * lgf[:, sl]), jnp.exp(-diff * lgb[:, sl]))
            kd_ref[h:h + 1, :] = jnp.exp((L - 1.0 - col) * lgf[:, sl])
            kd_ref[HEADS + h:HEADS + h + 1, :] = jnp.exp(col * lgb[:, sl])
        sf_ref[...] = jnp.zeros_like(sf_ref)
        sb_ref[...] = jnp.zeros_like(sb_ref)
        c_ref[...] = jnp.zeros_like(c_ref)
        m_ref[...] = jnp.zeros_like(m_ref)

    n_sub = qf_ref.shape[0] // L
    H = range(HEADS)
    C = range(2 * HEADS)
    sls = [slice(h * HD, (h + 1) * HD) for h in H]
    tril = ii >= jj
    triu = jj >= ii
    is_f = jj % (2 * HEADS) < HEADS
    ones_blk = jnp.ones((L, HD), BF16)

    def chunk(i, carry):
        i_b = n_sub - 1 - i
        rows_f = pl.ds(pl.multiple_of(i * L, L), L)
        rows_b = pl.ds(pl.multiple_of(i_b * L, L), L)
        part = lambda ref, rows, g, h: ref[rows, g * GW + h * HD:g * GW + (h + 1) * HD]

        def retention():
            qf, vf = (lambda h: part(qf_ref, rows_f, 0, h)), (lambda h: part(qf_ref, rows_f, 1, h))
            qb, vb = (lambda h: part(qb_ref, rows_b, 0, h)), (lambda h: part(qb_ref, rows_b, 1, h))
            ktf = lambda h: ktf_ref[i, sls[h], :]
            ktb = lambda h: ktb_ref[i_b, sls[h], :]
            qkc = [_dot(qf(h), jnp.concatenate([ktf(h), sf_ref[h].astype(BF16)], axis=1)) for h in H]
            p = [(qkc[h][:, :L] * dm_ref[h]).astype(BF16) for h in H]
            cross_b = [_dot(qb(h), sb_ref[h].astype(BF16)) for h in H]
            kdf = [(ktf(h).astype(F32) * kd_ref[h:h + 1, :]).astype(BF16) for h in H]
            pv = [_dot(jnp.concatenate([p[h], kdf[h]], axis=0), vf(h)) for h in H]
            for h in H:
                of_ref[rows_f, sls[h]] = (pv[h][:L] + qkc[h][:, L:] * qdf_ref[:, sls[h]]).astype(of_ref.dtype)
                ob_ref[rows_b, sls[h]] = (cross_b[h] * qdb_ref[:, sls[h]]).astype(ob_ref.dtype)
            upd_b = [_dot((ktb(h).astype(F32) * kd_ref[HEADS + h:HEADS + h + 1, :]).astype(BF16), vb(h)) for h in H]
            for h in H:
                sf_ref[h] = cd_ref[0:1, sls[h]] * sf_ref[h] + pv[h][L:]
                sb_ref[h] = cd_ref[1:2, sls[h]] * sb_ref[h] + upd_b[h]

        st = jnp.where(is_f, stf_ref[rows_f, :], stb_ref[rows_b, :])
        a_col = lambda c: st[:, c:c + 1]
        cm_col = lambda c: st[:, 2 * HEADS + c:2 * HEADS + c + 1]
        b_rows = jnp.where(lax.broadcasted_iota(jnp.int32, (SUBLANES, L), 0) < HEADS,
                           brf_ref[pl.ds(pl.multiple_of(i * SUBLANES, SUBLANES), SUBLANES), :],
                           brb_ref[pl.ds(pl.multiple_of(i_b * SUBLANES, SUBLANES), SUBLANES), :])

        def mlstm_group(cs):
            back = {c: c >= HEADS for c in cs}
            hd = {c: c % HEADS for c in cs}
            last = {c: 0 if back[c] else L - 1 for c in cs}
            q = lambda c: part(qb_ref, rows_b, 2, hd[c]) if back[c] else part(qf_ref, rows_f, 2, hd[c])
            v = lambda c: part(qb_ref, rows_b, 3, hd[c]) if back[c] else part(qf_ref, rows_f, 3, hd[c])
            kt = lambda c: (ktb_ref[i_b, GW + hd[c] * HD:GW + (hd[c] + 1) * HD, :] if back[c] else
                            ktf_ref[i, GW + hd[c] * HD:GW + (hd[c] + 1) * HD, :])
            v1 = lambda c: jnp.concatenate([v(c), ones_blk], axis=1)
            m_old = {c: m_ref[c:c + 1, 0:1] for c in cs}
            mu = {c: jnp.maximum(jnp.broadcast_to(cm_col(c), (L, L)), m_old[c]) for c in cs}
            e = {c: jnp.exp(jnp.where(triu if back[c] else tril, b_rows[c:c + 1, :] - mu[c], -jnp.inf)) for c in cs}
            qk = {c: _dot(q(c), kt(c)) for c in cs}
            s = {c: (qk[c] * e[c]).astype(BF16) for c in cs}
            qc = {c: _dot(q(c), c_ref[c].astype(BF16)) for c in cs}
            mu_last = {c: mu[c][last[c]:last[c] + 1, 0:1] for c in cs}
            kw = {c: (kt(c).astype(F32) * jnp.exp(b_rows[c:c + 1, :] - mu_last[c])).astype(BF16) for c in cs}
            sv = {c: _dot(jnp.concatenate([s[c], kw[c]], axis=0), v1(c)) for c in cs}
            for c in cs:
                w = jnp.exp(m_old[c] - mu[c])
                floor = jnp.exp(-(jnp.broadcast_to(a_col(c), (L, L)) + mu[c]))
                den = jnp.maximum(jnp.abs(w * qc[c][:, HD:] + sv[c][:L, HD:]), floor)
                dst, rows = (ob_ref, rows_b) if back[c] else (of_ref, rows_f)
                h_c = (w * qc[c][:, :HD] + sv[c][:L, :HD]) / den
                dst[rows, GW + hd[c] * HD:GW + (hd[c] + 1) * HD] = h_c.astype(dst.dtype)
            for c in cs:
                c_ref[c] = jnp.exp(m_old[c] - mu_last[c]) * c_ref[c] + sv[c][L:]
                m_ref[c:c + 1, :] = jnp.broadcast_to(a_col(c)[last[c]:last[c] + 1] + mu_last[c], (1, LANES))

        retention()
        for g in range(0, 2 * HEADS, MLSTM_GROUP):
            mlstm_group(range(g, g + MLSTM_GROUP))
        return carry

    lax.fori_loop(0, n_sub, chunk, 0, unroll=MIX_UNROLL)


def _mix_out_pieces(rows, tin, tiles_per_seq, gn_ref, wo_ref, nfw_ref, emit):
    of_ref, ofn_ref, ofp_ref, ob_ref, obn_ref, obp_ref, og_ref, ogn_ref, ogp_ref, x_ref, xn_ref, xp_ref = rows
    tm = x_ref.shape[0]
    n_ext = tm + 2 * SUBLANES
    v = {}

    def ext(m, nx, pv):
        return jnp.concatenate([m[...].astype(F32), nx[...].astype(F32)[:SUBLANES],
                                pv[...].astype(F32)[pv.shape[0] - SUBLANES:]], axis=0)

    def load():
        v["y"] = ext(of_ref, ofn_ref, ofp_ref) + ext(ob_ref, obn_ref, obp_ref)
        v["z"] = ext(og_ref, ogn_ref, ogp_ref)
        v["h"] = ext(x_ref, xn_ref, xp_ref)
        v["parts"] = []

    def head(g):
        sl = slice(g * HD, (g + 1) * HD)
        yg = v["y"][:, sl]
        d = yg - jnp.mean(yg, axis=-1, keepdims=True)
        n = d * lax.rsqrt(jnp.mean(d * d, axis=-1, keepdims=True) + HN_EPS) * gn_ref[:, sl]
        zg = v["z"][:, sl]
        gate = zg * _sigmoid(zg) if g < HEADS else _sigmoid(zg)
        v["parts"].append((n * gate).astype(BF16))
        if g % 2 == 1:
            pair = jnp.concatenate(v["parts"][-2:], axis=1)
            v["h"] = v["h"] + _dot(pair, wo_ref[(g - 1) * HD:(g + 1) * HD, :])

    def finish():
        r = lax.broadcasted_iota(jnp.int32, (n_ext, 1), 0)
        pad = ((r >= tm) & (r < tm + SUBLANES) & (tin == tiles_per_seq - 1)) | ((r >= tm + SUBLANES) & (tin == 0))
        emit(jnp.where(pad, 0.0, _rms(v["h"], nfw_ref[...])).astype(BF16), v["h"][:tm])

    return [load] + [functools.partial(head, g) for g in range(2 * HEADS)] + [finish]


def _out_ffn_kernel(*refs, tiles_per_seq):
    rows = refs[:12]
    (p_ref, gn_ref, wo_ref, nfw_ref, wg_ref, wu_ref, cw_ref, cb_ref, wd_ref, pnw_ref, pwg_ref, pgb_ref, pwp_ref, fnw_ref,
     y_ref, act_ref) = refs[12:]
    tm = p_ref.shape[0]
    stage = {}
    tin = pl.program_id(0) % tiles_per_seq
    for piece in _mix_out_pieces(rows, tin, tiles_per_seq, gn_ref, wo_ref, nfw_ref,
                                 lambda f_ext, h: stage.update(f_ext=f_ext, h=h)):
        piece()
    f_ext = stage["f_ext"]
    f = f_ext[:tm]

    fb = FF_BLOCK
    nblk = wg_ref.shape[1] // fb
    blk = lambda ref, j: ref[:, j * fb:(j + 1) * fb]
    proj = lambda j: (_dot(f_ext, blk(wg_ref, j)), _dot(f, blk(wu_ref, j)))
    nxt = proj(0)
    for j in range(nblk):
        u_ext, up = nxt
        if j + 1 < nblk:
            nxt = proj(j + 1)
        act_ref[:, j * fb:(j + 1) * fb] = (_gelu(_conv3(u_ext, tm, blk(cw_ref, j), blk(cb_ref, j))) * up).astype(BF16)
    h = stage["h"] + _dot(act_ref[...], wd_ref[...])
    halves = [slice(0, tm // 2), slice(tm // 2, tm)]
    gate_pre = [_dot(_rms(h[r], pnw_ref[...]).astype(BF16), pwg_ref[...]) for r in halves]
    ple = [_dot(p_ref[r, :].astype(BF16), pwp_ref[...]) for r in halves]
    for r, g, e in zip(halves, gate_pre, ple):
        y_ref[r, :] = _rms(h[r] + e * _sigmoid(g + pgb_ref[...]), fnw_ref[...])


def _const_spec(shape):
    nd = len(shape)
    return pl.BlockSpec(shape, lambda *_: (0,) * nd, pipeline_mode=pl.Buffered(1))


def _row_specs(tm, width, n_rows, halo=SUBLANES):
    per = tm // halo
    last = n_rows // halo - 1
    return [
        pl.BlockSpec((tm, width), lambda i: (i, 0)),
        pl.BlockSpec((halo, width), lambda i: (jnp.minimum((i + 1) * per, last), 0)),
        pl.BlockSpec((halo, width), lambda i: (jnp.maximum(i * per - 1, 0), 0)),
    ]


def _params(sem):
    return pltpu.CompilerParams(dimension_semantics=sem, vmem_limit_bytes=VMEM_LIMIT)


def _layer(x, p, w):
    bsz, seq, d = x.shape
    n = bsz * seq
    tm = ROW_TILE
    assert seq % tm == 0 and seq % CHUNK == 0
    tiles_per_seq = seq // tm
    x2 = x.reshape(n, d)
    row = lambda width: pl.BlockSpec((tm, width), lambda i: (i, 0))

    rows_per_chunk = SUBLANES
    ti = IN_TILE
    assert seq % ti == 0
    row_i = lambda width: pl.BlockSpec((ti, width), lambda i: (i, 0))
    pos_tile = pl.BlockSpec((ti, HD), lambda i: (i % (seq // ti), 0))
    qv, kt, og, stats, brow = pl.pallas_call(
        functools.partial(_in_proj_kernel, tiles_per_seq=seq // ti),
        grid=(n // ti,),
        in_specs=_row_specs(ti, d, n) + [pos_tile, pos_tile] + [_const_spec(a.shape) for a in w["in_proj"]],
        out_specs=[row_i(4 * GW), pl.BlockSpec((ti // CHUNK, 2 * GW, CHUNK), lambda i: (i, 0, 0)), row_i(2 * GW),
                   row_i(LANES), pl.BlockSpec((ti // CHUNK * rows_per_chunk, LANES), lambda i: (i, 0))],
        out_shape=[jax.ShapeDtypeStruct((n, 4 * GW), BF16), jax.ShapeDtypeStruct((n // CHUNK, 2 * GW, CHUNK), BF16),
                   jax.ShapeDtypeStruct((n, 2 * GW), BF16), jax.ShapeDtypeStruct((n, LANES), F32),
                   jax.ShapeDtypeStruct((n // CHUNK * rows_per_chunk, LANES), F32)],
        compiler_params=_params(("parallel",)),
        name="in_proj",
    )(x2, x2, x2, *w["rope"], *w["in_proj"])

    cb = MIX_CHUNKS
    assert seq % (cb * CHUNK) == 0
    nb = seq // (cb * CHUNK)
    fwd_i = lambda b, t: b * nb + t
    bwd_i = lambda b, t: b * nb + nb - 1 - t

    def block_specs(idx):
        return [pl.BlockSpec((cb * CHUNK, 4 * GW), lambda b, t: (idx(b, t), 0)),
                pl.BlockSpec((cb, 2 * GW, CHUNK), lambda b, t: (idx(b, t), 0, 0)),
                pl.BlockSpec((cb * CHUNK, LANES), lambda b, t: (idx(b, t), 0)),
                pl.BlockSpec((cb * rows_per_chunk, LANES), lambda b, t: (idx(b, t), 0))]

    out_f, out_b = pl.pallas_call(
        _mixer_kernel,
        grid=(bsz, nb),
        in_specs=[_const_spec(w["ret_logit"].shape)] + block_specs(fwd_i) + block_specs(bwd_i),
        out_specs=[pl.BlockSpec((cb * CHUNK, 2 * GW), lambda b, t: (fwd_i(b, t), 0)),
                   pl.BlockSpec((cb * CHUNK, 2 * GW), lambda b, t: (bwd_i(b, t), 0))],
        out_shape=[jax.ShapeDtypeStruct((n, 2 * GW), BF16)] * 2,
        scratch_shapes=[
            pltpu.VMEM((HEADS, CHUNK, CHUNK), F32),
            pltpu.VMEM((CHUNK, GW), F32), pltpu.VMEM((CHUNK, GW), F32),
            pltpu.VMEM((2 * HEADS, CHUNK), F32),
            pltpu.VMEM((2, GW), F32),
            pltpu.VMEM((HEADS, HD, HD), F32), pltpu.VMEM((HEADS, HD, HD), F32),
            pltpu.VMEM((2 * HEADS, HD, 2 * HD), F32),
            pltpu.VMEM((2 * HEADS, LANES), F32),
        ],
        compiler_params=_params(("arbitrary", "arbitrary")),
        name="mixers",
    )(w["ret_logit"], *([qv, kt, stats, brow] * 2))

    p2 = p.reshape(n, p.shape[-1])
    bf16_rows = 2 * SUBLANES
    mix_rows = _row_specs(tm, d, n, bf16_rows) * 3 + _row_specs(tm, d, n)
    y = pl.pallas_call(
        functools.partial(_out_ffn_kernel, tiles_per_seq=tiles_per_seq),
        grid=(n // tm,),
        in_specs=mix_rows + [row(p2.shape[-1])] + [_const_spec(a.shape) for a in (*w["mix_out"], *w["ffn"])],
        out_specs=row(d),
        out_shape=jax.ShapeDtypeStruct((n, d), F32),
        scratch_shapes=[pltpu.VMEM((tm, w["d_ff"]), BF16)],
        compiler_params=_params(("parallel",)),
        name="out_ffn",
    )(*((out_f,) * 3 + (out_b,) * 3 + (og,) * 3 + (x2,) * 3), p2, *w["mix_out"], *w["ffn"])
    return y.reshape(bsz, seq, d)


def kernel(x_prompt, x_sample, p_prompt, p_sample, norm_mix_w, w_in, mlstm_conv_w, mlstm_conv_b, mlstm_gate_b, ret_decay_logit, ret_gn_w, mlstm_gn_w, w_out, norm_ffn_w, ffn_w_gate, ffn_w_up, ffn_conv_w, ffn_conv_b, ffn_w_down, ple_w_proj, ple_norm_w, ple_w_gate, ple_gate_b, final_norm_w):
    depth = w_in.shape[0]
    d = x_prompt.shape[-1]
    d_ff = ffn_w_gate.shape[-1]
    assert d_ff % FF_BLOCK == 0
    rowv = lambda a: a.reshape(1, -1).astype(F32)
    inv = ROPE_BASE ** (-jnp.arange(0, HD, 2, dtype=F32) / HD)
    inv = jnp.concatenate([inv, inv]).reshape(1, HD)
    max_seq = max(x_prompt.shape[1], x_sample.shape[1])
    assert max_seq % ROW_TILE == 0
    rope = pl.pallas_call(
        _rope_table_kernel,
        grid=(max_seq // ROW_TILE,),
        in_specs=[_const_spec(inv.shape)],
        out_specs=[pl.BlockSpec((ROW_TILE, HD), lambda i: (i, 0))] * 2,
        out_shape=[jax.ShapeDtypeStruct((max_seq, HD), F32)] * 2,
        scratch_shapes=[pltpu.VMEM((ROW_TILE, HD), F32)] * 2,
        compiler_params=_params(("arbitrary",)),
        name="rope_tables",
    )(inv)

    assert depth == 1
    outs = [x_prompt, x_sample]
    for l in range(depth):
        wi = w_in[l].astype(BF16)
        n_gate = 2 * HEADS
        wgate = jnp.zeros((d, 2 * LANES), BF16)
        wgate = wgate.at[:, 0:n_gate].set(wi[:, 8 * GW:8 * GW + n_gate])
        wgate = wgate.at[:, LANES:LANES + n_gate].set(wi[:, 8 * GW + n_gate:8 * GW + 2 * n_gate])
        gb = mlstm_gate_b[l].astype(F32).reshape(-1)
        gbias = jnp.zeros((1, 2 * LANES), F32)
        gbias = gbias.at[0, 0:n_gate].set(gb[:n_gate]).at[0, LANES:LANES + n_gate].set(gb[n_gate:])
        w = {
            "rope": rope,
            "in_proj": [rowv(norm_mix_w[l]), wi, wgate, gbias, mlstm_conv_w[l].astype(F32), rowv(mlstm_conv_b[l])],
            "ret_logit": jnp.repeat(ret_decay_logit[l].astype(F32), HD, axis=1),
            "d_ff": d_ff,
            "mix_out": [jnp.concatenate([rowv(ret_gn_w[l]), rowv(mlstm_gn_w[l])], axis=1), w_out[l].astype(BF16),
                        rowv(norm_ffn_w[l])],
            "ffn": [ffn_w_gate[l].astype(BF16), ffn_w_up[l].astype(BF16), ffn_conv_w[l].astype(F32),
                    rowv(ffn_conv_b[l]), ffn_w_down[l].astype(BF16), rowv(ple_norm_w[l]), ple_w_gate[l].astype(BF16),
                    rowv(ple_gate_b[l]), ple_w_proj[l].astype(BF16), rowv(final_norm_w)],
        }
        outs = [_layer(h, p[l], w) for h, p in zip(outs, (p_prompt, p_sample))]
    return tuple(outs)
```

```python
import functools

import jax
import jax.numpy as jnp
from jax import lax
from jax.experimental import pallas as pl
from jax.experimental.pallas import tpu as pltpu

F32 = jnp.float32
BF16 = jnp.bfloat16

HEADS = 4
HD = 128
GW = HEADS * HD
CHUNK = 128
ROPE_BASE = 10000.0
RMS_EPS = 1e-6
HN_EPS = 1e-5
SUBLANES = 8
LANES = 128
ROW_TILE = 512
IN_TILE = 1024
FF_BLOCK = 512
MIX_CHUNKS = 8
MIX_UNROLL = 4
MLSTM_GROUP = 4
VMEM_LIMIT = 56 * 1024 * 1024


def _rms(x, w):
    return x * lax.rsqrt(jnp.mean(x * x, axis=-1, keepdims=True) + RMS_EPS) * w


def _rms_project(x, w_ref, projections, kb):
    inv = lax.rsqrt(jnp.mean(x * x, axis=-1, keepdims=True) + RMS_EPS)
    blocks, outs = [], [None] * len(projections)
    for k in range(0, x.shape[1], kb):
        xk = (x[:, k:k + kb] * inv * w_ref[:, k:k + kb]).astype(BF16)
        blocks.append(xk)
        for i, (rows, p_ref, lo, width) in enumerate(projections):
            part = _dot(xk[:rows], p_ref[k:k + kb, lo:lo + width])
            outs[i] = part if outs[i] is None else outs[i] + part
    return jnp.concatenate(blocks, axis=1), outs


def _log_sigmoid(x):
    return jnp.minimum(x, 0.0) - jnp.log1p(jnp.exp(-jnp.abs(x)))


def _sigmoid(x):
    return 1.0 / (1.0 + jnp.exp(-x))


def _gelu(x):
    return 0.5 * x * (1.0 + lax.erf(x * (0.5 ** 0.5)))


def _dot(a, b):
    return jnp.dot(a, b, preferred_element_type=F32)


def _ext_rows(main, nxt, prv, tin, tiles_per_seq):
    nxt = jnp.where(tin == tiles_per_seq - 1, 0.0, nxt)
    prv = jnp.where(tin == 0, 0.0, prv)
    return jnp.concatenate([main, nxt, prv], axis=0)


def _conv3(ext, rows, w, b):
    n = ext.shape[0]
    up = pltpu.roll(ext, 1, axis=0)[:rows]
    dn = pltpu.roll(ext, n - 1, axis=0)[:rows]
    return up * w[0:1] + ext[:rows] * w[1:2] + dn * w[2:3] + b


def _chunk_stats(ig, lf):
    tm = ig.shape[0]
    ii = lax.broadcasted_iota(jnp.int32, (CHUNK, CHUNK), 0)
    jj = lax.broadcasted_iota(jnp.int32, (CHUNK, CHUNK), 1)
    tril = (ii >= jj).astype(F32)
    triu = (jj >= ii).astype(F32)
    hi = lax.Precision.HIGHEST
    a = jnp.concatenate([
        jnp.where(jj < HEADS,
                  jnp.dot(tril, lf[r:r + CHUNK], precision=hi, preferred_element_type=F32),
                  jnp.dot(triu, lf[r:r + CHUNK], precision=hi, preferred_element_type=F32))
        for r in range(0, tm, CHUNK)], axis=0)
    b = ig - a
    r = lax.broadcasted_iota(jnp.int32, (tm, LANES), 0) % CHUNK
    pre = suf = b
    s = 1
    while s < CHUNK:
        pre = jnp.maximum(pre, jnp.where(r >= s, pltpu.roll(pre, s, axis=0), -jnp.inf))
        suf = jnp.maximum(suf, jnp.where(r < CHUNK - s, pltpu.roll(suf, tm - s, axis=0), -jnp.inf))
        s *= 2
    cm = jnp.where(lax.broadcasted_iota(jnp.int32, (tm, LANES), 1) < HEADS, pre, suf)
    return a, b, cm


def _rope_table_kernel(inv_ref, cos_ref, sin_ref, cr_ref, sr_ref):
    tm = cos_ref.shape[0]

    @pl.when(pl.program_id(0) == 0)
    def _offsets():
        off = lax.broadcasted_iota(jnp.int32, (tm, HD), 0).astype(F32) * inv_ref[...]
        cr_ref[...] = jnp.cos(off)
        sr_ref[...] = jnp.sin(off)

    base = (pl.program_id(0) * tm).astype(F32) * inv_ref[...]
    cb, sb = jnp.cos(base), jnp.sin(base)
    cos_ref[...] = cb * cr_ref[...] - sb * sr_ref[...]
    sin = sb * cr_ref[...] + cb * sr_ref[...]
    lane = lax.broadcasted_iota(jnp.int32, (tm, HD), 1)
    sin_ref[...] = jnp.where(lane < HD // 2, -sin, sin)


def _in_proj_kernel(x_ref, xn_ref, xp_ref, cos_ref, sin_ref, nw_ref, w_ref, wg_ref, gb_ref, cw_ref, cb_ref,
                    qv_ref, kt_ref, og_ref, st_ref, br_ref, *, tiles_per_seq):
    tm = x_ref.shape[0]
    tin = pl.program_id(0) % tiles_per_seq
    x_ext = _ext_rows(x_ref[...], xn_ref[...], xp_ref[...], tin, tiles_per_seq)
    a_ext, (gate_proj,) = _rms_project(x_ext, nw_ref, [(tm, wg_ref, 0, wg_ref.shape[1])], 2 * LANES)
    a = a_ext[:tm]
    scale = HD ** -0.5

    PW = 2 * HD

    def rope(t, i):
        th = t[:, i * HD:(i + 1) * HD]
        return th * cos_ref[...] + pltpu.roll(th, HD // 2, axis=1) * sin_ref[...]

    def put_rq(t, lo):
        for i in range(PW // HD):
            qv_ref[:, lo + i * HD:lo + (i + 1) * HD] = (rope(t, i) * scale).astype(BF16)

    def put_keys(row_lo, k):
        kt = k.T.astype(BF16)
        for ch in range(tm // CHUNK):
            kt_ref[ch, row_lo:row_lo + HD, :] = kt[:, ch * CHUNK:(ch + 1) * CHUNK]

    def put_rk(t, lo):
        for i in range(PW // HD):
            put_keys(lo + i * HD, rope(t, i))

    def conv_silu(t, lo):
        c = _conv3(t, tm, cw_ref[:, lo:lo + PW], cb_ref[:, lo:lo + PW])
        return c * _sigmoid(c)

    def put_mq(t, lo):
        qv_ref[:, 2 * GW + lo:2 * GW + lo + PW] = (conv_silu(t, lo) * scale).astype(BF16)

    def put_mk(t, lo):
        c = conv_silu(t, GW + lo)
        for i in range(PW // HD):
            put_keys(GW + lo + i * HD, c[:, i * HD:(i + 1) * HD])

    def put_gates(t, lo):
        g = t + gb_ref[...]
        acum, b, cmax = _chunk_stats(g[:, :LANES], _log_sigmoid(g[:, LANES:]))
        lane = lax.broadcasted_iota(jnp.int32, acum.shape, 1)
        st_ref[...] = jnp.where(lane < 2 * HEADS, acum, pltpu.roll(cmax, 2 * HEADS, axis=1))
        for ch in range(tm // CHUNK):
            br_ref[ch * SUBLANES:(ch + 1) * SUBLANES, :] = b[ch * CHUNK:(ch + 1) * CHUNK].T[:SUBLANES]

    def put(ref, base):
        def store(t, lo):
            ref[:, base + lo:base + lo + PW] = t.astype(BF16)
        return store

    def pieces(lhs, w_ref, w_lo, consume):
        return [(lhs, w_ref, w_lo + lo, consume, lo) for lo in range(0, GW, PW)]

    rq, rk = pieces(a, w_ref, 0, put_rq), pieces(a, w_ref, GW, put_rk)
    rv, rg = pieces(a, w_ref, 2 * GW, put(qv_ref, GW)), pieces(a, w_ref, 3 * GW, put(og_ref, 0))
    mq, mk = pieces(a_ext, w_ref, 4 * GW, put_mq), pieces(a_ext, w_ref, 5 * GW, put_mk)
    mv, mo = pieces(a, w_ref, 6 * GW, put(qv_ref, 3 * GW)), pieces(a, w_ref, 7 * GW, put(og_ref, GW))
    gates = [(None, None, 0, put_gates, 0)]
    order = [(gates[0],), (mq[0], rv[0]), (rq[0], rv[1]), (mq[1], rg[0]), (rq[1], rg[1]), (mk[0], mv[0]),
             (rk[0], mv[1]), (mk[1], rk[1]), (mo[0], mo[1])]
    project = lambda stage: [_dot(piece[0], piece[1][:, piece[2]:piece[2] + PW]) for piece in stage]
    pending = [gate_proj]
    for k, stage in enumerate(order):
        ready = pending
        if k + 1 < len(order):
            pending = project(order[k + 1])
        for piece, t in zip(stage, ready):
            piece[3](t, piece[4])


def _mixer_kernel(lg_ref, qf_ref, ktf_ref, stf_ref, brf_ref, qb_ref, ktb_ref, stb_ref, brb_ref,
                  of_ref, ob_ref, dm_ref, qdf_ref, qdb_ref, kd_ref, cd_ref, sf_ref, sb_ref, c_ref, m_ref):
    L = CHUNK
    ii = lax.broadcasted_iota(jnp.int32, (L, L), 0)
    jj = lax.broadcasted_iota(jnp.int32, (L, L), 1)

    @pl.when(pl.program_id(1) == 0)
    def _init():
        lg = _log_sigmoid(lg_ref[...])
        lgf, lgb = lg[0:1], lg[1:2]
        row = lax.broadcasted_iota(jnp.int32, (L, GW), 0).astype(F32)
        qdf_ref[...] = jnp.exp((row + 1.0) * lgf)
        qdb_ref[...] = jnp.exp((L - row) * lgb)
        cd_ref[...] = jnp.exp(float(L) * lg)
        diff = (ii - jj).astype(F32)
        col = lax.broadcasted_iota(jnp.int32, (1, L), 1).astype(F32)
        for h in range(HEADS):
            sl = slice(h * HD, (h + 1) * HD)
            dm_ref[h] = jnp.where(diff >= 0, jnp.exp(diff * lgf[:, sl]), jnp.exp(-diff * lgb[:, sl]))
            kd_ref[h:h + 1, :] = jnp.exp((L - 1.0 - col) * lgf[:, sl])
            kd_ref[HEADS + h:HEADS + h + 1, :] = jnp.exp(col * lgb[:, sl])
        sf_ref[...] = jnp.zeros_like(sf_ref)
        sb_ref[...] = jnp.zeros_like(sb_ref)
        c_ref[...] = jnp.zeros_like(c_ref)
        m_ref[...] = jnp.zeros_like(m_ref)

    n_sub = qf_ref.shape[0] // L
    H = range(HEADS)
    C = range(2 * HEADS)
    sls = [slice(h * HD, (h + 1) * HD) for h in H]
    tril = ii >= jj
    triu = jj >= ii
    is_f = jj % (2 * HEADS) < HEADS
    ones_blk = jnp.ones((L, HD), BF16)

    def chunk(i, carry):
        i_b = n_sub - 1 - i
        rows_f = pl.ds(pl.multiple_of(i * L, L), L)
        rows_b = pl.ds(pl.multiple_of(i_b * L, L), L)
        part = lambda ref, rows, g, h: ref[rows, g * GW + h * HD:g * GW + (h + 1) * HD]

        def retention():
            qf, vf = (lambda h: part(qf_ref, rows_f, 0, h)), (lambda h: part(qf_ref, rows_f, 1, h))
            qb, vb = (lambda h: part(qb_ref, rows_b, 0, h)), (lambda h: part(qb_ref, rows_b, 1, h))
            ktf = lambda h: ktf_ref[i, sls[h], :]
            ktb = lambda h: ktb_ref[i_b, sls[h], :]
            qkc = [_dot(qf(h), jnp.concatenate([ktf(h), sf_ref[h].astype(BF16)], axis=1)) for h in H]
            p = [(qkc[h][:, :L] * dm_ref[h]).astype(BF16) for h in H]
            cross_b = [_dot(qb(h), sb_ref[h].astype(BF16)) for h in H]
            kdf = [(ktf(h).astype(F32) * kd_ref[h:h + 1, :]).astype(BF16) for h in H]
            pv = [_dot(jnp.concatenate([p[h], kdf[h]], axis=0), vf(h)) for h in H]
            for h in H:
                of_ref[rows_f, sls[h]] = (pv[h][:L] + qkc[h][:, L:] * qdf_ref[:, sls[h]]).astype(of_ref.dtype)
                ob_ref[rows_b, sls[h]] = (cross_b[h] * qdb_ref[:, sls[h]]).astype(ob_ref.dtype)
            upd_b = [_dot((ktb(h).astype(F32) * kd_ref[HEADS + h:HEADS + h + 1, :]).astype(BF16), vb(h)) for h in H]
            for h in H:
                sf_ref[h] = cd_ref[0:1, sls[h]] * sf_ref[h] + pv[h][L:]
                sb_ref[h] = cd_ref[1:2, sls[h]] * sb_ref[h] + upd_b[h]

        st = jnp.where(is_f, stf_ref[rows_f, :], stb_ref[rows_b, :])
        a_col = lambda c: st[:, c:c + 1]
        cm_col = lambda c: st[:, 2 * HEADS + c:2 * HEADS + c + 1]
        b_rows = jnp.where(lax.broadcasted_iota(jnp.int32, (SUBLANES, L), 0) < HEADS,
                           brf_ref[pl.ds(pl.multiple_of(i * SUBLANES, SUBLANES), SUBLANES), :],
                           brb_ref[pl.ds(pl.multiple_of(i_b * SUBLANES, SUBLANES), SUBLANES), :])

        def mlstm_group(cs):
            back = {c: c >= HEADS for c in cs}
            hd = {c: c % HEADS for c in cs}
            last = {c: 0 if back[c] else L - 1 for c in cs}
            q = lambda c: part(qb_ref, rows_b, 2, hd[c]) if back[c] else part(qf_ref, rows_f, 2, hd[c])
            v = lambda c: part(qb_ref, rows_b, 3, hd[c]) if back[c] else part(qf_ref, rows_f, 3, hd[c])
            kt = lambda c: (ktb_ref[i_b, GW + hd[c] * HD:GW + (hd[c] + 1) * HD, :] if back[c] else
                            ktf_ref[i, GW + hd[c] * HD:GW + (hd[c] + 1) * HD, :])
            v1 = lambda c: jnp.concatenate([v(c), ones_blk], axis=1)
            m_old = {c: m_ref[c:c + 1, 0:1] for c in cs}
            mu = {c: jnp.maximum(jnp.broadcast_to(cm_col(c), (L, L)), m_old[c]) for c in cs}
            e = {c: jnp.exp(jnp.where(triu if back[c] else tril, b_rows[c:c + 1, :] - mu[c], -jnp.inf)) for c in cs}
            qk = {c: _dot(q(c), kt(c)) for c in cs}
            s = {c: (qk[c] * e[c]).astype(BF16) for c in cs}
            qc = {c: _dot(q(c), c_ref[c].astype(BF16)) for c in cs}
            mu_last = {c: mu[c][last[c]:last[c] + 1, 0:1] for c in cs}
            kw = {c: (kt(c).astype(F32) * jnp.exp(b_rows[c:c + 1, :] - mu_last[c])).astype(BF16) for c in cs}
            sv = {c: _dot(jnp.concatenate([s[c], kw[c]], axis=0), v1(c)) for c in cs}
            for c in cs:
                w = jnp.exp(m_old[c] - mu[c])
                floor = jnp.exp(-(jnp.broadcast_to(a_col(c), (L, L)) + mu[c]))
                den = jnp.maximum(jnp.abs(w * qc[c][:, HD:] + sv[c][:L, HD:]), floor)
                dst, rows = (ob_ref, rows_b) if back[c] else (of_ref, rows_f)
                h_c = (w * qc[c][:, :HD] + sv[c][:L, :HD]) / den
                dst[rows, GW + hd[c] * HD:GW + (hd[c] + 1) * HD] = h_c.astype(dst.dtype)
            for c in cs:
                c_ref[c] = jnp.exp(m_old[c] - mu_last[c]) * c_ref[c] + sv[c][L:]
                m_ref[c:c + 1, :] = jnp.broadcast_to(a_col(c)[last[c]:last[c] + 1] + mu_last[c], (1, LANES))

        retention()
        for g in range(0, 2 * HEADS, MLSTM_GROUP):
            mlstm_group(range(g, g + MLSTM_GROUP))
        return carry

    lax.fori_loop(0, n_sub, chunk, 0, unroll=MIX_UNROLL)


def _mix_out_pieces(rows, tin, tiles_per_seq, gn_ref, wo_ref, nfw_ref, emit):
    of_ref, ofn_ref, ofp_ref, ob_ref, obn_ref, obp_ref, og_ref, ogn_ref, ogp_ref, x_ref, xn_ref, xp_ref = rows
    tm = x_ref.shape[0]
    n_ext = tm + 2 * SUBLANES
    v = {}

    def ext(m, nx, pv):
        return jnp.concatenate([m[...].astype(F32), nx[...].astype(F32)[:SUBLANES],
                                pv[...].astype(F32)[pv.shape[0] - SUBLANES:]], axis=0)

    def load():
        v["y"] = ext(of_ref, ofn_ref, ofp_ref) + ext(ob_ref, obn_ref, obp_ref)
        v["z"] = ext(og_ref, ogn_ref, ogp_ref)
        v["h"] = ext(x_ref, xn_ref, xp_ref)
        v["parts"] = []

    def head(g):
        sl = slice(g * HD, (g + 1) * HD)
        yg = v["y"][:, sl]
        d = yg - jnp.mean(yg, axis=-1, keepdims=True)
        n = d * lax.rsqrt(jnp.mean(d * d, axis=-1, keepdims=True) + HN_EPS) * gn_ref[:, sl]
        zg = v["z"][:, sl]
        gate = zg * _sigmoid(zg) if g < HEADS else _sigmoid(zg)
        v["parts"].append((n * gate).astype(BF16))
        if g % 2 == 1:
            pair = jnp.concatenate(v["parts"][-2:], axis=1)
            v["h"] = v["h"] + _dot(pair, wo_ref[(g - 1) * HD:(g + 1) * HD, :])

    def finish():
        r = lax.broadcasted_iota(jnp.int32, (n_ext, 1), 0)
        pad = ((r >= tm) & (r < tm + SUBLANES) & (tin == tiles_per_seq - 1)) | ((r >= tm + SUBLANES) & (tin == 0))
        emit(jnp.where(pad, 0.0, _rms(v["h"], nfw_ref[...])).astype(BF16), v["h"][:tm])

    return [load] + [functools.partial(head, g) for g in range(2 * HEADS)] + [finish]


def _out_ffn_kernel(*refs, tiles_per_seq):
    rows = refs[:12]
    (p_ref, gn_ref, wo_ref, nfw_ref, wg_ref, wu_ref, cw_ref, cb_ref, wd_ref, pnw_ref, pwg_ref, pgb_ref, pwp_ref, fnw_ref,
     y_ref, act_ref) = refs[12:]
    tm = p_ref.shape[0]
    stage = {}
    tin = pl.program_id(0) % tiles_per_seq
    for piece in _mix_out_pieces(rows, tin, tiles_per_seq, gn_ref, wo_ref, nfw_ref,
                                 lambda f_ext, h: stage.update(f_ext=f_ext, h=h)):
        piece()
    f_ext = stage["f_ext"]
    f = f_ext[:tm]

    d_ff = wg_ref.shape[1]
    bounds = [(lo, min(lo + FF_BLOCK, d_ff)) for lo in range(0, d_ff, FF_BLOCK)]
    nblk = len(bounds)
    blk = lambda ref, j: ref[:, bounds[j][0]:bounds[j][1]]
    proj = lambda j: (_dot(f_ext, blk(wg_ref, j)), _dot(f, blk(wu_ref, j)))
    nxt = proj(0)
    for j in range(nblk):
        u_ext, up = nxt
        if j + 1 < nblk:
            nxt = proj(j + 1)
        act = _gelu(_conv3(u_ext, tm, blk(cw_ref, j), blk(cb_ref, j))) * up
        act_ref[:, bounds[j][0]:bounds[j][1]] = act.astype(BF16)
    h = stage["h"] + _dot(act_ref[...], wd_ref[...])
    halves = [slice(0, tm // 2), slice(tm // 2, tm)]
    gate_pre = [_dot(_rms(h[r], pnw_ref[...]).astype(BF16), pwg_ref[...]) for r in halves]
    ple = [_dot(p_ref[r, :].astype(BF16), pwp_ref[...]) for r in halves]
    for r, g, e in zip(halves, gate_pre, ple):
        y_ref[r, :] = _rms(h[r] + e * _sigmoid(g + pgb_ref[...]), fnw_ref[...])


def _const_spec(shape):
    nd = len(shape)
    return pl.BlockSpec(shape, lambda *_: (0,) * nd, pipeline_mode=pl.Buffered(1))


def _row_specs(tm, width, n_rows, halo=SUBLANES):
    per = tm // halo
    last = n_rows // halo - 1
    return [
        pl.BlockSpec((tm, width), lambda i: (i, 0)),
        pl.BlockSpec((halo, width), lambda i: (jnp.minimum((i + 1) * per, last), 0)),
        pl.BlockSpec((halo, width), lambda i: (jnp.maximum(i * per - 1, 0), 0)),
    ]


def _params(sem):
    return pltpu.CompilerParams(dimension_semantics=sem, vmem_limit_bytes=VMEM_LIMIT)


def _layer(x, p, w):
    bsz, seq, d = x.shape
    n = bsz * seq
    tm = ROW_TILE
    assert seq % tm == 0 and seq % CHUNK == 0
    tiles_per_seq = seq // tm
    x2 = x.reshape(n, d)
    row = lambda width: pl.BlockSpec((tm, width), lambda i: (i, 0))

    rows_per_chunk = SUBLANES
    ti = IN_TILE
    assert seq % ti == 0
    row_i = lambda width: pl.BlockSpec((ti, width), lambda i: (i, 0))
    pos_tile = pl.BlockSpec((ti, HD), lambda i: (i % (seq // ti), 0))
    qv, kt, og, stats, brow = pl.pallas_call(
        functools.partial(_in_proj_kernel, tiles_per_seq=seq // ti),
        grid=(n // ti,),
        in_specs=_row_specs(ti, d, n) + [pos_tile, pos_tile] + [_const_spec(a.shape) for a in w["in_proj"]],
        out_specs=[row_i(4 * GW), pl.BlockSpec((ti // CHUNK, 2 * GW, CHUNK), lambda i: (i, 0, 0)), row_i(2 * GW),
                   row_i(LANES), pl.BlockSpec((ti // CHUNK * rows_per_chunk, LANES), lambda i: (i, 0))],
        out_shape=[jax.ShapeDtypeStruct((n, 4 * GW), BF16), jax.ShapeDtypeStruct((n // CHUNK, 2 * GW, CHUNK), BF16),
                   jax.ShapeDtypeStruct((n, 2 * GW), BF16), jax.ShapeDtypeStruct((n, LANES), F32),
                   jax.ShapeDtypeStruct((n // CHUNK * rows_per_chunk, LANES), F32)],
        compiler_params=_params(("parallel",)),
        name="in_proj",
    )(x2, x2, x2, *w["rope"], *w["in_proj"])

    cb = MIX_CHUNKS
    assert seq % (cb * CHUNK) == 0
    nb = seq // (cb * CHUNK)
    fwd_i = lambda b, t: b * nb + t
    bwd_i = lambda b, t: b * nb + nb - 1 - t

    def block_specs(idx):
        return [pl.BlockSpec((cb * CHUNK, 4 * GW), lambda b, t: (idx(b, t), 0)),
                pl.BlockSpec((cb, 2 * GW, CHUNK), lambda b, t: (idx(b, t), 0, 0)),
                pl.BlockSpec((cb * CHUNK, LANES), lambda b, t: (idx(b, t), 0)),
                pl.BlockSpec((cb * rows_per_chunk, LANES), lambda b, t: (idx(b, t), 0))]

    out_f, out_b = pl.pallas_call(
        _mixer_kernel,
        grid=(bsz, nb),
        in_specs=[_const_spec(w["ret_logit"].shape)] + block_specs(fwd_i) + block_specs(bwd_i),
        out_specs=[pl.BlockSpec((cb * CHUNK, 2 * GW), lambda b, t: (fwd_i(b, t), 0)),
                   pl.BlockSpec((cb * CHUNK, 2 * GW), lambda b, t: (bwd_i(b, t), 0))],
        out_shape=[jax.ShapeDtypeStruct((n, 2 * GW), BF16)] * 2,
        scratch_shapes=[
            pltpu.VMEM((HEADS, CHUNK, CHUNK), F32),
            pltpu.VMEM((CHUNK, GW), F32), pltpu.VMEM((CHUNK, GW), F32),
            pltpu.VMEM((2 * HEADS, CHUNK), F32),
            pltpu.VMEM((2, GW), F32),
            pltpu.VMEM((HEADS, HD, HD), F32), pltpu.VMEM((HEADS, HD, HD), F32),
            pltpu.VMEM((2 * HEADS, HD, 2 * HD), F32),
            pltpu.VMEM((2 * HEADS, LANES), F32),
        ],
        compiler_params=_params(("arbitrary", "arbitrary")),
        name="mixers",
    )(w["ret_logit"], *([qv, kt, stats, brow] * 2))

    p2 = p.reshape(n, p.shape[-1])
    bf16_rows = 2 * SUBLANES
    mix_rows = _row_specs(tm, d, n, bf16_rows) * 3 + _row_specs(tm, d, n)
    y = pl.pallas_call(
        functools.partial(_out_ffn_kernel, tiles_per_seq=tiles_per_seq),
        grid=(n // tm,),
        in_specs=mix_rows + [row(p2.shape[-1])] + [_const_spec(a.shape) for a in (*w["mix_out"], *w["ffn"])],
        out_specs=row(d),
        out_shape=jax.ShapeDtypeStruct((n, d), F32),
        scratch_shapes=[pltpu.VMEM((tm, w["d_ff"]), BF16)],
        compiler_params=_params(("parallel",)),
        name="out_ffn",
    )(*((out_f,) * 3 + (out_b,) * 3 + (og,) * 3 + (x2,) * 3), p2, *w["mix_out"], *w["ffn"])
    return y.reshape(bsz, seq, d)


def kernel(x_prompt, x_sample, p_prompt, p_sample, norm_mix_w, w_in, mlstm_conv_w, mlstm_conv_b, mlstm_gate_b, ret_decay_logit, ret_gn_w, mlstm_gn_w, w_out, norm_ffn_w, ffn_w_gate, ffn_w_up, ffn_conv_w, ffn_conv_b, ffn_w_down, ple_w_proj, ple_norm_w, ple_w_gate, ple_gate_b, final_norm_w):
    depth = w_in.shape[0]
    d = x_prompt.shape[-1]
    d_ff = ffn_w_gate.shape[-1]
    assert d_ff % LANES == 0
    rowv = lambda a: a.reshape(1, -1).astype(F32)
    inv = ROPE_BASE ** (-jnp.arange(0, HD, 2, dtype=F32) / HD)
    inv = jnp.concatenate([inv, inv]).reshape(1, HD)
    max_seq = max(x_prompt.shape[1], x_sample.shape[1])
    assert max_seq % ROW_TILE == 0
    rope = pl.pallas_call(
        _rope_table_kernel,
        grid=(max_seq // ROW_TILE,),
        in_specs=[_const_spec(inv.shape)],
        out_specs=[pl.BlockSpec((ROW_TILE, HD), lambda i: (i, 0))] * 2,
        out_shape=[jax.ShapeDtypeStruct((max_seq, HD), F32)] * 2,
        scratch_shapes=[pltpu.VMEM((ROW_TILE, HD), F32)] * 2,
        compiler_params=_params(("arbitrary",)),
        name="rope_tables",
    )(inv)

    assert depth == 1
    outs = [x_prompt, x_sample]
    for l in range(depth):
        wi = w_in[l].astype(BF16)
        n_gate = 2 * HEADS
        wgate = jnp.zeros((d, 2 * LANES), BF16)
        wgate = wgate.at[:, 0:n_gate].set(wi[:, 8 * GW:8 * GW + n_gate])
        wgate = wgate.at[:, LANES:LANES + n_gate].set(wi[:, 8 * GW + n_gate:8 * GW + 2 * n_gate])
        gb = mlstm_gate_b[l].astype(F32).reshape(-1)
        gbias = jnp.zeros((1, 2 * LANES), F32)
        gbias = gbias.at[0, 0:n_gate].set(gb[:n_gate]).at[0, LANES:LANES + n_gate].set(gb[n_gate:])
        w = {
            "rope": rope,
            "in_proj": [rowv(norm_mix_w[l]), wi, wgate, gbias, mlstm_conv_w[l].astype(F32), rowv(mlstm_conv_b[l])],
            "ret_logit": jnp.repeat(ret_decay_logit[l].astype(F32), HD, axis=1),
            "d_ff": d_ff,
            "mix_out": [jnp.concatenate([rowv(ret_gn_w[l]), rowv(mlstm_gn_w[l])], axis=1), w_out[l].astype(BF16),
                        rowv(norm_ffn_w[l])],
            "ffn": [ffn_w_gate[l].astype(BF16), ffn_w_up[l].astype(BF16), ffn_conv_w[l].astype(F32),
                    rowv(ffn_conv_b[l]), ffn_w_down[l].astype(BF16), rowv(ple_norm_w[l]), ple_w_gate[l].astype(BF16),
                    rowv(ple_gate_b[l]), ple_w_proj[l].astype(BF16), rowv(final_norm_w)],
        }
        outs = [_layer(h, p[l], w) for h, p in zip(outs, (p_prompt, p_sample))]
    return tuple(outs)
```
